```python
import jax, jax.numpy as jnp
from jax import lax
import numpy as np

D_MODEL = 1024
BATCH = 1
SEQ = 16384
DEPTH = 2

D_FF = 2816
NORM_EPS = 1e-6
FFN_RES_WEIGHT = 0.5
GM_WIDTH = 512
GM_GROUPS = 4
GM_CHUNK = 128
RET_HEADS = 4
RET_DK = 64
RET_DV = 128
RET_CHUNK = 128
NSA_HEADS = 8
NSA_KV = 2
NSA_REP = NSA_HEADS // NSA_KV
NSA_DH = 64
CMP_LEN = 32
CMP_STRIDE = 16
CMP_HIDDEN = 128
SLC_BLOCK = 64
N_SELECT = 16
WINDOW = 512
Q_BLOCK = 128
N_NSA_BRANCH = 3
N_MIXERS = 3
MIX_WIDTH = 512
IN_SPLITS = (GM_WIDTH, GM_WIDTH, RET_HEADS * RET_DK, RET_HEADS * RET_DK, RET_HEADS * RET_DV, RET_HEADS * RET_DV, NSA_HEADS * NSA_DH, 6 * NSA_KV * NSA_DH, N_NSA_BRANCH * NSA_HEADS)
D_IN = sum(IN_SPLITS)
BIG = 1e9
NEG = -1e30

kernel_name = 'hybrid_gmlp_retention_nsa_macaron'


def rms_norm(x, g):
    xf = x.astype(jnp.float32)
    y = xf * lax.rsqrt(jnp.mean(xf * xf, axis=-1, keepdims=True) + NORM_EPS)
    return (y * g.astype(jnp.float32)).astype(x.dtype)


def layer_norm(x, g, b):
    xf = x.astype(jnp.float32)
    mu = jnp.mean(xf, axis=-1, keepdims=True)
    xc = xf - mu
    var = jnp.mean(xc * xc, axis=-1, keepdims=True)
    y = xc * lax.rsqrt(var + NORM_EPS) * g.astype(jnp.float32) + b.astype(jnp.float32)
    return y.astype(x.dtype)


def swiglu_ffn(x, w1, w2):
    a, b = jnp.split(x @ w1, 2, axis=-1)
    return (jax.nn.silu(a) * b) @ w2


def masked_softmax(s, mask):
    s = jnp.where(mask, s.astype(jnp.float32), NEG)
    m = jnp.max(s, axis=-1, keepdims=True)
    e = jnp.where(mask, jnp.exp(s - m), 0.0)
    return e / jnp.maximum(jnp.sum(e, axis=-1, keepdims=True), 1e-30)


def alibi_slopes():
    h = jnp.arange(1, NSA_HEADS + 1, dtype=jnp.float32)
    return (2.0 ** (-8.0 * h / NSA_HEADS)).reshape(NSA_KV, NSA_REP)


def gmlp_mixer(u, v, ln_g, ln_b, ws, bs):
    bsz, s, _ = u.shape
    nc = s // GM_CHUNK
    cg = GM_WIDTH // GM_GROUPS
    v = layer_norm(v, ln_g, ln_b).reshape(bsz, nc, GM_CHUNK, GM_GROUPS, cg)
    causal = jnp.tril(jnp.ones((GM_CHUNK, GM_CHUNK), dtype=bool))
    w = jnp.where(causal[None], ws, 0.0)
    sv = jnp.einsum('gts,bnsgc->bntgc', w, v) + jnp.transpose(bs)[None, None, :, :, None]
    return u * sv.reshape(bsz, s, GM_WIDTH)


def retention_mixer(q, k, v, g, gn_g, gn_b):
    f32 = jnp.float32
    bsz, s, _ = q.shape
    nc = s // RET_CHUNK
    q = q.reshape(bsz, nc, RET_CHUNK, RET_HEADS, RET_DK).astype(f32)
    k = (k.reshape(bsz, nc, RET_CHUNK, RET_HEADS, RET_DK) * RET_DK ** -0.5).astype(f32)
    v = v.reshape(bsz, nc, RET_CHUNK, RET_HEADS, RET_DV).astype(f32)
    log_gamma = jnp.log(1.0 - 2.0 ** (-5.0 - jnp.arange(RET_HEADS, dtype=f32)))
    pos = jnp.arange(RET_CHUNK, dtype=f32)
    diff = pos[:, None] - pos[None, :]
    intra_decay = jnp.where(diff >= 0, jnp.exp(log_gamma[:, None, None] * jnp.maximum(diff, 0.0)), 0.0)
    scores = jnp.einsum('bnihd,bnjhd->bnhij', q, k) * intra_decay
    intra = jnp.einsum('bnhij,bnjhe->bnihe', scores, v)
    k_decay = jnp.exp(log_gamma[None, :] * (RET_CHUNK - 1.0 - pos)[:, None])
    kv = jnp.einsum('bnjhd,jh,bnjhe->nbhde', k, k_decay, v)
    chunk_decay = jnp.exp(log_gamma * RET_CHUNK)[None, :, None, None]

    def step(state, kv_c):
        return state * chunk_decay + kv_c, state

    _, prev = lax.scan(step, jnp.zeros((bsz, RET_HEADS, RET_DK, RET_DV), f32), kv)
    q_decay = jnp.exp(log_gamma[None, :] * (pos + 1.0)[:, None])
    cross = jnp.einsum('bnihd,nbhde->bnihe', q, prev) * q_decay[None, None, :, :, None]
    y = layer_norm(intra + cross, gn_g.reshape(RET_HEADS, RET_DV), gn_b.reshape(RET_HEADS, RET_DV))
    return jax.nn.silu(g.astype(f32)) * y.reshape(bsz, s, RET_HEADS * RET_DV)


def nsa_mixer(q, kv, gate_logits, cmp_pos, cmp_w1, cmp_w2):
    f32 = jnp.float32
    bsz, s, _ = q.shape
    n_cmp = s // CMP_STRIDE - 1
    n_slc = s // SLC_BLOCK
    n_sel = min(N_SELECT, n_slc)
    q = q.reshape(bsz, s, NSA_KV, NSA_REP, NSA_DH) * NSA_DH ** -0.5
    k_c, v_c, k_s, v_s, k_w, v_w = [t.reshape(bsz, s, NSA_KV, NSA_DH) for t in jnp.split(kv, 6, axis=-1)]

    def compress(t, pos, w1, w2):
        seg = t.reshape(bsz, s // CMP_STRIDE, CMP_STRIDE, NSA_KV, NSA_DH)
        blocks = jnp.concatenate([seg[:, :-1], seg[:, 1:]], axis=2) + pos[None, None, :, None, :]
        flat = jnp.transpose(blocks, (0, 1, 3, 2, 4)).reshape(bsz, n_cmp, NSA_KV, CMP_LEN * NSA_DH)
        return jax.nn.gelu(flat @ w1) @ w2

    kc = compress(k_c, cmp_pos[0], cmp_w1[0], cmp_w2[0])
    vc = compress(v_c, cmp_pos[1], cmp_w1[1], cmp_w2[1])
    cmp_end = jnp.arange(n_cmp) * CMP_STRIDE + CMP_LEN - 1
    ci = jnp.arange(n_cmp)
    sj = jnp.arange(n_slc)
    overlap = ((ci[:, None] * CMP_STRIDE < (sj[None, :] + 1) * SLC_BLOCK) & (ci[:, None] * CMP_STRIDE + CMP_LEN > sj[None, :] * SLC_BLOCK)).astype(f32)
    ks_blocks = jnp.transpose(k_s.reshape(bsz, n_slc, SLC_BLOCK, NSA_KV, NSA_DH), (0, 3, 1, 2, 4))
    vs_blocks = jnp.transpose(v_s.reshape(bsz, n_slc, SLC_BLOCK, NSA_KV, NSA_DH), (0, 3, 1, 2, 4))
    kw_pad = jnp.pad(k_w, ((0, 0), (WINDOW, 0), (0, 0), (0, 0)))
    vw_pad = jnp.pad(v_w, ((0, 0), (WINDOW, 0), (0, 0), (0, 0)))
    gates = jax.nn.sigmoid(gate_logits.astype(f32)).reshape(bsz, s, NSA_KV, NSA_REP, N_NSA_BRANCH)
    slopes = alibi_slopes()
    bi = jnp.arange(bsz)[:, None, None, None]
    gi = jnp.arange(NSA_KV)[None, :, None, None]

    def block_fn(qb):
        t0 = qb * Q_BLOCK
        qblk = lax.dynamic_slice_in_dim(q, t0, Q_BLOCK, axis=1)
        tpos = t0 + jnp.arange(Q_BLOCK)
        s_c = jnp.einsum('bqgrd,bngd->bgrqn', qblk, kc)
        dist_c = tpos[:, None] - cmp_end[None, :]
        s_c = s_c - slopes[None, :, :, None, None] * dist_c.astype(f32)
        p_c = masked_softmax(s_c, dist_c >= 0)
        o_c = jnp.einsum('bgrqn,bngd->bqgrd', p_c, vc)
        imp = jnp.einsum('bgrqn,nj->bgqj', p_c, overlap)
        cur = tpos // SLC_BLOCK
        jj = sj[None, :]
        forced = (jj == 0) | (jj == cur[:, None]) | (jj == cur[:, None] - 1)
        imp = jnp.where(forced, BIG, imp)
        imp = jnp.where(jj > cur[:, None], -BIG, imp)
        _, sel = lax.top_k(imp, n_sel)
        ks = ks_blocks[bi, gi, sel]
        vs = vs_blocks[bi, gi, sel]
        s_s = jnp.einsum('bqgrd,bgqnkd->bgrqnk', qblk, ks)
        kpos = sel[..., None] * SLC_BLOCK + jnp.arange(SLC_BLOCK)
        dist_s = (tpos[None, None, :, None, None] - kpos)[:, :, None]
        s_s = s_s - slopes[None, :, :, None, None, None] * dist_s.astype(f32)
        m_tot = n_sel * SLC_BLOCK
        p_s = masked_softmax(s_s.reshape(bsz, NSA_KV, NSA_REP, Q_BLOCK, m_tot), (dist_s >= 0).reshape(bsz, NSA_KV, 1, Q_BLOCK, m_tot))
        o_s = jnp.einsum('bgrqm,bgqmd->bqgrd', p_s, vs.reshape(bsz, NSA_KV, Q_BLOCK, m_tot, NSA_DH))
        kw = lax.dynamic_slice_in_dim(kw_pad, t0, Q_BLOCK + WINDOW, axis=1)
        vw = lax.dynamic_slice_in_dim(vw_pad, t0, Q_BLOCK + WINDOW, axis=1)
        wpos = t0 - WINDOW + jnp.arange(Q_BLOCK + WINDOW)
        dist_w = tpos[:, None] - wpos[None, :]
        mask_w = (dist_w >= 0) & (dist_w < WINDOW) & (wpos[None, :] >= 0)
        s_w = jnp.einsum('bqgrd,bkgd->bgrqk', qblk, kw) - slopes[None, :, :, None, None] * dist_w.astype(f32)
        p_w = masked_softmax(s_w, mask_w)
        o_w = jnp.einsum('bgrqk,bkgd->bqgrd', p_w, vw)
        g = lax.dynamic_slice_in_dim(gates, t0, Q_BLOCK, axis=1)
        return g[..., 0:1] * o_c + g[..., 1:2] * o_s + g[..., 2:3] * o_w

    out = lax.map(block_fn, jnp.arange(s // Q_BLOCK))
    return jnp.transpose(out, (1, 0, 2, 3, 4, 5)).reshape(bsz, s, NSA_HEADS * NSA_DH)


def setup_inputs(seed: int = 0) -> dict:
    key = jax.random.key(seed)
    ks = jax.random.split(key, 24)
    L = DEPTH
    f32 = jnp.float32

    def nrm(k, shape, scale):
        return jax.random.normal(k, shape, f32) * scale

    def gain(k, shape):
        return 1.0 + 0.02 * jax.random.normal(k, shape, f32)

    return {
        'x': nrm(ks[0], (BATCH, SEQ, D_MODEL), 1.0),
        'ffn1_norm': gain(ks[1], (L, D_MODEL)),
        'ffn1_w1': nrm(ks[2], (L, D_MODEL, 2 * D_FF), D_MODEL ** -0.5),
        'ffn1_w2': nrm(ks[3], (L, D_FF, D_MODEL), D_FF ** -0.5),
        'mix_norm': gain(ks[4], (L, D_MODEL)),
        'w_in': nrm(ks[5], (L, D_MODEL, D_IN), D_MODEL ** -0.5),
        'gm_ln_g': gain(ks[6], (L, GM_WIDTH)),
        'gm_ln_b': nrm(ks[7], (L, GM_WIDTH), 0.02),
        'gm_ws': nrm(ks[8], (L, GM_GROUPS, GM_CHUNK, GM_CHUNK), GM_CHUNK ** -0.5),
        'gm_bs': gain(ks[9], (L, GM_GROUPS, GM_CHUNK)),
        'ret_gn_g': gain(ks[10], (L, RET_HEADS * RET_DV)),
        'ret_gn_b': nrm(ks[11], (L, RET_HEADS * RET_DV), 0.02),
        'cmp_pos': nrm(ks[12], (L, 2, CMP_LEN, NSA_DH), 0.02),
        'cmp_w1': nrm(ks[13], (L, 2, CMP_LEN * NSA_DH, CMP_HIDDEN), (CMP_LEN * NSA_DH) ** -0.5),
        'cmp_w2': nrm(ks[14], (L, 2, CMP_HIDDEN, NSA_DH), CMP_HIDDEN ** -0.5),
        'w_branch_out': nrm(ks[15], (L, N_MIXERS, MIX_WIDTH, D_MODEL), MIX_WIDTH ** -0.5),
        'w_merge_gate': nrm(ks[16], (L, D_MODEL, N_MIXERS * D_MODEL), D_MODEL ** -0.5),
        'b_merge_gate': nrm(ks[17], (L, N_MIXERS * D_MODEL), 0.02),
        'w_o': nrm(ks[18], (L, D_MODEL, D_MODEL), D_MODEL ** -0.5),
        'ffn2_norm': gain(ks[19], (L, D_MODEL)),
        'ffn2_w1': nrm(ks[20], (L, D_MODEL, 2 * D_FF), D_MODEL ** -0.5),
        'ffn2_w2': nrm(ks[21], (L, D_FF, D_MODEL), D_FF ** -0.5),
        'final_norm': gain(ks[22], (D_MODEL,)),
    }


def reference(x, ffn1_norm, ffn1_w1, ffn1_w2, mix_norm, w_in, gm_ln_g, gm_ln_b, gm_ws, gm_bs, ret_gn_g, ret_gn_b, cmp_pos, cmp_w1, cmp_w2, w_branch_out, w_merge_gate, b_merge_gate, w_o, ffn2_norm, ffn2_w1, ffn2_w2, final_norm):
    splits = np.cumsum(IN_SPLITS)[:-1].tolist()
    for l in range(DEPTH):
        x = x + FFN_RES_WEIGHT * swiglu_ffn(rms_norm(x, ffn1_norm[l]), ffn1_w1[l], ffn1_w2[l])
        h = rms_norm(x, mix_norm[l])
        proj = h @ w_in[l]
        gm_u, gm_v, r_q, r_k, r_v, r_g, n_q, n_kv, n_g = jnp.split(proj, splits, axis=-1)
        y_a = gmlp_mixer(jax.nn.gelu(gm_u), jax.nn.gelu(gm_v), gm_ln_g[l], gm_ln_b[l], gm_ws[l], gm_bs[l])
        y_b = retention_mixer(r_q, r_k, r_v, r_g, ret_gn_g[l], ret_gn_b[l])
        y_c = nsa_mixer(n_q, n_kv, n_g, cmp_pos[l], cmp_w1[l], cmp_w2[l])
        g_a, g_b, g_c = jnp.split(jax.nn.sigmoid(h @ w_merge_gate[l] + b_merge_gate[l]), N_MIXERS, axis=-1)
        mix = g_a * (y_a @ w_branch_out[l, 0]) + g_b * (y_b @ w_branch_out[l, 1]) + g_c * (y_c @ w_branch_out[l, 2])
        x = x + mix @ w_o[l]
        x = x + FFN_RES_WEIGHT * swiglu_ffn(rms_norm(x, ffn2_norm[l]), ffn2_w1[l], ffn2_w2[l])
    return rms_norm(x, final_norm)
```

```python
import functools
import math

import jax
import jax.numpy as jnp
from jax import lax
from jax.experimental import pallas as pl
from jax.experimental.pallas import tpu as pltpu

F32 = jnp.float32
BF16 = jnp.bfloat16

D_MODEL = 1024
D_FF = 2816
NORM_EPS = 1e-6
FFN_RES_WEIGHT = 0.5
GM_WIDTH = 512
GM_GROUPS = 4
GM_CHUNK = 128
RET_HEADS = 4
RET_DK = 64
RET_DV = 128
RET_CHUNK = 128
NSA_HEADS = 8
NSA_KV = 2
NSA_REP = NSA_HEADS // NSA_KV
NSA_DH = 64
CMP_LEN = 32
CMP_STRIDE = 16
CMP_HIDDEN = 128
SLC_BLOCK = 64
N_SELECT = 16
WINDOW = 512
Q_BLOCK = 128
N_MIXERS = 3
MIX_WIDTH = 512
IN_SPLITS = (512, 512, 256, 256, 512, 512, 512, 768, 24)
D_IN = sum(IN_SPLITS)
D_IN_PAD = 4096
BIG = 1e9
NEG = -1e30
PICKED = -3e38

OFF_GM_U, OFF_GM_V, OFF_RQ, OFF_RK, OFF_RV, OFF_RG, OFF_NQ, OFF_NKV, OFF_NG = (
    0, 512, 1024, 1280, 1536, 2048, 2560, 3072, 3840)

QCOLS = NSA_REP * Q_BLOCK
SEL_TILE = 512
SEL_TILE_BLOCKS = SEL_TILE // SLC_BLOCK
KAUG = 128
WIN_KEYS = WINDOW + Q_BLOCK
VMEM_LIMIT = 56 * 1024 * 1024


def _cparams(sem, vmem=VMEM_LIMIT):
    return pltpu.CompilerParams(dimension_semantics=sem, vmem_limit_bytes=vmem)


def _rms(x, g):
    return x * lax.rsqrt(jnp.mean(x * x, axis=-1, keepdims=True) + NORM_EPS) * g


def _dot(a, b):
    return jnp.dot(a, b, preferred_element_type=F32)


def _ffn_kernel(x_ref, g_ref, w1a_ref, w1b_ref, w2_ref, fin_ref, o_ref, h_ref, acc_ref, *, final_norm):
    f = pl.program_id(1)

    @pl.when(f == 0)
    def _():
        h_ref[...] = _rms(x_ref[...], g_ref[...]).astype(BF16)
        acc_ref[...] = jnp.zeros_like(acc_ref)

    h = h_ref[...]
    a = _dot(h, w1a_ref[...])
    b = _dot(h, w1b_ref[...])
    act = (jax.nn.silu(a) * b).astype(BF16)
    acc_ref[...] += _dot(act, w2_ref[...])

    @pl.when(f == pl.num_programs(1) - 1)
    def _():
        y = x_ref[...] + FFN_RES_WEIGHT * acc_ref[...]
        if final_norm:
            y = _rms(y, fin_ref[...])
        o_ref[...] = y


def _ffn(x, g, w1, w2, fin, final_norm, tm=1024, tf=256):
    s = x.shape[0]
    nf = D_FF // tf
    return pl.pallas_call(
        functools.partial(_ffn_kernel, final_norm=final_norm),
        out_shape=jax.ShapeDtypeStruct((s, D_MODEL), F32),
        grid=(s // tm, nf),
        in_specs=[
            pl.BlockSpec((tm, D_MODEL), lambda i, f: (i, 0)),
            pl.BlockSpec((1, D_MODEL), lambda i, f: (0, 0)),
            pl.BlockSpec((D_MODEL, tf), lambda i, f: (0, f)),
            pl.BlockSpec((D_MODEL, tf), lambda i, f: (0, f + nf)),
            pl.BlockSpec((tf, D_MODEL), lambda i, f: (f, 0)),
            pl.BlockSpec((1, D_MODEL), lambda i, f: (0, 0)),
        ],
        out_specs=pl.BlockSpec((tm, D_MODEL), lambda i, f: (i, 0)),
        scratch_shapes=[pltpu.VMEM((tm, D_MODEL), BF16), pltpu.VMEM((tm, D_MODEL), F32)],
        compiler_params=_cparams(("arbitrary", "arbitrary")),
        name="ffn",
    )(x, g, w1, w1, w2, fin)


def _proj_kernel(x_ref, g_ref, w_ref, o_ref, h_ref):
    @pl.when(pl.program_id(1) == 0)
    def _():
        h_ref[...] = _rms(x_ref[...], g_ref[...]).astype(BF16)

    o_ref[...] = _dot(h_ref[...], w_ref[...])


def _proj(x, g, w, tm=1024, tn=512):
    s = x.shape[0]
    return pl.pallas_call(
        _proj_kernel,
        out_shape=jax.ShapeDtypeStruct((s, D_IN_PAD), F32),
        grid=(s // tm, D_IN_PAD // tn),
        in_specs=[
            pl.BlockSpec((tm, D_MODEL), lambda i, n: (i, 0)),
            pl.BlockSpec((1, D_MODEL), lambda i, n: (0, 0)),
            pl.BlockSpec((D_MODEL, tn), lambda i, n: (0, n)),
        ],
        out_specs=pl.BlockSpec((tm, tn), lambda i, n: (i, n)),
        scratch_shapes=[pltpu.VMEM((tm, D_MODEL), BF16)],
        compiler_params=_cparams(("arbitrary", "arbitrary")),
        name="proj",
    )(x, g, w)


def _gmlp_kernel(u_ref, v_ref, lg_ref, lb_ref, ws_ref, bst_ref, o_ref, *, chunks):
    cg = GM_WIDTH // GM_GROUPS
    u = jax.nn.gelu(u_ref[...])
    v = jax.nn.gelu(v_ref[...])
    mu = jnp.mean(v, axis=-1, keepdims=True)
    vc = v - mu
    var = jnp.mean(vc * vc, axis=-1, keepdims=True)
    vn = (vc * lax.rsqrt(var + NORM_EPS) * lg_ref[...] + lb_ref[...]).astype(BF16)
    row = lax.broadcasted_iota(jnp.int32, (GM_CHUNK, GM_CHUNK), 0)
    col = lax.broadcasted_iota(jnp.int32, (GM_CHUNK, GM_CHUNK), 1)
    causal = row >= col
    bst = bst_ref[...]
    for gi in range(GM_GROUPS):
        w = jnp.where(causal, ws_ref[gi], 0.0).astype(BF16)
        bias = bst[:, gi:gi + 1]
        for c in range(chunks):
            rs = slice(c * GM_CHUNK, (c + 1) * GM_CHUNK)
            cs = slice(gi * cg, (gi + 1) * cg)
            sv = _dot(w, vn[rs, cs]) + bias
            o_ref[rs, cs] = (u[rs, cs] * sv).astype(o_ref.dtype)


def _gmlp(proj, ln_g, ln_b, ws, bst, tm=512):
    s = proj.shape[0]
    return pl.pallas_call(
        functools.partial(_gmlp_kernel, chunks=tm // GM_CHUNK),
        out_shape=jax.ShapeDtypeStruct((s, GM_WIDTH), BF16),
        grid=(s // tm,),
        in_specs=[
            pl.BlockSpec((tm, GM_WIDTH), lambda i: (i, OFF_GM_U // GM_WIDTH)),
            pl.BlockSpec((tm, GM_WIDTH), lambda i: (i, OFF_GM_V // GM_WIDTH)),
            pl.BlockSpec((1, GM_WIDTH), lambda i: (0, 0)),
            pl.BlockSpec((1, GM_WIDTH), lambda i: (0, 0)),
            pl.BlockSpec((GM_GROUPS, GM_CHUNK, GM_CHUNK), lambda i: (0, 0, 0)),
            pl.BlockSpec((GM_CHUNK, GM_GROUPS), lambda i: (0, 0)),
        ],
        out_specs=pl.BlockSpec((tm, GM_WIDTH), lambda i: (i, 0)),
        compiler_params=_cparams(("arbitrary",)),
        name="gmlp",
    )(proj, proj, ln_g, ln_b, ws, bst)


def _ret_kernel(q_ref, k_ref, v_ref, g_ref, gg_ref, gb_ref, o_ref, st_ref, *, chunks):
    @pl.when(pl.program_id(0) == 0)
    def _():
        st_ref[...] = jnp.zeros_like(st_ref)

    c = RET_CHUNK
    ri = lax.broadcasted_iota(jnp.int32, (c, c), 0)
    ci = lax.broadcasted_iota(jnp.int32, (c, c), 1)
    diff = (ri - ci).astype(F32)
    pos_k = lax.broadcasted_iota(jnp.int32, (c, RET_DK), 0).astype(F32)
    pos_v = lax.broadcasted_iota(jnp.int32, (c, RET_DV), 0).astype(F32)
    for h in range(RET_HEADS):
        log_gamma = math.log(1.0 - 2.0 ** (-5.0 - h))
        intra_decay = jnp.where(diff >= 0, jnp.exp(log_gamma * jnp.maximum(diff, 0.0)), 0.0)
        k_decay = jnp.exp(log_gamma * (c - 1.0 - pos_k))
        q_decay = jnp.exp(log_gamma * (pos_v + 1.0))
        chunk_decay = math.exp(log_gamma * c)
        gam = gg_ref[:, h * RET_DV:(h + 1) * RET_DV]
        bet = gb_ref[:, h * RET_DV:(h + 1) * RET_DV]
        for n in range(chunks):
            rs = slice(n * c, (n + 1) * c)
            q = q_ref[rs, h * RET_DK:(h + 1) * RET_DK].astype(BF16)
            kf = k_ref[rs, h * RET_DK:(h + 1) * RET_DK] * (RET_DK ** -0.5)
            k = kf.astype(BF16)
            v = v_ref[rs, h * RET_DV:(h + 1) * RET_DV].astype(BF16)
            scores = lax.dot_general(q, k, (((1,), (1,)), ((), ())), preferred_element_type=F32) * intra_decay
            intra = _dot(scores.astype(BF16), v)
            state = st_ref[h]
            cross = _dot(q, state.astype(BF16)) * q_decay
            kd = (kf * k_decay).astype(BF16)
            kv = lax.dot_general(kd, v, (((0,), (0,)), ((), ())), preferred_element_type=F32)
            st_ref[h] = state * chunk_decay + kv
            y = intra + cross
            mu = jnp.mean(y, axis=-1, keepdims=True)
            yc = y - mu
            var = jnp.mean(yc * yc, axis=-1, keepdims=True)
            yn = yc * lax.rsqrt(var + NORM_EPS) * gam + bet
            gate = g_ref[rs, h * RET_DV:(h + 1) * RET_DV]
            o_ref[rs, h * RET_DV:(h + 1) * RET_DV] = (jax.nn.silu(gate) * yn).astype(o_ref.dtype)


def _retention(proj, gn_g, gn_b, tm=512):
    s = proj.shape[0]
    hk = RET_HEADS * RET_DK
    hv = RET_HEADS * RET_DV
    return pl.pallas_call(
        functools.partial(_ret_kernel, chunks=tm // RET_CHUNK),
        out_shape=jax.ShapeDtypeStruct((s, hv), BF16),
        grid=(s // tm,),
        in_specs=[
            pl.BlockSpec((tm, hk), lambda i: (i, OFF_RQ // hk)),
            pl.BlockSpec((tm, hk), lambda i: (i, OFF_RK // hk)),
            pl.BlockSpec((tm, hv), lambda i: (i, OFF_RV // hv)),
            pl.BlockSpec((tm, hv), lambda i: (i, OFF_RG // hv)),
            pl.BlockSpec((1, hv), lambda i: (0, 0)),
            pl.BlockSpec((1, hv), lambda i: (0, 0)),
        ],
        out_specs=pl.BlockSpec((tm, hv), lambda i: (i, 0)),
        scratch_shapes=[pltpu.VMEM((RET_HEADS, RET_DK, RET_DV), F32)],
        compiler_params=_cparams(("arbitrary",)),
        name="retention",
    )(proj, proj, proj, proj, gn_g, gn_b)


def _compress_kernel(seg_ref, pos_ref, w1_ref, w2_ref, o_ref):
    seg = seg_ref[0, 0]
    n = seg.shape[0]
    half = CMP_STRIDE * NSA_DH
    lo = _dot((seg + pos_ref[0, 0:1, :]).astype(BF16), w1_ref[0, 0:half, :])
    hi = _dot((seg + pos_ref[0, 1:2, :]).astype(BF16), w1_ref[0, half:2 * half, :])
    hi_next = jnp.concatenate([hi[1:], hi[:1]], axis=0)
    hid = jax.nn.gelu(lo + hi_next).astype(BF16)
    out = _dot(hid, w2_ref[0])
    rowi = lax.broadcasted_iota(jnp.int32, out.shape, 0)
    o_ref[0, 0] = jnp.where(rowi < n - 1, out, 0.0).astype(o_ref.dtype)


def _compress(segs, posflat, w1, w2):
    _, kv, n, width = segs.shape
    return pl.pallas_call(
        _compress_kernel,
        out_shape=jax.ShapeDtypeStruct((2, kv, n, NSA_DH), BF16),
        grid=(2, kv),
        in_specs=[
            pl.BlockSpec((1, 1, n, width), lambda t, g: (t, g, 0, 0)),
            pl.BlockSpec((1, 2, width), lambda t, g: (t, 0, 0)),
            pl.BlockSpec((1, 2 * width, CMP_HIDDEN), lambda t, g: (t, 0, 0)),
            pl.BlockSpec((1, CMP_HIDDEN, NSA_DH), lambda t, g: (t, 0, 0)),
        ],
        out_specs=pl.BlockSpec((1, 1, n, NSA_DH), lambda t, g: (t, g, 0, 0)),
        compiler_params=_cparams(("arbitrary", "arbitrary")),
        name="nsa_compress",
    )(segs, posflat, w1, w2)


def _nsa_cmp_kernel(qt_ref, kc_ref, vct_ref, ovt_ref, slope_ref, oc_ref, selb_ref, any_ref, a_ref):
    qb = pl.program_id(1)
    ncmp = kc_ref.shape[1]
    nslc = ovt_ref.shape[0]
    slope = slope_ref[0]

    @pl.when(qb == 0)
    def _():
        c = lax.broadcasted_iota(jnp.int32, (ncmp, QCOLS), 0)
        q = lax.broadcasted_iota(jnp.int32, (ncmp, QCOLS), 1) & (Q_BLOCK - 1)
        a_ref[...] = slope * (q - c * CMP_STRIDE - (CMP_LEN - 1)).astype(F32)

    t0 = qb * Q_BLOCK
    qt = qt_ref[0, 0] * (NSA_DH ** -0.5)
    s = _dot(kc_ref[0], qt.astype(BF16))
    z = s - a_ref[...] - slope * t0.astype(F32)
    c = lax.broadcasted_iota(jnp.int32, (ncmp, QCOLS), 0)
    q = lax.broadcasted_iota(jnp.int32, (ncmp, QCOLS), 1) & (Q_BLOCK - 1)
    mask = (q - c * CMP_STRIDE) >= (CMP_LEN - 1) - t0
    z = jnp.where(mask, z, NEG)
    m = jnp.max(z, axis=0, keepdims=True)
    e = jnp.where(mask, jnp.exp(z - m), 0.0)
    p = e / jnp.maximum(jnp.sum(e, axis=0, keepdims=True), 1e-30)
    pb = p.astype(BF16)
    oc_ref[0, 0] = _dot(vct_ref[0], pb)

    imp4 = _dot(ovt_ref[...], pb)
    imp = imp4[:, 0:Q_BLOCK]
    for r in range(1, NSA_REP):
        imp = imp + imp4[:, r * Q_BLOCK:(r + 1) * Q_BLOCK]
    j = lax.broadcasted_iota(jnp.int32, (nslc, Q_BLOCK), 0)
    qq = lax.broadcasted_iota(jnp.int32, (nslc, Q_BLOCK), 1)
    cur = 2 * qb + jnp.where(qq >= SLC_BLOCK, 1, 0)
    forced = (j == 0) | (j == cur) | (j == cur - 1)
    imp = jnp.where(forced, BIG, imp)
    imp = jnp.where(j > cur, -BIG, imp)
    jf = j.astype(F32)

    def pick(_, carry):
        imp, sel = carry
        best = jnp.max(imp, axis=0, keepdims=True)
        first = jnp.min(jnp.where(imp == best, jf, float(nslc)), axis=0, keepdims=True)
        hit = jf == first
        return jnp.where(hit, PICKED, imp), jnp.where(hit, 1.0, sel)

    _, sel = lax.fori_loop(0, min(N_SELECT, nslc), pick, (imp, jnp.zeros((nslc, Q_BLOCK), F32)))
    selb_ref[0, 0] = jnp.where(sel > 0.0, 0.0, NEG)
    ntile = nslc // SEL_TILE_BLOCKS
    ti = lax.broadcasted_iota(jnp.int32, (ntile, nslc), 0) * SEL_TILE_BLOCKS
    tj = lax.broadcasted_iota(jnp.int32, (ntile, nslc), 1)
    member = jnp.where((tj >= ti) & (tj < ti + SEL_TILE_BLOCKS), 1.0, 0.0).astype(BF16)
    count = _dot(member, sel.astype(BF16))
    any_ref[0, 0] = jnp.broadcast_to(jnp.max(count, axis=1, keepdims=True), (ntile, Q_BLOCK))


def _nsa_cmp(qt, kc, vct, ovt, slopes):
    kv, nqb = qt.shape[0], qt.shape[1]
    ncmp = kc.shape[1]
    nslc = ovt.shape[0]
    ntile = nslc // SEL_TILE_BLOCKS
    return pl.pallas_call(
        _nsa_cmp_kernel,
        out_shape=[
            jax.ShapeDtypeStruct((kv, nqb, NSA_DH, QCOLS), F32),
            jax.ShapeDtypeStruct((kv, nqb, nslc, Q_BLOCK), F32),
            jax.ShapeDtypeStruct((kv, nqb, ntile, Q_BLOCK), F32),
        ],
        grid=(kv, nqb),
        in_specs=[
            pl.BlockSpec((1, 1, NSA_DH, QCOLS), lambda g, b: (g, b, 0, 0)),
            pl.BlockSpec((1, ncmp, NSA_DH), lambda g, b: (g, 0, 0)),
            pl.BlockSpec((1, NSA_DH, ncmp), lambda g, b: (g, 0, 0)),
            pl.BlockSpec((nslc, ncmp), lambda g, b: (0, 0)),
            pl.BlockSpec((1, 1, QCOLS), lambda g, b: (g, 0, 0)),
        ],
        out_specs=[
            pl.BlockSpec((1, 1, NSA_DH, QCOLS), lambda g, b: (g, b, 0, 0)),
            pl.BlockSpec((1, 1, nslc, Q_BLOCK), lambda g, b: (g, b, 0, 0)),
            pl.BlockSpec((1, 1, ntile, Q_BLOCK), lambda g, b: (g, b, 0, 0)),
        ],
        scratch_shapes=[pltpu.VMEM((ncmp, QCOLS), F32)],
        compiler_params=_cparams(("arbitrary", "arbitrary")),
        name="nsa_cmp_topk",
    )(qt, kc, vct, ovt, slopes)


def _nsa_attn_kernel(any_ref, qt_ref, ksa_ref, vst_ref, kw_ref, vwt_ref, selb_ref, oc_ref, gl_ref, slope_ref,
                     o_ref, qaug_ref, a_ref, aw_ref, m_ref, l_ref, acc_ref):
    g = pl.program_id(0)
    qb = pl.program_id(1)
    nqb = pl.num_programs(1)
    ntile = selb_ref.shape[2] // SEL_TILE_BLOCKS
    slope = slope_ref[0]

    @pl.when(qb == 0)
    def _():
        c = lax.broadcasted_iota(jnp.int32, (SEL_TILE, QCOLS), 0)
        q = lax.broadcasted_iota(jnp.int32, (SEL_TILE, QCOLS), 1) & (Q_BLOCK - 1)
        a_ref[...] = slope * (q - c).astype(F32)
        cw = lax.broadcasted_iota(jnp.int32, (WIN_KEYS, QCOLS), 0)
        qw = lax.broadcasted_iota(jnp.int32, (WIN_KEYS, QCOLS), 1) & (Q_BLOCK - 1)
        dist = qw - cw + WINDOW
        aw_ref[...] = jnp.where((dist >= 0) & (dist < WINDOW), slope * dist.astype(F32), -NEG)
        qaug_ref[...] = jnp.zeros_like(qaug_ref)

    t0 = qb * Q_BLOCK
    qt = (qt_ref[0, 0] * (NSA_DH ** -0.5)).astype(BF16)
    qaug_ref[0:NSA_DH, :] = qt
    m_ref[...] = jnp.full_like(m_ref, NEG)
    l_ref[...] = jnp.zeros_like(l_ref)
    acc_ref[...] = jnp.zeros_like(acc_ref)

    def sel_tile(kt, diagonal):
        sb = selb_ref[0, 0, pl.ds(pl.multiple_of(kt * SEL_TILE_BLOCKS, SEL_TILE_BLOCKS), SEL_TILE_BLOCKS), :]
        sb = jnp.concatenate([sb] * NSA_REP, axis=1)
        qaug_ref[NSA_DH:NSA_DH + 16, :] = jnp.concatenate([sb, jnp.zeros_like(sb)], axis=0).astype(BF16)
        k = ksa_ref[0, pl.ds(pl.multiple_of(kt * SEL_TILE, SEL_TILE), SEL_TILE), :]
        s = _dot(k, qaug_ref[...])
        off = t0 - kt * SEL_TILE
        rv = slope * off.astype(F32)
        z = s - a_ref[...]
        if diagonal:
            c = lax.broadcasted_iota(jnp.int32, (SEL_TILE, QCOLS), 0)
            q = lax.broadcasted_iota(jnp.int32, (SEL_TILE, QCOLS), 1) & (Q_BLOCK - 1)
            z = jnp.where(c - q <= off, z, NEG)
        m_old = m_ref[...]
        m_new = jnp.maximum(m_old, jnp.max(z, axis=0, keepdims=True) - rv)
        p = jnp.exp(z - (m_new + rv))
        alpha = jnp.exp(m_old - m_new)
        l_ref[...] = alpha * l_ref[...] + jnp.sum(p, axis=0, keepdims=True)
        acc_ref[...] = alpha * acc_ref[...] + _dot(vst_ref[0, kt], p.astype(BF16))
        m_ref[...] = m_new

    kd = t0 // SEL_TILE

    def body(kt, carry):
        @pl.when(any_ref[(g * nqb + qb) * ntile + kt] > 0)
        def _():
            sel_tile(kt, False)
        return carry

    lax.fori_loop(0, kd, body, 0)
    sel_tile(kd, True)
    o_sel = acc_ref[...] / l_ref[...]

    first = qb - WINDOW // Q_BLOCK
    idx = [jnp.maximum(first + i, 0) for i in range(WIN_KEYS // Q_BLOCK)]
    kw = jnp.concatenate([kw_ref[0, i] for i in idx], axis=0)
    vwt = jnp.concatenate([vwt_ref[0, i] for i in idx], axis=1)
    s = _dot(kw, qt)
    cw = lax.broadcasted_iota(jnp.int32, (WIN_KEYS, QCOLS), 0)
    z = jnp.where(cw >= WINDOW - t0, s - aw_ref[...], NEG)
    m = jnp.max(z, axis=0, keepdims=True)
    p = jnp.exp(z - m)
    o_win = _dot(vwt, p.astype(BF16)) / jnp.sum(p, axis=0, keepdims=True)

    gates = jax.nn.sigmoid(gl_ref[0, 0])
    out = gates[0:1] * oc_ref[0, 0] + gates[1:2] * o_sel + gates[2:3] * o_win
    o_ref[0, 0] = out.astype(o_ref.dtype)


def _nsa_attn(tile_any, qt, ksa, vst, kw, vwt, selb, oc, gl, slopes):
    kv, nqb = qt.shape[0], qt.shape[1]
    s = ksa.shape[1]
    nslc = selb.shape[2]
    grid_spec = pltpu.PrefetchScalarGridSpec(
        num_scalar_prefetch=1,
        grid=(kv, nqb),
        in_specs=[
            pl.BlockSpec((1, 1, NSA_DH, QCOLS), lambda g, b, a: (g, b, 0, 0)),
            pl.BlockSpec((1, s, KAUG), lambda g, b, a: (g, 0, 0)),
            pl.BlockSpec((1, s // SEL_TILE, NSA_DH, SEL_TILE), lambda g, b, a: (g, 0, 0, 0)),
            pl.BlockSpec((1, s // Q_BLOCK, Q_BLOCK, NSA_DH), lambda g, b, a: (g, 0, 0, 0)),
            pl.BlockSpec((1, s // Q_BLOCK, NSA_DH, Q_BLOCK), lambda g, b, a: (g, 0, 0, 0)),
            pl.BlockSpec((1, 1, nslc, Q_BLOCK), lambda g, b, a: (g, b, 0, 0)),
            pl.BlockSpec((1, 1, NSA_DH, QCOLS), lambda g, b, a: (g, b, 0, 0)),
            pl.BlockSpec((1, 1, 3, QCOLS), lambda g, b, a: (g, b, 0, 0)),
            pl.BlockSpec((1, 1, QCOLS), lambda g, b, a: (g, 0, 0)),
        ],
        out_specs=pl.BlockSpec((1, 1, NSA_DH, QCOLS), lambda g, b, a: (g, b, 0, 0)),
        scratch_shapes=[
            pltpu.VMEM((KAUG, QCOLS), BF16),
            pltpu.VMEM((SEL_TILE, QCOLS), F32),
            pltpu.VMEM((WIN_KEYS, QCOLS), F32),
            pltpu.VMEM((1, QCOLS), F32),
            pltpu.VMEM((1, QCOLS), F32),
            pltpu.VMEM((NSA_DH, QCOLS), F32),
        ],
    )
    return pl.pallas_call(
        _nsa_attn_kernel,
        out_shape=jax.ShapeDtypeStruct((kv, nqb, NSA_DH, QCOLS), BF16),
        grid_spec=grid_spec,
        compiler_params=_cparams(("arbitrary", "arbitrary")),
        name="nsa_attn",
    )(tile_any, qt, ksa, vst, kw, vwt, selb, oc, gl, slopes)


def _merge_kernel(x_ref, g_ref, ya_ref, yb_ref, yc_ref, wg_ref, bg_ref, wb_ref, wo_ref, o_ref):
    x = x_ref[...]
    h = _rms(x, g_ref[...]).astype(BF16)
    mix = None
    for mi, y_ref in enumerate((ya_ref, yb_ref, yc_ref)):
        cs = slice(mi * D_MODEL, (mi + 1) * D_MODEL)
        gate = jax.nn.sigmoid(_dot(h, wg_ref[:, cs]) + bg_ref[:, cs])
        term = gate * _dot(y_ref[...], wb_ref[mi])
        mix = term if mix is None else mix + term
    o_ref[...] = x + _dot(mix.astype(BF16), wo_ref[...])


def _merge(x, g, ya, yb, yc, wg, bg, wb, wo, tm=256):
    s = x.shape[0]
    row = lambda i: (i, 0)
    fixed2 = lambda i: (0, 0)
    return pl.pallas_call(
        _merge_kernel,
        out_shape=jax.ShapeDtypeStruct((s, D_MODEL), F32),
        grid=(s // tm,),
        in_specs=[
            pl.BlockSpec((tm, D_MODEL), row),
            pl.BlockSpec((1, D_MODEL), fixed2),
            pl.BlockSpec((tm, MIX_WIDTH), row),
            pl.BlockSpec((tm, MIX_WIDTH), row),
            pl.BlockSpec((tm, MIX_WIDTH), row),
            pl.BlockSpec((D_MODEL, N_MIXERS * D_MODEL), fixed2),
            pl.BlockSpec((1, N_MIXERS * D_MODEL), fixed2),
            pl.BlockSpec((N_MIXERS, MIX_WIDTH, D_MODEL), lambda i: (0, 0, 0)),
            pl.BlockSpec((D_MODEL, D_MODEL), fixed2),
        ],
        out_specs=pl.BlockSpec((tm, D_MODEL), row),
        compiler_params=_cparams(("arbitrary",)),
        name="merge",
    )(x, g, ya, yb, yc, wg, bg, wb, wo)


def _alibi_slope_cols():
    h = jnp.arange(1, NSA_HEADS + 1, dtype=F32)
    slopes = (2.0 ** (-8.0 * h / NSA_HEADS)).reshape(NSA_KV, NSA_REP)
    return jnp.repeat(slopes, Q_BLOCK, axis=1).reshape(NSA_KV, 1, QCOLS)


def _overlap_t(s):
    n_cmp_pad = s // CMP_STRIDE
    n_slc = s // SLC_BLOCK
    ci = jnp.arange(n_cmp_pad)[None, :]
    sj = jnp.arange(n_slc)[:, None]
    ov = (ci * CMP_STRIDE < (sj + 1) * SLC_BLOCK) & (ci * CMP_STRIDE + CMP_LEN > sj * SLC_BLOCK)
    return ov.astype(BF16)


def _nsa(proj, cmp_pos, cmp_w1, cmp_w2):
    s = proj.shape[0]
    nqb = s // Q_BLOCK
    kvw = NSA_KV * NSA_DH
    n_q = proj[:, OFF_NQ:OFF_NQ + NSA_HEADS * NSA_DH]
    parts = [proj[:, OFF_NKV + i * kvw:OFF_NKV + (i + 1) * kvw] for i in range(6)]
    k_c, v_c, k_s, v_s, k_w, v_w = [p.reshape(s, NSA_KV, NSA_DH).transpose(1, 0, 2) for p in parts]
    n_g = proj[:, OFF_NG:OFF_NG + 3 * NSA_HEADS]

    qt = n_q.reshape(nqb, Q_BLOCK, NSA_KV, NSA_REP, NSA_DH).transpose(2, 0, 4, 3, 1)
    qt = qt.reshape(NSA_KV, nqb, NSA_DH, QCOLS).astype(BF16)
    gl = n_g.reshape(nqb, Q_BLOCK, NSA_KV, NSA_REP, 3).transpose(2, 0, 4, 3, 1).reshape(NSA_KV, nqb, 3, QCOLS)
    slopes = _alibi_slope_cols()

    segs = jnp.stack([k_c, v_c]).reshape(2, NSA_KV, s // CMP_STRIDE, CMP_STRIDE * NSA_DH)
    posflat = cmp_pos.reshape(2, 2, CMP_STRIDE * NSA_DH)
    cmp = _compress(segs, posflat, cmp_w1.astype(BF16), cmp_w2.astype(BF16))
    kc = cmp[0]
    vct = cmp[1].transpose(0, 2, 1)
    oc, selb, anyf = _nsa_cmp(qt, kc, vct, _overlap_t(s), slopes)
    tile_any = (anyf[..., 0] > 0).astype(jnp.int32).reshape(-1)

    onehot = (jnp.arange(s)[:, None] // SLC_BLOCK % SEL_TILE_BLOCKS == jnp.arange(KAUG - NSA_DH)[None, :])
    ksa = jnp.concatenate(
        [k_s.astype(BF16), jnp.broadcast_to(onehot.astype(BF16), (NSA_KV, s, KAUG - NSA_DH))], axis=-1)
    vst = v_s.astype(BF16).reshape(NSA_KV, s // SEL_TILE, SEL_TILE, NSA_DH).transpose(0, 1, 3, 2)
    kw = k_w.astype(BF16).reshape(NSA_KV, nqb, Q_BLOCK, NSA_DH)
    vwt = v_w.astype(BF16).reshape(NSA_KV, nqb, Q_BLOCK, NSA_DH).transpose(0, 1, 3, 2)
    yt = _nsa_attn(tile_any, qt, ksa, vst, kw, vwt, selb, oc, gl, slopes)
    y = yt.reshape(NSA_KV, nqb, NSA_DH, NSA_REP, Q_BLOCK).transpose(1, 4, 0, 3, 2)
    return y.reshape(s, NSA_HEADS * NSA_DH)


def kernel(x, ffn1_norm, ffn1_w1, ffn1_w2, mix_norm, w_in, gm_ln_g, gm_ln_b, gm_ws, gm_bs, ret_gn_g, ret_gn_b,
           cmp_pos, cmp_w1, cmp_w2, w_branch_out, w_merge_gate, b_merge_gate, w_o, ffn2_norm, ffn2_w1, ffn2_w2,
           final_norm):
    bsz, s, _ = x.shape
    depth = ffn1_w1.shape[0]
    row = lambda v: v.reshape(1, -1)
    fin = row(final_norm)
    outs = []
    for b in range(bsz):
        xb = x[b]
        for l in range(depth):
            xb = _ffn(xb, row(ffn1_norm[l]), ffn1_w1[l].astype(BF16), ffn1_w2[l].astype(BF16), fin, False)
            w_in_pad = jnp.pad(w_in[l].astype(BF16), ((0, 0), (0, D_IN_PAD - D_IN)))
            proj = _proj(xb, row(mix_norm[l]), w_in_pad)
            y_a = _gmlp(proj, row(gm_ln_g[l]), row(gm_ln_b[l]), gm_ws[l], gm_bs[l].T)
            y_b = _retention(proj, row(ret_gn_g[l]), row(ret_gn_b[l]))
            y_c = _nsa(proj, cmp_pos[l], cmp_w1[l], cmp_w2[l])
            xb = _merge(xb, row(mix_norm[l]), y_a, y_b, y_c, w_merge_gate[l].astype(BF16), row(b_merge_gate[l]),
                        w_branch_out[l].astype(BF16), w_o[l].astype(BF16))
            xb = _ffn(xb, row(ffn2_norm[l]), ffn2_w1[l].astype(BF16), ffn2_w2[l].astype(BF16), fin,
                      l == depth - 1)
        outs.append(xb)
    return jnp.stack(outs)
```

```python
import functools
import math

import jax
import jax.numpy as jnp
from jax import lax
from jax.experimental import pallas as pl
from jax.experimental.pallas import tpu as pltpu

F32 = jnp.float32
BF16 = jnp.bfloat16

D_MODEL = 1024
D_FF = 2816
NORM_EPS = 1e-6
FFN_RES_WEIGHT = 0.5
GM_WIDTH = 512
GM_GROUPS = 4
GM_CHUNK = 128
RET_HEADS = 4
RET_DK = 64
RET_DV = 128
RET_CHUNK = 128
NSA_HEADS = 8
NSA_KV = 2
NSA_REP = NSA_HEADS // NSA_KV
NSA_DH = 64
CMP_LEN = 32
CMP_STRIDE = 16
CMP_HIDDEN = 128
SLC_BLOCK = 64
N_SELECT = 16
WINDOW = 512
Q_BLOCK = 128
N_MIXERS = 3
MIX_WIDTH = 512
IN_SPLITS = (512, 512, 256, 256, 512, 512, 512, 768, 24)
D_IN = sum(IN_SPLITS)
D_IN_PAD = 4096
BIG = 1e9
NEG = -1e30
PICKED = -(2.0 ** 127)

OFF_GM_U, OFF_GM_V, OFF_RQ, OFF_RK, OFF_RV, OFF_RG, OFF_NQ, OFF_NKV, OFF_NG = (
    0, 512, 1024, 1280, 1536, 2048, 2560, 3072, 3840)

QCOLS = NSA_REP * Q_BLOCK
SEL_TILE = 512
SEL_TILE_BLOCKS = SEL_TILE // SLC_BLOCK
SUB = 128
KAUG = 128
COL_SEL = NSA_DH
COL_POS = NSA_DH + SEL_TILE_BLOCKS
VAUG = 80
WIN_TILES = (WINDOW + Q_BLOCK) // SUB
CMP_TILE = 128
CMP_TILE_TOKENS = CMP_TILE * CMP_STRIDE
CMP_TILE_SLC = CMP_TILE_TOKENS // SLC_BLOCK
N_NSA_BRANCH = 3
KVW = NSA_KV * NSA_DH
PA_WIDTH = OFF_NQ
WN_CV = PA_WIDTH
WN_GL = WN_CV + 2 * KVW
WN_KS = WN_GL + NSA_KV * KAUG
WN_WIDTH = WN_KS + 2 * NSA_KV * KAUG
WT_ROWS = NSA_HEADS * NSA_DH + 2 * KVW
VMEM_LIMIT = 56 * 1024 * 1024
CAST_BLOCK_BYTES = 4 * 1024 * 1024


def _cparams(sem, vmem=VMEM_LIMIT):
    return pltpu.CompilerParams(dimension_semantics=sem, vmem_limit_bytes=vmem)


def _cast_kernel(w_ref, o_ref):
    o_ref[...] = w_ref[...].astype(o_ref.dtype)


def _to_bf16(w):
    w2 = w.reshape(-1, w.shape[-1])
    r, c = w2.shape
    tr = 8
    while r % (2 * tr) == 0 and 2 * tr * c * 4 <= CAST_BLOCK_BYTES:
        tr *= 2
    out = pl.pallas_call(
        _cast_kernel,
        out_shape=jax.ShapeDtypeStruct((r, c), BF16),
        grid=(r // tr,),
        in_specs=[pl.BlockSpec((tr, c), lambda i: (i, 0))],
        out_specs=pl.BlockSpec((tr, c), lambda i: (i, 0)),
        compiler_params=_cparams(("arbitrary",)),
        name="cast_bf16",
    )(w2)
    return out.reshape(w.shape)


def _rms(x, g):
    return x * lax.rsqrt(jnp.mean(x * x, axis=-1, keepdims=True) + NORM_EPS) * g


def _dot(a, b):
    return jnp.dot(a, b, preferred_element_type=F32)


def _ffn_kernel(x_ref, g_ref, w1a_ref, w1b_ref, w2_ref, fin_ref, o_ref, h_ref, acc_ref, *, final_norm):
    f = pl.program_id(1)

    @pl.when(f == 0)
    def _():
        h_ref[...] = _rms(x_ref[...], g_ref[...]).astype(BF16)
        acc_ref[...] = jnp.zeros_like(acc_ref)

    h = h_ref[...]
    a = _dot(h, w1a_ref[...])
    b = _dot(h, w1b_ref[...])
    act = (jax.nn.silu(a) * b).astype(BF16)
    acc_ref[...] += _dot(act, w2_ref[...])

    @pl.when(f == pl.num_programs(1) - 1)
    def _():
        y = x_ref[...] + FFN_RES_WEIGHT * acc_ref[...]
        if final_norm:
            y = _rms(y, fin_ref[...])
        o_ref[...] = y


def _ffn(x, g, w1, w2, fin, final_norm, tm=1024, tf=256):
    s = x.shape[0]
    nf = D_FF // tf
    return pl.pallas_call(
        functools.partial(_ffn_kernel, final_norm=final_norm),
        out_shape=jax.ShapeDtypeStruct((s, D_MODEL), F32),
        grid=(s // tm, nf),
        in_specs=[
            pl.BlockSpec((tm, D_MODEL), lambda i, f: (i, 0)),
            pl.BlockSpec((1, D_MODEL), lambda i, f: (0, 0)),
            pl.BlockSpec((D_MODEL, tf), lambda i, f: (0, f)),
            pl.BlockSpec((D_MODEL, tf), lambda i, f: (0, f + nf)),
            pl.BlockSpec((tf, D_MODEL), lambda i, f: (f, 0)),
            pl.BlockSpec((1, D_MODEL), lambda i, f: (0, 0)),
        ],
        out_specs=pl.BlockSpec((tm, D_MODEL), lambda i, f: (i, 0)),
        scratch_shapes=[pltpu.VMEM((tm, D_MODEL), BF16), pltpu.VMEM((tm, D_MODEL), F32)],
        compiler_params=_cparams(("arbitrary", "arbitrary")),
        name="ffn",
    )(x, g, w1, w1, w2, fin)


def _proj_kernel(x_ref, g_ref, wn_ref, wt_ref, kcs_ref, kcw_ref, vones_ref,
                 pa_ref, pcv_ref, pgl_ref, ksa_ref, kwa_ref, qt_ref, vsta_ref, vwta_ref):
    tm = x_ref.shape[0]
    h = _rms(x_ref[...], g_ref[...]).astype(BF16)
    step = 512
    for c in range(0, PA_WIDTH, step):
        pa_ref[:, c:c + step] = _dot(h, wn_ref[:, c:c + step])
    pcv_ref[...] = _dot(h, wn_ref[:, WN_CV:WN_CV + 2 * KVW])
    pgl_ref[...] = _dot(h, wn_ref[:, WN_GL:WN_GL + 2 * KAUG])
    kk = _dot(h, wn_ref[:, WN_KS:WN_KS + 4 * KAUG])
    for g in range(NSA_KV):
        ksa_ref[g] = (kk[:, g * KAUG:(g + 1) * KAUG] + kcs_ref[...]).astype(BF16)
        kwa_ref[g] = (kk[:, (NSA_KV + g) * KAUG:(NSA_KV + g + 1) * KAUG] + kcw_ref[...]).astype(BF16)
    pt = lax.dot_general(wt_ref[...], h, (((1,), (1,)), ((), ())), preferred_element_type=F32)
    nq = NSA_HEADS * NSA_DH
    for g in range(NSA_KV):
        for b in range(tm // Q_BLOCK):
            for r in range(NSA_REP):
                hd = g * NSA_REP + r
                qt_ref[g, b, :, r * Q_BLOCK:(r + 1) * Q_BLOCK] = (
                    pt[hd * NSA_DH:(hd + 1) * NSA_DH, b * Q_BLOCK:(b + 1) * Q_BLOCK].astype(BF16))
            rows = slice(nq + KVW + g * NSA_DH, nq + KVW + (g + 1) * NSA_DH)
            vwta_ref[g, b, 0:NSA_DH, :] = pt[rows, b * SUB:(b + 1) * SUB].astype(BF16)
            vwta_ref[g, b, NSA_DH:VAUG, :] = vones_ref[:, 0:SUB]
        for t in range(tm // SEL_TILE):
            rows = slice(nq + g * NSA_DH, nq + (g + 1) * NSA_DH)
            vsta_ref[g, t, 0:NSA_DH, :] = pt[rows, t * SEL_TILE:(t + 1) * SEL_TILE].astype(BF16)
            vsta_ref[g, t, NSA_DH:VAUG, :] = vones_ref[...]


def _proj(x, g, wn, wt, kcs, kcw, vones, tm=512):
    s = x.shape[0]
    fixed = lambda i: (0, 0)
    return pl.pallas_call(
        _proj_kernel,
        out_shape=[
            jax.ShapeDtypeStruct((s, PA_WIDTH), F32),
            jax.ShapeDtypeStruct((s, 2 * KVW), F32),
            jax.ShapeDtypeStruct((s, 2 * KAUG), F32),
            jax.ShapeDtypeStruct((NSA_KV, s, KAUG), BF16),
            jax.ShapeDtypeStruct((NSA_KV, s, KAUG), BF16),
            jax.ShapeDtypeStruct((NSA_KV, s // Q_BLOCK, NSA_DH, QCOLS), BF16),
            jax.ShapeDtypeStruct((NSA_KV, s // SEL_TILE, VAUG, SEL_TILE), BF16),
            jax.ShapeDtypeStruct((NSA_KV, s // SUB, VAUG, SUB), BF16),
        ],
        grid=(s // tm,),
        in_specs=[
            pl.BlockSpec((tm, D_MODEL), lambda i: (i, 0)),
            pl.BlockSpec((1, D_MODEL), fixed),
            pl.BlockSpec((D_MODEL, WN_WIDTH), fixed),
            pl.BlockSpec((WT_ROWS, D_MODEL), fixed),
            pl.BlockSpec((tm, KAUG), fixed),
            pl.BlockSpec((tm, KAUG), fixed),
            pl.BlockSpec((VAUG - NSA_DH, SEL_TILE), fixed),
        ],
        out_specs=[
            pl.BlockSpec((tm, PA_WIDTH), lambda i: (i, 0)),
            pl.BlockSpec((tm, 2 * KVW), lambda i: (i, 0)),
            pl.BlockSpec((tm, 2 * KAUG), lambda i: (i, 0)),
            pl.BlockSpec((NSA_KV, tm, KAUG), lambda i: (0, i, 0)),
            pl.BlockSpec((NSA_KV, tm, KAUG), lambda i: (0, i, 0)),
            pl.BlockSpec((NSA_KV, tm // Q_BLOCK, NSA_DH, QCOLS), lambda i: (0, i, 0, 0)),
            pl.BlockSpec((NSA_KV, tm // SEL_TILE, VAUG, SEL_TILE), lambda i: (0, i, 0, 0)),
            pl.BlockSpec((NSA_KV, tm // SUB, VAUG, SUB), lambda i: (0, i, 0, 0)),
        ],
        compiler_params=_cparams(("arbitrary",)),
        name="proj",
    )(x, g, wn, wt, kcs, kcw, vones)


def _proj_weights(w_in):
    w = w_in.astype(BF16)
    z = lambda n: jnp.zeros((D_MODEL, n), BF16)
    kv = lambda i, g: w[:, OFF_NKV + i * KVW + g * NSA_DH:OFF_NKV + i * KVW + (g + 1) * NSA_DH]
    ngl = N_NSA_BRANCH * NSA_REP
    cols = [w[:, 0:PA_WIDTH], w[:, OFF_NKV:OFF_NKV + 2 * KVW]]
    for g in range(NSA_KV):
        cols += [w[:, OFF_NG + g * ngl:OFF_NG + (g + 1) * ngl], z(KAUG - ngl)]
    for i in (2, 4):
        for g in range(NSA_KV):
            cols += [kv(i, g), z(KAUG - NSA_DH)]
    wn = jnp.concatenate(cols, axis=1)
    wt = jnp.concatenate([w[:, OFF_NQ:OFF_NQ + NSA_HEADS * NSA_DH], kv(3, 0), kv(3, 1), kv(5, 0), kv(5, 1)], axis=1).T
    return wn, wt


def _proj_constants(tm=512):
    pos = jnp.arange(tm)
    zero = jnp.zeros((tm, NSA_DH), F32)

    def pos_cols(p):
        return jnp.stack([p // 16 * 16, p % 16], axis=1).astype(F32)

    onehot = (pos[:, None] // SLC_BLOCK % SEL_TILE_BLOCKS == jnp.arange(SEL_TILE_BLOCKS)[None, :]).astype(F32)
    tail = jnp.zeros((tm, KAUG - COL_POS - 2), F32)
    kcs = jnp.concatenate([zero, onehot, pos_cols(pos % SEL_TILE), tail], axis=1)
    kcw = jnp.concatenate([zero, jnp.zeros_like(onehot), pos_cols(pos % SUB), tail], axis=1)
    r = jnp.arange(VAUG - NSA_DH)[:, None]
    vones = jnp.broadcast_to(jnp.where(r == 0, 1.0, 0.0), (VAUG - NSA_DH, SEL_TILE)).astype(BF16)
    return kcs, kcw, vones


def _gmlp_kernel(u_ref, v_ref, lg_ref, lb_ref, ws_ref, bst_ref, o_ref, *, chunks):
    cg = GM_WIDTH // GM_GROUPS
    u = jax.nn.gelu(u_ref[...])
    v = jax.nn.gelu(v_ref[...])
    mu = jnp.mean(v, axis=-1, keepdims=True)
    vc = v - mu
    var = jnp.mean(vc * vc, axis=-1, keepdims=True)
    vn = (vc * lax.rsqrt(var + NORM_EPS) * lg_ref[...] + lb_ref[...]).astype(BF16)
    row = lax.broadcasted_iota(jnp.int32, (GM_CHUNK, GM_CHUNK), 0)
    col = lax.broadcasted_iota(jnp.int32, (GM_CHUNK, GM_CHUNK), 1)
    causal = row >= col
    bst = bst_ref[...]
    for gi in range(GM_GROUPS):
        w = jnp.where(causal, ws_ref[gi], 0.0).astype(BF16)
        bias = bst[:, gi:gi + 1]
        for c in range(chunks):
            rs = slice(c * GM_CHUNK, (c + 1) * GM_CHUNK)
            cs = slice(gi * cg, (gi + 1) * cg)
            sv = _dot(w, vn[rs, cs]) + bias
            o_ref[rs, cs] = (u[rs, cs] * sv).astype(o_ref.dtype)


def _gmlp(proj, ln_g, ln_b, ws, bst, tm=512):
    s = proj.shape[0]
    return pl.pallas_call(
        functools.partial(_gmlp_kernel, chunks=tm // GM_CHUNK),
        out_shape=jax.ShapeDtypeStruct((s, GM_WIDTH), BF16),
        grid=(s // tm,),
        in_specs=[
            pl.BlockSpec((tm, GM_WIDTH), lambda i: (i, OFF_GM_U // GM_WIDTH)),
            pl.BlockSpec((tm, GM_WIDTH), lambda i: (i, OFF_GM_V // GM_WIDTH)),
            pl.BlockSpec((1, GM_WIDTH), lambda i: (0, 0)),
            pl.BlockSpec((1, GM_WIDTH), lambda i: (0, 0)),
            pl.BlockSpec((GM_GROUPS, GM_CHUNK, GM_CHUNK), lambda i: (0, 0, 0)),
            pl.BlockSpec((GM_CHUNK, GM_GROUPS), lambda i: (0, 0)),
        ],
        out_specs=pl.BlockSpec((tm, GM_WIDTH), lambda i: (i, 0)),
        compiler_params=_cparams(("arbitrary",)),
        name="gmlp",
    )(proj, proj, ln_g, ln_b, ws, bst)


def _ret_kernel(q_ref, k_ref, v_ref, g_ref, gg_ref, gb_ref, o_ref, st_ref, *, chunks):
    @pl.when(pl.program_id(0) == 0)
    def _():
        st_ref[...] = jnp.zeros_like(st_ref)

    c = RET_CHUNK
    ri = lax.broadcasted_iota(jnp.int32, (c, c), 0)
    ci = lax.broadcasted_iota(jnp.int32, (c, c), 1)
    diff = (ri - ci).astype(F32)
    pos_k = lax.broadcasted_iota(jnp.int32, (c, RET_DK), 0).astype(F32)
    pos_v = lax.broadcasted_iota(jnp.int32, (c, RET_DV), 0).astype(F32)
    for h in range(RET_HEADS):
        log_gamma = math.log(1.0 - 2.0 ** (-5.0 - h))
        intra_decay = jnp.where(diff >= 0, jnp.exp(log_gamma * jnp.maximum(diff, 0.0)), 0.0)
        k_decay = jnp.exp(log_gamma * (c - 1.0 - pos_k))
        q_decay = jnp.exp(log_gamma * (pos_v + 1.0))
        chunk_decay = math.exp(log_gamma * c)
        gam = gg_ref[:, h * RET_DV:(h + 1) * RET_DV]
        bet = gb_ref[:, h * RET_DV:(h + 1) * RET_DV]
        for n in range(chunks):
            rs = slice(n * c, (n + 1) * c)
            q = q_ref[rs, h * RET_DK:(h + 1) * RET_DK].astype(BF16)
            kf = k_ref[rs, h * RET_DK:(h + 1) * RET_DK] * (RET_DK ** -0.5)
            k = kf.astype(BF16)
            v = v_ref[rs, h * RET_DV:(h + 1) * RET_DV].astype(BF16)
            scores = lax.dot_general(q, k, (((1,), (1,)), ((), ())), preferred_element_type=F32) * intra_decay
            intra = _dot(scores.astype(BF16), v)
            state = st_ref[h]
            cross = _dot(q, state.astype(BF16)) * q_decay
            kd = (kf * k_decay).astype(BF16)
            kv = lax.dot_general(kd, v, (((0,), (0,)), ((), ())), preferred_element_type=F32)
            st_ref[h] = state * chunk_decay + kv
            y = intra + cross
            mu = jnp.mean(y, axis=-1, keepdims=True)
            yc = y - mu
            var = jnp.mean(yc * yc, axis=-1, keepdims=True)
            yn = yc * lax.rsqrt(var + NORM_EPS) * gam + bet
            gate = g_ref[rs, h * RET_DV:(h + 1) * RET_DV]
            o_ref[rs, h * RET_DV:(h + 1) * RET_DV] = (jax.nn.silu(gate) * yn).astype(o_ref.dtype)


def _retention(proj, gn_g, gn_b, tm=512):
    s = proj.shape[0]
    hk = RET_HEADS * RET_DK
    hv = RET_HEADS * RET_DV
    return pl.pallas_call(
        functools.partial(_ret_kernel, chunks=tm // RET_CHUNK),
        out_shape=jax.ShapeDtypeStruct((s, hv), BF16),
        grid=(s // tm,),
        in_specs=[
            pl.BlockSpec((tm, hk), lambda i: (i, OFF_RQ // hk)),
            pl.BlockSpec((tm, hk), lambda i: (i, OFF_RK // hk)),
            pl.BlockSpec((tm, hv), lambda i: (i, OFF_RV // hv)),
            pl.BlockSpec((tm, hv), lambda i: (i, OFF_RG // hv)),
            pl.BlockSpec((1, hv), lambda i: (0, 0)),
            pl.BlockSpec((1, hv), lambda i: (0, 0)),
        ],
        out_specs=pl.BlockSpec((tm, hv), lambda i: (i, 0)),
        scratch_shapes=[pltpu.VMEM((RET_HEADS, RET_DK, RET_DV), F32)],
        compiler_params=_cparams(("arbitrary",)),
        name="retention",
    )(proj, proj, proj, proj, gn_g, gn_b)


def _compress_kernel(t_ref, pos_ref, w1_ref, w2_ref, o_ref, xlo_ref, xhi_ref, *, values):
    n = t_ref.shape[0] // CMP_STRIDE
    for i in range(CMP_STRIDE):
        x = t_ref[pl.ds(i, n, stride=CMP_STRIDE), :]
        xlo_ref[:, i * KVW:(i + 1) * KVW] = (x + pos_ref[0:1, i * KVW:(i + 1) * KVW]).astype(BF16)
        xhi_ref[:, i * KVW:(i + 1) * KVW] = (x + pos_ref[1:2, i * KVW:(i + 1) * KVW]).astype(BF16)
    lo = _dot(xlo_ref[...], w1_ref[0])
    hi = _dot(xhi_ref[...], w1_ref[1])
    hi_next = jnp.concatenate([hi[1:], hi[:1]], axis=0)
    rowi = lax.broadcasted_iota(jnp.int32, lo.shape, 0)
    hid = jnp.where(rowi < n - 1, jax.nn.gelu(lo + hi_next), 0.0).astype(BF16)
    for g in range(NSA_KV):
        if values:
            vt = lax.dot_general(w2_ref[g], hid, (((1,), (1,)), ((), ())), preferred_element_type=F32)
            r = lax.broadcasted_iota(jnp.int32, vt.shape, 0)
            vt = jnp.where(r == NSA_DH, 1.0, vt).astype(BF16)
            for c in range(n // CMP_TILE):
                o_ref[g, c] = vt[:, c * CMP_TILE:(c + 1) * CMP_TILE]
        else:
            k = _dot(hid, w2_ref[g])
            col = lax.broadcasted_iota(jnp.int32, k.shape, 1)
            blk = (lax.broadcasted_iota(jnp.int32, k.shape, 0) & (CMP_TILE - 1)) * CMP_STRIDE
            k = jnp.where(col == NSA_DH, blk.astype(F32), k).astype(BF16)
            for c in range(n // CMP_TILE):
                o_ref[g, c] = k[c * CMP_TILE:(c + 1) * CMP_TILE, :]


def _compress(pcv, posrows, w1, w2, values):
    s = pcv.shape[0]
    n = s // CMP_STRIDE
    nct = n // CMP_TILE
    out_tile = (VAUG, CMP_TILE) if values else (CMP_TILE, KAUG)
    width = CMP_STRIDE * KVW
    return pl.pallas_call(
        functools.partial(_compress_kernel, values=values),
        out_shape=jax.ShapeDtypeStruct((NSA_KV, nct) + out_tile, BF16),
        grid=(1,),
        in_specs=[
            pl.BlockSpec((s, KVW), lambda i: (0, 1 if values else 0)),
            pl.BlockSpec((2, width), lambda i: (0, 0)),
            pl.BlockSpec((2, width, NSA_KV * CMP_HIDDEN), lambda i: (0, 0, 0)),
            pl.BlockSpec((NSA_KV,) + w2.shape[1:], lambda i: (0, 0, 0)),
        ],
        out_specs=pl.BlockSpec((NSA_KV, nct) + out_tile, lambda i: (0, 0, 0, 0)),
        scratch_shapes=[pltpu.VMEM((n, width), BF16), pltpu.VMEM((n, width), BF16)],
        compiler_params=_cparams(("arbitrary",)),
        name="nsa_compress_v" if values else "nsa_compress_k",
    )(pcv, posrows, w1, w2)


def _compress_weights(pos, w1, w2, values):
    posrows = jnp.tile(pos.reshape(2, CMP_STRIDE, 1, NSA_DH), (1, 1, NSA_KV, 1)).reshape(2, CMP_STRIDE * KVW)
    w = w1.astype(BF16).reshape(2, CMP_STRIDE, NSA_DH, CMP_HIDDEN)
    eye = jnp.eye(NSA_KV, dtype=BF16)
    w1b = jnp.einsum('hidc,gk->higdkc', w, eye).reshape(2, CMP_STRIDE * KVW, NSA_KV * CMP_HIDDEN)
    w2g = jnp.einsum('cd,gk->gkcd', w2.astype(BF16), eye).reshape(NSA_KV, NSA_KV * CMP_HIDDEN, NSA_DH)
    if values:
        w2g = jnp.pad(w2g.transpose(0, 2, 1), ((0, 0), (0, VAUG - NSA_DH), (0, 0)))
    else:
        w2g = jnp.pad(w2g, ((0, 0), (0, 0), (0, KAUG - NSA_DH)))
    return posrows, w1b, w2g


def _nsa_cmp_kernel(qt_ref, kca_ref, vcta_ref, ovl_ref, slope_ref, oc_ref, selb_ref, any_ref,
                    qaug_ref, dmask_ref, z_ref, imp_ref, jf_ref):
    qb = pl.program_id(1)
    nslc = selb_ref.shape[2]
    slope = slope_ref[0]

    @pl.when(qb == 0)
    def _():
        cl = lax.broadcasted_iota(jnp.int32, (CMP_TILE, QCOLS), 0)
        q = lax.broadcasted_iota(jnp.int32, (CMP_TILE, QCOLS), 1) & (Q_BLOCK - 1)
        dmask_ref[...] = (cl * CMP_STRIDE - q).astype(F32)
        qaug_ref[...] = jnp.zeros_like(qaug_ref)
        r16 = lax.broadcasted_iota(jnp.int32, (16, QCOLS), 0)
        qaug_ref[NSA_DH:NSA_DH + 16, :] = jnp.where(r16 == 0, slope, 0.0).astype(BF16)
        jf_ref[...] = lax.broadcasted_iota(jnp.int32, jf_ref.shape, 0).astype(F32)

    t0 = qb * Q_BLOCK
    qaug_ref[0:NSA_DH, :] = qt_ref[0, 0] * (NSA_DH ** -0.5)
    imp_ref[...] = jnp.zeros_like(imp_ref)
    td = t0 // CMP_TILE_TOKENS

    def tile_offset(ti):
        return slope * (t0 - ti * CMP_TILE_TOKENS).astype(F32)

    def scores(ti, m, masked):
        s = _dot(kca_ref[0, ti], qaug_ref[...])
        if masked:
            lim = (t0 - ti * CMP_TILE_TOKENS - (CMP_LEN - 1)).astype(F32)
            s = jnp.where(dmask_ref[...] <= lim, s, NEG)
        z_ref[ti] = s
        return jnp.maximum(m, jnp.max(s, axis=0, keepdims=True) - tile_offset(ti))

    lo = jnp.maximum(td - 1, 0)
    m = jnp.full((1, QCOLS), NEG, F32)
    m = lax.fori_loop(0, lo, lambda ti, mm: scores(ti, mm, False), m)
    m = lax.fori_loop(lo, td + 1, lambda ti, mm: scores(ti, mm, True), m)

    def weigh(ti, acc):
        p = jnp.exp(z_ref[ti] - (m + tile_offset(ti))).astype(BF16)
        rows = pl.ds(pl.multiple_of(ti * CMP_TILE_SLC, CMP_TILE_SLC), 2 * CMP_TILE_SLC)
        imp_ref[rows, :] += _dot(ovl_ref[...], p)
        return acc + _dot(vcta_ref[0, ti], p)

    acc = lax.fori_loop(0, td + 1, weigh, jnp.zeros((VAUG, QCOLS), F32))
    inv = jnp.where(m > 0.5 * NEG, 1.0 / acc[NSA_DH:NSA_DH + 1], 0.0)
    oc_ref[0, 0] = acc[0:NSA_DH] * inv

    imp4 = imp_ref[0:nslc, :] * inv
    imp = imp4[:, 0:Q_BLOCK]
    for r in range(1, NSA_REP):
        imp = imp + imp4[:, r * Q_BLOCK:(r + 1) * Q_BLOCK]
    j = lax.broadcasted_iota(jnp.int32, (nslc, Q_BLOCK), 0)
    qq = lax.broadcasted_iota(jnp.int32, (nslc, Q_BLOCK), 1)
    cur = 2 * qb + jnp.where(qq >= SLC_BLOCK, 1, 0)
    forced = (j == 0) | (j == cur) | (j == cur - 1)
    imp = jnp.where(j > cur, -BIG, imp)
    imp = jnp.where(forced, PICKED, imp)

    for _ in range(max(min(N_SELECT, nslc) - 3, 0)):
        best = jnp.max(imp, axis=0, keepdims=True)
        first = jnp.min(jnp.where(imp == best, jf_ref[...], float(nslc)), axis=0, keepdims=True)
        imp = jnp.where(jf_ref[...] == first, PICKED, imp)
    sel = imp == PICKED
    selb_ref[0, 0] = jnp.where(sel, 0.0, NEG)
    ntile = nslc // SEL_TILE_BLOCKS
    ti = lax.broadcasted_iota(jnp.int32, (ntile, nslc), 0) * SEL_TILE_BLOCKS
    tj = lax.broadcasted_iota(jnp.int32, (ntile, nslc), 1)
    member = jnp.where((tj >= ti) & (tj < ti + SEL_TILE_BLOCKS), 1.0, 0.0).astype(BF16)
    count = _dot(member, jnp.where(sel, 1.0, 0.0).astype(BF16))
    any_ref[0, 0] = jnp.broadcast_to(jnp.max(count, axis=1, keepdims=True), (ntile, Q_BLOCK))


def _nsa_cmp(qt, kca, vcta, ovl, slopes, nslc):
    kv, nqb = qt.shape[0], qt.shape[1]
    nct = kca.shape[1]
    ntile = nslc // SEL_TILE_BLOCKS
    return pl.pallas_call(
        _nsa_cmp_kernel,
        out_shape=[
            jax.ShapeDtypeStruct((kv, nqb, NSA_DH, QCOLS), F32),
            jax.ShapeDtypeStruct((kv, nqb, nslc, Q_BLOCK), F32),
            jax.ShapeDtypeStruct((kv, nqb, ntile, Q_BLOCK), F32),
        ],
        grid=(kv, nqb),
        in_specs=[
            pl.BlockSpec((1, 1, NSA_DH, QCOLS), lambda g, b: (g, b, 0, 0)),
            pl.BlockSpec((1, nct, CMP_TILE, KAUG), lambda g, b: (g, 0, 0, 0)),
            pl.BlockSpec((1, nct, VAUG, CMP_TILE), lambda g, b: (g, 0, 0, 0)),
            pl.BlockSpec((2 * CMP_TILE_SLC, CMP_TILE), lambda g, b: (0, 0)),
            pl.BlockSpec((1, 1, QCOLS), lambda g, b: (g, 0, 0)),
        ],
        out_specs=[
            pl.BlockSpec((1, 1, NSA_DH, QCOLS), lambda g, b: (g, b, 0, 0)),
            pl.BlockSpec((1, 1, nslc, Q_BLOCK), lambda g, b: (g, b, 0, 0)),
            pl.BlockSpec((1, 1, ntile, Q_BLOCK), lambda g, b: (g, b, 0, 0)),
        ],
        scratch_shapes=[
            pltpu.VMEM((KAUG, QCOLS), BF16),
            pltpu.VMEM((CMP_TILE, QCOLS), F32),
            pltpu.VMEM((nct, CMP_TILE, QCOLS), F32),
            pltpu.VMEM((nslc + CMP_TILE_SLC, QCOLS), F32),
            pltpu.VMEM((nslc, Q_BLOCK), F32),
        ],
        compiler_params=_cparams(("arbitrary", "arbitrary")),
        name="nsa_cmp_topk",
    )(qt, kca, vcta, ovl, slopes)


def _nsa_attn_kernel(any_ref, qt_ref, ksa_ref, vsta_ref, kwa_ref, vwta_ref, selb_ref, oc_ref, gl_ref, slope_ref,
                     o_ref, qaug_ref, addlo_ref, addhi_ref, z_ref, m_ref, mx_ref, acc_ref):
    g = pl.program_id(0)
    qb = pl.program_id(1)
    nqb = pl.num_programs(1)
    ntile = selb_ref.shape[2] // SEL_TILE_BLOCKS
    slope = slope_ref[0]
    nsub = SEL_TILE // SUB

    @pl.when(qb == 0)
    def _():
        cl = lax.broadcasted_iota(jnp.int32, (SUB, QCOLS), 0)
        q = lax.broadcasted_iota(jnp.int32, (SUB, QCOLS), 1) & (Q_BLOCK - 1)
        addlo_ref[...] = jnp.where(cl <= q, 0.0, NEG)
        addhi_ref[...] = jnp.where(cl > q, 0.0, NEG)
        qaug_ref[...] = jnp.zeros_like(qaug_ref)

    t0 = qb * Q_BLOCK
    qaug_ref[0:NSA_DH, :] = qt_ref[0, 0] * (NSA_DH ** -0.5)
    m_ref[...] = jnp.full_like(m_ref, NEG)
    acc_ref[...] = jnp.zeros_like(acc_ref)
    r8 = lax.broadcasted_iota(jnp.int32, (SEL_TILE_BLOCKS, QCOLS), 0)
    slope_rows = jnp.where(r8 < 2, slope, 0.0)

    def set_selection_rows(kt):
        sb = selb_ref[0, 0, pl.ds(pl.multiple_of(kt * SEL_TILE_BLOCKS, SEL_TILE_BLOCKS), SEL_TILE_BLOCKS), :]
        sb = jnp.concatenate([sb] * NSA_REP, axis=1)
        qaug_ref[COL_SEL:COL_SEL + 16, :] = jnp.concatenate([sb, slope_rows], axis=0).astype(BF16)

    def sub_scores(kt, i, add_ref):
        k = ksa_ref[0, pl.ds(pl.multiple_of(kt * SEL_TILE + i * SUB, SUB), SUB), :]
        s = _dot(k, qaug_ref[...])
        if add_ref is not None:
            s = s + add_ref[...]
        z_ref[i] = s
        return jnp.max(s, axis=0, keepdims=True)

    def sub_pv(kt, i, shift):
        p = jnp.exp(z_ref[i] - shift).astype(BF16)
        return _dot(vsta_ref[0, kt, :, i * SUB:(i + 1) * SUB], p)

    def full_tile(kt):
        set_selection_rows(kt)
        rv = slope * (t0 - kt * SEL_TILE).astype(F32)
        mx = sub_scores(kt, 0, None)
        for i in range(1, nsub):
            mx = jnp.maximum(mx, sub_scores(kt, i, None))
        m_old = m_ref[...]
        m_new = jnp.maximum(m_old, mx - rv)
        shift = m_new + rv
        pv = sub_pv(kt, 0, shift)
        for i in range(1, nsub):
            pv = pv + sub_pv(kt, i, shift)
        acc_ref[...] = jnp.exp(m_old - m_new) * acc_ref[...] + pv
        m_ref[...] = m_new

    def diagonal_tile(kt, lim):
        set_selection_rows(kt)
        rv = slope * (t0 - kt * SEL_TILE).astype(F32)
        mx_ref[...] = jnp.full_like(mx_ref, NEG)
        for i in range(nsub):
            @pl.when(i < lim)
            def _():
                mx_ref[...] = jnp.maximum(mx_ref[...], sub_scores(kt, i, None))

            @pl.when(i == lim)
            def _():
                mx_ref[...] = jnp.maximum(mx_ref[...], sub_scores(kt, i, addlo_ref))

        m_old = m_ref[...]
        m_new = jnp.maximum(m_old, mx_ref[...] - rv)
        shift = m_new + rv
        acc_ref[...] = jnp.exp(m_old - m_new) * acc_ref[...]
        for i in range(nsub):
            @pl.when(i <= lim)
            def _():
                acc_ref[...] += sub_pv(kt, i, shift)

        m_ref[...] = m_new

    kd = t0 // SEL_TILE

    def body(kt, carry):
        @pl.when(any_ref[(g * nqb + qb) * ntile + kt] > 0)
        def _():
            full_tile(kt)
        return carry

    lax.fori_loop(0, kd, body, 0)
    diagonal_tile(kd, (t0 - kd * SEL_TILE) // SUB)
    acc = acc_ref[...]
    o_sel = acc[0:NSA_DH] / acc[NSA_DH:NSA_DH + 1]

    first = qb - WINDOW // Q_BLOCK
    idx = [jnp.maximum(first + i, 0) for i in range(WIN_TILES)]
    consts = []
    mw = None
    for i in range(WIN_TILES):
        s = _dot(kwa_ref[0, idx[i]], qaug_ref[...])
        if i == 0:
            s = s + addhi_ref[...]
        if i == WIN_TILES - 1:
            s = s + addlo_ref[...]
        z_ref[i] = s
        ci = slope * float(i * SUB) + jnp.where(first + i >= 0, 0.0, NEG)
        consts.append(ci)
        mi = jnp.max(s, axis=0, keepdims=True) + ci
        mw = mi if mw is None else jnp.maximum(mw, mi)
    pvw = None
    for i in range(WIN_TILES):
        p = jnp.exp(z_ref[i] - (mw - consts[i])).astype(BF16)
        t = _dot(vwta_ref[0, idx[i]], p)
        pvw = t if pvw is None else pvw + t
    o_win = pvw[0:NSA_DH] / pvw[NSA_DH:NSA_DH + 1]

    gates = jax.nn.sigmoid(gl_ref[0, 0])
    out = gates[0:1] * oc_ref[0, 0] + gates[1:2] * o_sel + gates[2:3] * o_win
    out_t = out.T
    heads = [out_t[r * Q_BLOCK:(r + 1) * Q_BLOCK, :] for r in range(NSA_REP)]
    o_ref[...] = jnp.concatenate(heads, axis=1).astype(o_ref.dtype)


def _nsa_attn(tile_any, qt, ksa, vsta, kwa, vwta, selb, oc, gl, slopes):
    kv, nqb = qt.shape[0], qt.shape[1]
    s = ksa.shape[1]
    nslc = selb.shape[2]
    grid_spec = pltpu.PrefetchScalarGridSpec(
        num_scalar_prefetch=1,
        grid=(kv, nqb),
        in_specs=[
            pl.BlockSpec((1, 1, NSA_DH, QCOLS), lambda g, b, a: (g, b, 0, 0)),
            pl.BlockSpec((1, s, KAUG), lambda g, b, a: (g, 0, 0)),
            pl.BlockSpec((1, s // SEL_TILE, VAUG, SEL_TILE), lambda g, b, a: (g, 0, 0, 0)),
            pl.BlockSpec((1, s // SUB, SUB, KAUG), lambda g, b, a: (g, 0, 0, 0)),
            pl.BlockSpec((1, s // SUB, VAUG, SUB), lambda g, b, a: (g, 0, 0, 0)),
            pl.BlockSpec((1, 1, nslc, Q_BLOCK), lambda g, b, a: (g, b, 0, 0)),
            pl.BlockSpec((1, 1, NSA_DH, QCOLS), lambda g, b, a: (g, b, 0, 0)),
            pl.BlockSpec((1, 1, 3, QCOLS), lambda g, b, a: (g, b, 0, 0)),
            pl.BlockSpec((1, 1, QCOLS), lambda g, b, a: (g, 0, 0)),
        ],
        out_specs=pl.BlockSpec((Q_BLOCK, NSA_REP * NSA_DH), lambda g, b, a: (b, g)),
        scratch_shapes=[
            pltpu.VMEM((KAUG, QCOLS), BF16),
            pltpu.VMEM((SUB, QCOLS), F32),
            pltpu.VMEM((SUB, QCOLS), F32),
            pltpu.VMEM((max(SEL_TILE // SUB, WIN_TILES), SUB, QCOLS), F32),
            pltpu.VMEM((1, QCOLS), F32),
            pltpu.VMEM((1, QCOLS), F32),
            pltpu.VMEM((VAUG, QCOLS), F32),
        ],
    )
    return pl.pallas_call(
        _nsa_attn_kernel,
        out_shape=jax.ShapeDtypeStruct((s, NSA_HEADS * NSA_DH), BF16),
        grid_spec=grid_spec,
        compiler_params=_cparams(("arbitrary", "arbitrary")),
        name="nsa_attn",
    )(tile_any, qt, ksa, vsta, kwa, vwta, selb, oc, gl, slopes)


def _merge_kernel(x_ref, g_ref, ya_ref, yb_ref, yc_ref, wg_ref, bg_ref, wb_ref, wo_ref, o_ref):
    x = x_ref[...]
    h = _rms(x, g_ref[...]).astype(BF16)
    mix = None
    for mi, y_ref in enumerate((ya_ref, yb_ref, yc_ref)):
        cs = slice(mi * D_MODEL, (mi + 1) * D_MODEL)
        gate = jax.nn.sigmoid(_dot(h, wg_ref[:, cs]) + bg_ref[:, cs])
        term = gate * _dot(y_ref[...], wb_ref[mi])
        mix = term if mix is None else mix + term
    o_ref[...] = x + _dot(mix.astype(BF16), wo_ref[...])


def _merge(x, g, ya, yb, yc, wg, bg, wb, wo, tm=256):
    s = x.shape[0]
    row = lambda i: (i, 0)
    fixed2 = lambda i: (0, 0)
    return pl.pallas_call(
        _merge_kernel,
        out_shape=jax.ShapeDtypeStruct((s, D_MODEL), F32),
        grid=(s // tm,),
        in_specs=[
            pl.BlockSpec((tm, D_MODEL), row),
            pl.BlockSpec((1, D_MODEL), fixed2),
            pl.BlockSpec((tm, MIX_WIDTH), row),
            pl.BlockSpec((tm, MIX_WIDTH), row),
            pl.BlockSpec((tm, MIX_WIDTH), row),
            pl.BlockSpec((D_MODEL, N_MIXERS * D_MODEL), fixed2),
            pl.BlockSpec((1, N_MIXERS * D_MODEL), fixed2),
            pl.BlockSpec((N_MIXERS, MIX_WIDTH, D_MODEL), lambda i: (0, 0, 0)),
            pl.BlockSpec((D_MODEL, D_MODEL), fixed2),
        ],
        out_specs=pl.BlockSpec((tm, D_MODEL), row),
        compiler_params=_cparams(("arbitrary",)),
        name="merge",
    )(x, g, ya, yb, yc, wg, bg, wb, wo)


def _alibi_slope_cols():
    h = jnp.arange(1, NSA_HEADS + 1, dtype=F32)
    slopes = (2.0 ** (-8.0 * h / NSA_HEADS)).reshape(NSA_KV, NSA_REP)
    return jnp.repeat(slopes, Q_BLOCK, axis=1).reshape(NSA_KV, 1, QCOLS)


def _overlap_local():
    cl = jnp.arange(CMP_TILE)[None, :]
    jl = jnp.arange(2 * CMP_TILE_SLC)[:, None]
    ov = (cl * CMP_STRIDE < (jl + 1) * SLC_BLOCK) & (cl * CMP_STRIDE + CMP_LEN > jl * SLC_BLOCK)
    return ov.astype(BF16)


def _nsa(pcv, pgl, ksa, kwa, qt, vsta, vwta, cmp_pos, cmp_w1, cmp_w2):
    s = pcv.shape[0]
    nqb = s // Q_BLOCK
    slopes = _alibi_slope_cols()
    kca = _compress(pcv, *_compress_weights(cmp_pos[0], cmp_w1[0], cmp_w2[0], False), False)
    vcta = _compress(pcv, *_compress_weights(cmp_pos[1], cmp_w1[1], cmp_w2[1], True), True)
    oc, selb, anyf = _nsa_cmp(qt, kca, vcta, _overlap_local(), slopes, s // SLC_BLOCK)
    tile_any = (anyf[..., 0] > 0).astype(jnp.int32).reshape(-1)
    ngl = N_NSA_BRANCH * NSA_REP
    gl = pgl.reshape(nqb, Q_BLOCK, NSA_KV, KAUG)[..., :ngl].reshape(nqb, Q_BLOCK, NSA_KV, NSA_REP, N_NSA_BRANCH)
    gl = gl.transpose(2, 0, 4, 3, 1).reshape(NSA_KV, nqb, N_NSA_BRANCH, QCOLS)
    kwa = kwa.reshape(NSA_KV, s // SUB, SUB, KAUG)
    return _nsa_attn(tile_any, qt, ksa, vsta, kwa, vwta, selb, oc, gl, slopes)


def kernel(x, ffn1_norm, ffn1_w1, ffn1_w2, mix_norm, w_in, gm_ln_g, gm_ln_b, gm_ws, gm_bs, ret_gn_g, ret_gn_b,
           cmp_pos, cmp_w1, cmp_w2, w_branch_out, w_merge_gate, b_merge_gate, w_o, ffn2_norm, ffn2_w1, ffn2_w2,
           final_norm):
    bsz, s, _ = x.shape
    depth = ffn1_w1.shape[0]
    row = lambda v: v.reshape(1, -1)
    fin = row(final_norm)
    f1w1, f1w2, f2w1, f2w2 = (_to_bf16(w) for w in (ffn1_w1, ffn1_w2, ffn2_w1, ffn2_w2))
    w_in_b, wg_b, wb_b, wo_b = (_to_bf16(w) for w in (w_in, w_merge_gate, w_branch_out, w_o))
    kcs, kcw, vones = _proj_constants()
    outs = []
    for b in range(bsz):
        xb = x[b]
        for l in range(depth):
            xb = _ffn(xb, row(ffn1_norm[l]), f1w1[l], f1w2[l], fin, False)
            wn, wt = _proj_weights(w_in_b[l])
            pa, pcv, pgl, ksa, kwa, qt, vsta, vwta = _proj(xb, row(mix_norm[l]), wn, wt, kcs, kcw, vones)
            y_a = _gmlp(pa, row(gm_ln_g[l]), row(gm_ln_b[l]), gm_ws[l], gm_bs[l].T)
            y_b = _retention(pa, row(ret_gn_g[l]), row(ret_gn_b[l]))
            y_c = _nsa(pcv, pgl, ksa, kwa, qt, vsta, vwta, cmp_pos[l], cmp_w1[l], cmp_w2[l])
            xb = _merge(xb, row(mix_norm[l]), y_a, y_b, y_c, wg_b[l], row(b_merge_gate[l]), wb_b[l], wo_b[l])
            xb = _ffn(xb, row(ffn2_norm[l]), f2w1[l], f2w2[l], fin, l == depth - 1)
        outs.append(xb)
    return jnp.stack(outs)
```

```python
import functools
import math

import jax
import jax.numpy as jnp
from jax import lax
from jax.experimental import pallas as pl
from jax.experimental.pallas import tpu as pltpu

F32 = jnp.float32
BF16 = jnp.bfloat16

D_MODEL = 1024
D_FF = 2816
NORM_EPS = 1e-6
FFN_RES_WEIGHT = 0.5
GM_WIDTH = 512
GM_GROUPS = 4
GM_CHUNK = 128
RET_HEADS = 4
RET_DK = 64
RET_DV = 128
RET_CHUNK = 128
NSA_HEADS = 8
NSA_KV = 2
NSA_REP = NSA_HEADS // NSA_KV
NSA_DH = 64
CMP_LEN = 32
CMP_STRIDE = 16
CMP_HIDDEN = 128
SLC_BLOCK = 64
N_SELECT = 16
WINDOW = 512
Q_BLOCK = 128
N_MIXERS = 3
MIX_WIDTH = 512
IN_SPLITS = (512, 512, 256, 256, 512, 512, 512, 768, 24)
D_IN = sum(IN_SPLITS)
D_IN_PAD = 4096
BIG = 1e9
NEG = -1e30
PICKED = -(2.0 ** 127)

OFF_GM_U, OFF_GM_V, OFF_RQ, OFF_RK, OFF_RV, OFF_RG, OFF_NQ, OFF_NKV, OFF_NG = (
    0, 512, 1024, 1280, 1536, 2048, 2560, 3072, 3840)

QCOLS = NSA_REP * Q_BLOCK
SEL_TILE = 512
SEL_TILE_BLOCKS = SEL_TILE // SLC_BLOCK
SUB = 128
SUB_BLOCKS = SUB // SLC_BLOCK
FAR_GROUP = 4
KAUG = 128
COL_SEL = NSA_DH
COL_POS = NSA_DH + SEL_TILE_BLOCKS
VAUG = 80
WIN_TILES = (WINDOW + Q_BLOCK) // SUB
CMP_TILE = 128
CMP_TILE_TOKENS = CMP_TILE * CMP_STRIDE
CMP_TILE_SLC = CMP_TILE_TOKENS // SLC_BLOCK
CMP_VARIANT_STEP = 2
N_NSA_BRANCH = 3
KVW = NSA_KV * NSA_DH
PA_WIDTH = OFF_NQ
WN_CV = PA_WIDTH
WN_GL = WN_CV + 2 * KVW
WN_KS = WN_GL + NSA_KV * KAUG
WN_WIDTH = WN_KS + 2 * NSA_KV * KAUG
WT_ROWS = NSA_HEADS * NSA_DH + 2 * KVW
VMEM_LIMIT = 56 * 1024 * 1024
CAST_BLOCK_BYTES = 4 * 1024 * 1024


def _cparams(sem, vmem=VMEM_LIMIT):
    return pltpu.CompilerParams(dimension_semantics=sem, vmem_limit_bytes=vmem)


def _cast_kernel(w_ref, o_ref):
    o_ref[...] = w_ref[...].astype(o_ref.dtype)


def _to_bf16(w):
    w2 = w.reshape(-1, w.shape[-1])
    r, c = w2.shape
    tr = 8
    while r % (2 * tr) == 0 and 2 * tr * c * 4 <= CAST_BLOCK_BYTES:
        tr *= 2
    out = pl.pallas_call(
        _cast_kernel,
        out_shape=jax.ShapeDtypeStruct((r, c), BF16),
        grid=(r // tr,),
        in_specs=[pl.BlockSpec((tr, c), lambda i: (i, 0))],
        out_specs=pl.BlockSpec((tr, c), lambda i: (i, 0)),
        compiler_params=_cparams(("arbitrary",)),
        name="cast_bf16",
    )(w2)
    return out.reshape(w.shape)


def _rms(x, g):
    return x * lax.rsqrt(jnp.mean(x * x, axis=-1, keepdims=True) + NORM_EPS) * g


def _dot(a, b):
    return jnp.dot(a, b, preferred_element_type=F32)


def _ffn_kernel(x_ref, g_ref, w1a_ref, w1b_ref, w2_ref, fin_ref, o_ref, h_ref, acc_ref, *, final_norm):
    f = pl.program_id(1)

    @pl.when(f == 0)
    def _():
        h_ref[...] = _rms(x_ref[...], g_ref[...]).astype(BF16)
        acc_ref[...] = jnp.zeros_like(acc_ref)

    h = h_ref[...]
    a = _dot(h, w1a_ref[...])
    b = _dot(h, w1b_ref[...])
    act = (jax.nn.silu(a) * b).astype(BF16)
    acc_ref[...] += _dot(act, w2_ref[...])

    @pl.when(f == pl.num_programs(1) - 1)
    def _():
        y = x_ref[...] + FFN_RES_WEIGHT * acc_ref[...]
        if final_norm:
            y = _rms(y, fin_ref[...])
        o_ref[...] = y


def _ffn(x, g, w1, w2, fin, final_norm, tm=1024, tf=256):
    s = x.shape[0]
    nf = D_FF // tf
    return pl.pallas_call(
        functools.partial(_ffn_kernel, final_norm=final_norm),
        out_shape=jax.ShapeDtypeStruct((s, D_MODEL), F32),
        grid=(s // tm, nf),
        in_specs=[
            pl.BlockSpec((tm, D_MODEL), lambda i, f: (i, 0)),
            pl.BlockSpec((1, D_MODEL), lambda i, f: (0, 0)),
            pl.BlockSpec((D_MODEL, tf), lambda i, f: (0, f)),
            pl.BlockSpec((D_MODEL, tf), lambda i, f: (0, f + nf)),
            pl.BlockSpec((tf, D_MODEL), lambda i, f: (f, 0)),
            pl.BlockSpec((1, D_MODEL), lambda i, f: (0, 0)),
        ],
        out_specs=pl.BlockSpec((tm, D_MODEL), lambda i, f: (i, 0)),
        scratch_shapes=[pltpu.VMEM((tm, D_MODEL), BF16), pltpu.VMEM((tm, D_MODEL), F32)],
        compiler_params=_cparams(("arbitrary", "arbitrary")),
        name="ffn",
    )(x, g, w1, w1, w2, fin)


def _proj_kernel(x_ref, g_ref, wn_ref, wt_ref, kcs_ref, kcw_ref, vones_ref,
                 pa_ref, pcv_ref, pgl_ref, ksa_ref, kwa_ref, qt_ref, vsta_ref, vwta_ref):
    tm = x_ref.shape[0]
    h = _rms(x_ref[...], g_ref[...]).astype(BF16)
    step = 512
    for c in range(0, PA_WIDTH, step):
        pa_ref[:, c:c + step] = _dot(h, wn_ref[:, c:c + step])
    pcv_ref[...] = _dot(h, wn_ref[:, WN_CV:WN_CV + 2 * KVW])
    pgl_ref[...] = _dot(h, wn_ref[:, WN_GL:WN_GL + 2 * KAUG])
    kk = _dot(h, wn_ref[:, WN_KS:WN_KS + 4 * KAUG])
    for g in range(NSA_KV):
        ksa_ref[g] = (kk[:, g * KAUG:(g + 1) * KAUG] + kcs_ref[...]).astype(BF16)
        kwa_ref[g] = (kk[:, (NSA_KV + g) * KAUG:(NSA_KV + g + 1) * KAUG] + kcw_ref[...]).astype(BF16)
    pt = lax.dot_general(wt_ref[...], h, (((1,), (1,)), ((), ())), preferred_element_type=F32)
    nq = NSA_HEADS * NSA_DH
    for g in range(NSA_KV):
        for b in range(tm // Q_BLOCK):
            for r in range(NSA_REP):
                hd = g * NSA_REP + r
                qt_ref[g, b, :, r * Q_BLOCK:(r + 1) * Q_BLOCK] = (
                    pt[hd * NSA_DH:(hd + 1) * NSA_DH, b * Q_BLOCK:(b + 1) * Q_BLOCK].astype(BF16))
            for v_ref, first_row in ((vsta_ref, nq), (vwta_ref, nq + KVW)):
                rows = slice(first_row + g * NSA_DH, first_row + (g + 1) * NSA_DH)
                v_ref[g, b, 0:NSA_DH, :] = pt[rows, b * SUB:(b + 1) * SUB].astype(BF16)
                v_ref[g, b, NSA_DH:VAUG, :] = vones_ref[...]


def _proj(x, g, wn, wt, kcs, kcw, vones, tm=512):
    s = x.shape[0]
    fixed = lambda i: (0, 0)
    return pl.pallas_call(
        _proj_kernel,
        out_shape=[
            jax.ShapeDtypeStruct((s, PA_WIDTH), F32),
            jax.ShapeDtypeStruct((s, 2 * KVW), F32),
            jax.ShapeDtypeStruct((s, 2 * KAUG), F32),
            jax.ShapeDtypeStruct((NSA_KV, s, KAUG), BF16),
            jax.ShapeDtypeStruct((NSA_KV, s, KAUG), BF16),
            jax.ShapeDtypeStruct((NSA_KV, s // Q_BLOCK, NSA_DH, QCOLS), BF16),
            jax.ShapeDtypeStruct((NSA_KV, s // SUB, VAUG, SUB), BF16),
            jax.ShapeDtypeStruct((NSA_KV, s // SUB, VAUG, SUB), BF16),
        ],
        grid=(s // tm,),
        in_specs=[
            pl.BlockSpec((tm, D_MODEL), lambda i: (i, 0)),
            pl.BlockSpec((1, D_MODEL), fixed),
            pl.BlockSpec((D_MODEL, WN_WIDTH), fixed),
            pl.BlockSpec((WT_ROWS, D_MODEL), fixed),
            pl.BlockSpec((tm, KAUG), fixed),
            pl.BlockSpec((tm, KAUG), fixed),
            pl.BlockSpec((VAUG - NSA_DH, SUB), fixed),
        ],
        out_specs=[
            pl.BlockSpec((tm, PA_WIDTH), lambda i: (i, 0)),
            pl.BlockSpec((tm, 2 * KVW), lambda i: (i, 0)),
            pl.BlockSpec((tm, 2 * KAUG), lambda i: (i, 0)),
            pl.BlockSpec((NSA_KV, tm, KAUG), lambda i: (0, i, 0)),
            pl.BlockSpec((NSA_KV, tm, KAUG), lambda i: (0, i, 0)),
            pl.BlockSpec((NSA_KV, tm // Q_BLOCK, NSA_DH, QCOLS), lambda i: (0, i, 0, 0)),
            pl.BlockSpec((NSA_KV, tm // SUB, VAUG, SUB), lambda i: (0, i, 0, 0)),
            pl.BlockSpec((NSA_KV, tm // SUB, VAUG, SUB), lambda i: (0, i, 0, 0)),
        ],
        compiler_params=_cparams(("arbitrary",)),
        name="proj",
    )(x, g, wn, wt, kcs, kcw, vones)


def _proj_weights(w_in):
    w = w_in.astype(BF16)
    z = lambda n: jnp.zeros((D_MODEL, n), BF16)
    kv = lambda i, g: w[:, OFF_NKV + i * KVW + g * NSA_DH:OFF_NKV + i * KVW + (g + 1) * NSA_DH]
    ngl = N_NSA_BRANCH * NSA_REP
    cols = [w[:, 0:PA_WIDTH], w[:, OFF_NKV:OFF_NKV + 2 * KVW]]
    for g in range(NSA_KV):
        cols += [w[:, OFF_NG + g * ngl:OFF_NG + (g + 1) * ngl], z(KAUG - ngl)]
    for i in (2, 4):
        for g in range(NSA_KV):
            cols += [kv(i, g), z(KAUG - NSA_DH)]
    wn = jnp.concatenate(cols, axis=1)
    wt = jnp.concatenate([w[:, OFF_NQ:OFF_NQ + NSA_HEADS * NSA_DH], kv(3, 0), kv(3, 1), kv(5, 0), kv(5, 1)], axis=1).T
    return wn, wt


def _proj_constants(tm=512):
    pos = jnp.arange(tm)
    zero = jnp.zeros((tm, NSA_DH), F32)

    def pos_cols(p):
        return jnp.stack([p // 16 * 16, p % 16], axis=1).astype(F32)

    onehot = (pos[:, None] // SLC_BLOCK % SEL_TILE_BLOCKS == jnp.arange(SEL_TILE_BLOCKS)[None, :]).astype(F32)
    tail = jnp.zeros((tm, KAUG - COL_POS - 2), F32)
    kcs = jnp.concatenate([zero, onehot, pos_cols(pos % SUB), tail], axis=1)
    kcw = jnp.concatenate([zero, jnp.zeros_like(onehot), pos_cols(pos % SUB), tail], axis=1)
    r = jnp.arange(VAUG - NSA_DH)[:, None]
    vones = jnp.broadcast_to(jnp.where(r == 0, 1.0, 0.0), (VAUG - NSA_DH, SUB)).astype(BF16)
    return kcs, kcw, vones


def _gmlp_kernel(u_ref, v_ref, lg_ref, lb_ref, ws_ref, bst_ref, o_ref, *, chunks):
    cg = GM_WIDTH // GM_GROUPS
    u = jax.nn.gelu(u_ref[...])
    v = jax.nn.gelu(v_ref[...])
    mu = jnp.mean(v, axis=-1, keepdims=True)
    vc = v - mu
    var = jnp.mean(vc * vc, axis=-1, keepdims=True)
    vn = (vc * lax.rsqrt(var + NORM_EPS) * lg_ref[...] + lb_ref[...]).astype(BF16)
    row = lax.broadcasted_iota(jnp.int32, (GM_CHUNK, GM_CHUNK), 0)
    col = lax.broadcasted_iota(jnp.int32, (GM_CHUNK, GM_CHUNK), 1)
    causal = row >= col
    bst = bst_ref[...]
    for gi in range(GM_GROUPS):
        w = jnp.where(causal, ws_ref[gi], 0.0).astype(BF16)
        bias = bst[:, gi:gi + 1]
        for c in range(chunks):
            rs = slice(c * GM_CHUNK, (c + 1) * GM_CHUNK)
            cs = slice(gi * cg, (gi + 1) * cg)
            sv = _dot(w, vn[rs, cs]) + bias
            o_ref[rs, cs] = (u[rs, cs] * sv).astype(o_ref.dtype)


def _gmlp(proj, ln_g, ln_b, ws, bst, tm=512):
    s = proj.shape[0]
    return pl.pallas_call(
        functools.partial(_gmlp_kernel, chunks=tm // GM_CHUNK),
        out_shape=jax.ShapeDtypeStruct((s, GM_WIDTH), BF16),
        grid=(s // tm,),
        in_specs=[
            pl.BlockSpec((tm, GM_WIDTH), lambda i: (i, OFF_GM_U // GM_WIDTH)),
            pl.BlockSpec((tm, GM_WIDTH), lambda i: (i, OFF_GM_V // GM_WIDTH)),
            pl.BlockSpec((1, GM_WIDTH), lambda i: (0, 0)),
            pl.BlockSpec((1, GM_WIDTH), lambda i: (0, 0)),
            pl.BlockSpec((GM_GROUPS, GM_CHUNK, GM_CHUNK), lambda i: (0, 0, 0)),
            pl.BlockSpec((GM_CHUNK, GM_GROUPS), lambda i: (0, 0)),
        ],
        out_specs=pl.BlockSpec((tm, GM_WIDTH), lambda i: (i, 0)),
        compiler_params=_cparams(("arbitrary",)),
        name="gmlp",
    )(proj, proj, ln_g, ln_b, ws, bst)


def _ret_kernel(q_ref, k_ref, v_ref, g_ref, gg_ref, gb_ref, o_ref, st_ref, *, chunks):
    @pl.when(pl.program_id(0) == 0)
    def _():
        st_ref[...] = jnp.zeros_like(st_ref)

    c = RET_CHUNK
    ri = lax.broadcasted_iota(jnp.int32, (c, c), 0)
    ci = lax.broadcasted_iota(jnp.int32, (c, c), 1)
    diff = (ri - ci).astype(F32)
    pos_k = lax.broadcasted_iota(jnp.int32, (c, RET_DK), 0).astype(F32)
    pos_v = lax.broadcasted_iota(jnp.int32, (c, RET_DV), 0).astype(F32)
    for h in range(RET_HEADS):
        log_gamma = math.log(1.0 - 2.0 ** (-5.0 - h))
        intra_decay = jnp.where(diff >= 0, jnp.exp(log_gamma * jnp.maximum(diff, 0.0)), 0.0)
        k_decay = jnp.exp(log_gamma * (c - 1.0 - pos_k))
        q_decay = jnp.exp(log_gamma * (pos_v + 1.0))
        chunk_decay = math.exp(log_gamma * c)
        gam = gg_ref[:, h * RET_DV:(h + 1) * RET_DV]
        bet = gb_ref[:, h * RET_DV:(h + 1) * RET_DV]
        for n in range(chunks):
            rs = slice(n * c, (n + 1) * c)
            q = q_ref[rs, h * RET_DK:(h + 1) * RET_DK].astype(BF16)
            kf = k_ref[rs, h * RET_DK:(h + 1) * RET_DK] * (RET_DK ** -0.5)
            k = kf.astype(BF16)
            v = v_ref[rs, h * RET_DV:(h + 1) * RET_DV].astype(BF16)
            scores = lax.dot_general(q, k, (((1,), (1,)), ((), ())), preferred_element_type=F32) * intra_decay
            intra = _dot(scores.astype(BF16), v)
            state = st_ref[h]
            cross = _dot(q, state.astype(BF16)) * q_decay
            kd = (kf * k_decay).astype(BF16)
            kv = lax.dot_general(kd, v, (((0,), (0,)), ((), ())), preferred_element_type=F32)
            st_ref[h] = state * chunk_decay + kv
            y = intra + cross
            mu = jnp.mean(y, axis=-1, keepdims=True)
            yc = y - mu
            var = jnp.mean(yc * yc, axis=-1, keepdims=True)
            yn = yc * lax.rsqrt(var + NORM_EPS) * gam + bet
            gate = g_ref[rs, h * RET_DV:(h + 1) * RET_DV]
            o_ref[rs, h * RET_DV:(h + 1) * RET_DV] = (jax.nn.silu(gate) * yn).astype(o_ref.dtype)


def _retention(proj, gn_g, gn_b, tm=512):
    s = proj.shape[0]
    hk = RET_HEADS * RET_DK
    hv = RET_HEADS * RET_DV
    return pl.pallas_call(
        functools.partial(_ret_kernel, chunks=tm // RET_CHUNK),
        out_shape=jax.ShapeDtypeStruct((s, hv), BF16),
        grid=(s // tm,),
        in_specs=[
            pl.BlockSpec((tm, hk), lambda i: (i, OFF_RQ // hk)),
            pl.BlockSpec((tm, hk), lambda i: (i, OFF_RK // hk)),
            pl.BlockSpec((tm, hv), lambda i: (i, OFF_RV // hv)),
            pl.BlockSpec((tm, hv), lambda i: (i, OFF_RG // hv)),
            pl.BlockSpec((1, hv), lambda i: (0, 0)),
            pl.BlockSpec((1, hv), lambda i: (0, 0)),
        ],
        out_specs=pl.BlockSpec((tm, hv), lambda i: (i, 0)),
        scratch_shapes=[pltpu.VMEM((RET_HEADS, RET_DK, RET_DV), F32)],
        compiler_params=_cparams(("arbitrary",)),
        name="retention",
    )(proj, proj, proj, proj, gn_g, gn_b)


def _compress_kernel(t_ref, pos_ref, w1_ref, w2_ref, o_ref, xlo_ref, xhi_ref, *, values):
    n = t_ref.shape[0] // CMP_STRIDE
    for i in range(CMP_STRIDE):
        x = t_ref[pl.ds(i, n, stride=CMP_STRIDE), :]
        xlo_ref[:, i * KVW:(i + 1) * KVW] = (x + pos_ref[0:1, i * KVW:(i + 1) * KVW]).astype(BF16)
        xhi_ref[:, i * KVW:(i + 1) * KVW] = (x + pos_ref[1:2, i * KVW:(i + 1) * KVW]).astype(BF16)
    lo = _dot(xlo_ref[...], w1_ref[0])
    hi = _dot(xhi_ref[...], w1_ref[1])
    hi_next = jnp.concatenate([hi[1:], hi[:1]], axis=0)
    rowi = lax.broadcasted_iota(jnp.int32, lo.shape, 0)
    hid = jnp.where(rowi < n - 1, jax.nn.gelu(lo + hi_next), 0.0).astype(BF16)
    for g in range(NSA_KV):
        if values:
            vt = lax.dot_general(w2_ref[g], hid, (((1,), (1,)), ((), ())), preferred_element_type=F32)
            r = lax.broadcasted_iota(jnp.int32, vt.shape, 0)
            vt = jnp.where(r == NSA_DH, 1.0, vt).astype(BF16)
            for c in range(n // CMP_TILE):
                o_ref[g, c] = vt[:, c * CMP_TILE:(c + 1) * CMP_TILE]
        else:
            k = _dot(hid, w2_ref[g])
            col = lax.broadcasted_iota(jnp.int32, k.shape, 1)
            blk = (lax.broadcasted_iota(jnp.int32, k.shape, 0) & (CMP_TILE - 1)) * CMP_STRIDE
            k = jnp.where(col == NSA_DH, blk.astype(F32), k).astype(BF16)
            for c in range(n // CMP_TILE):
                o_ref[g, c] = k[c * CMP_TILE:(c + 1) * CMP_TILE, :]


def _compress(pcv, posrows, w1, w2, values):
    s = pcv.shape[0]
    n = s // CMP_STRIDE
    nct = n // CMP_TILE
    out_tile = (VAUG, CMP_TILE) if values else (CMP_TILE, KAUG)
    width = CMP_STRIDE * KVW
    return pl.pallas_call(
        functools.partial(_compress_kernel, values=values),
        out_shape=jax.ShapeDtypeStruct((NSA_KV, nct) + out_tile, BF16),
        grid=(1,),
        in_specs=[
            pl.BlockSpec((s, KVW), lambda i: (0, 1 if values else 0)),
            pl.BlockSpec((2, width), lambda i: (0, 0)),
            pl.BlockSpec((2, width, NSA_KV * CMP_HIDDEN), lambda i: (0, 0, 0)),
            pl.BlockSpec((NSA_KV,) + w2.shape[1:], lambda i: (0, 0, 0)),
        ],
        out_specs=pl.BlockSpec((NSA_KV, nct) + out_tile, lambda i: (0, 0, 0, 0)),
        scratch_shapes=[pltpu.VMEM((n, width), BF16), pltpu.VMEM((n, width), BF16)],
        compiler_params=_cparams(("arbitrary",)),
        name="nsa_compress_v" if values else "nsa_compress_k",
    )(pcv, posrows, w1, w2)


def _compress_weights(pos, w1, w2, values):
    posrows = jnp.tile(pos.reshape(2, CMP_STRIDE, 1, NSA_DH), (1, 1, NSA_KV, 1)).reshape(2, CMP_STRIDE * KVW)
    w = w1.astype(BF16).reshape(2, CMP_STRIDE, NSA_DH, CMP_HIDDEN)
    eye = jnp.eye(NSA_KV, dtype=BF16)
    w1b = jnp.einsum('hidc,gk->higdkc', w, eye).reshape(2, CMP_STRIDE * KVW, NSA_KV * CMP_HIDDEN)
    w2g = jnp.einsum('cd,gk->gkcd', w2.astype(BF16), eye).reshape(NSA_KV, NSA_KV * CMP_HIDDEN, NSA_DH)
    if values:
        w2g = jnp.pad(w2g.transpose(0, 2, 1), ((0, 0), (0, VAUG - NSA_DH), (0, 0)))
    else:
        w2g = jnp.pad(w2g, ((0, 0), (0, 0), (0, KAUG - NSA_DH)))
    return posrows, w1b, w2g


def _nsa_cmp_kernel(qt_ref, kca_ref, vcta_ref, ovl_ref, slope_ref, oc_ref, selb_ref, any_ref, dmask_ref, jf_ref):
    qb = pl.program_id(0)
    nct = kca_ref.shape[1]
    nslc = selb_ref.shape[2]
    nsub = nslc // SUB_BLOCKS

    @pl.when(qb == 0)
    def _():
        cl = lax.broadcasted_iota(jnp.int32, (CMP_TILE, QCOLS), 0)
        q = lax.broadcasted_iota(jnp.int32, (CMP_TILE, QCOLS), 1) & (Q_BLOCK - 1)
        dmask_ref[...] = (cl * CMP_STRIDE - q).astype(F32)
        jf_ref[...] = lax.broadcasted_iota(jnp.int32, jf_ref.shape, 0).astype(F32)

    t0 = qb * Q_BLOCK
    r16 = lax.broadcasted_iota(jnp.int32, (16, QCOLS), 0)
    td = t0 // CMP_TILE_TOKENS

    def colmax(s):
        return jnp.max(s, axis=0, keepdims=True)

    def variant(nt, g):
        slope = slope_ref[g]
        qa = jnp.concatenate([qt_ref[g, 0] * (NSA_DH ** -0.5), jnp.where(r16 == 0, slope, 0.0).astype(BF16),
                              jnp.zeros((KAUG - NSA_DH - 16, QCOLS), BF16)], axis=0)
        nr = nt * CMP_TILE_SLC
        offs = [slope * (t0 - ti * CMP_TILE_TOKENS).astype(F32) for ti in range(nt)]
        zs, m = [], None
        for ti in range(nt):
            s = _dot(kca_ref[g, ti], qa)
            if ti >= nt - CMP_VARIANT_STEP - 1:
                lim = (t0 - ti * CMP_TILE_TOKENS - (CMP_LEN - 1)).astype(F32)
                s = jnp.where(dmask_ref[...] <= lim, s, NEG)
            zs.append(s)
            cand = colmax(s) - offs[ti]
            m = cand if m is None else jnp.maximum(m, cand)
        acc, pieces = None, []
        for ti in range(nt):
            p = jnp.exp(zs[ti] - (m + offs[ti])).astype(BF16)
            t = _dot(vcta_ref[g, ti], p)
            acc = t if acc is None else acc + t
            pieces.append(_dot(ovl_ref[...], p))
        inv = jnp.where(m > 0.5 * NEG, 1.0 / acc[NSA_DH:NSA_DH + 1], 0.0)
        oc_ref[g, 0] = acc[0:NSA_DH] * inv
        blocks = []
        for b in range(nt):
            part = pieces[b][0:CMP_TILE_SLC]
            if b >= 1:
                part = part + pieces[b - 1][CMP_TILE_SLC:2 * CMP_TILE_SLC]
            blocks.append(part)
        imp4 = jnp.concatenate(blocks, axis=0) * inv
        imp = imp4[:, 0:Q_BLOCK]
        for r in range(1, NSA_REP):
            imp = imp + imp4[:, r * Q_BLOCK:(r + 1) * Q_BLOCK]
        j = lax.broadcasted_iota(jnp.int32, (nr, Q_BLOCK), 0)
        qq = lax.broadcasted_iota(jnp.int32, (nr, Q_BLOCK), 1)
        cur = 2 * qb + jnp.where(qq >= SLC_BLOCK, 1, 0)
        forced = (j == 0) | (j == cur) | (j == cur - 1)
        imp = jnp.where(j > cur, -BIG, imp)
        imp = jnp.where(forced, PICKED, imp)
        jf = jf_ref[0:nr, :]
        for _ in range(max(min(N_SELECT, nslc) - 3, 0)):
            best = jnp.max(imp, axis=0, keepdims=True)
            first = jnp.min(jnp.where(imp == best, jf, float(nslc)), axis=0, keepdims=True)
            imp = jnp.where(jf == first, PICKED, imp)
        sel = imp == PICKED
        selb_ref[g, 0, 0:nr, :] = jnp.where(sel, 0.0, NEG)
        if nr < nslc:
            selb_ref[g, 0, nr:nslc, :] = jnp.full((nslc - nr, Q_BLOCK), NEG, F32)
        ti = lax.broadcasted_iota(jnp.int32, (nsub, nr), 0) * SUB_BLOCKS
        tj = lax.broadcasted_iota(jnp.int32, (nsub, nr), 1)
        member = jnp.where((tj >= ti) & (tj < ti + SUB_BLOCKS), 1.0, 0.0).astype(BF16)
        count = _dot(member, jnp.where(sel, 1.0, 0.0).astype(BF16))
        any_ref[g, 0] = lax.dot_general(jnp.ones((8, Q_BLOCK), BF16), count.astype(BF16),
                                        (((1,), (1,)), ((), ())), preferred_element_type=F32)

    def both_groups(nt):
        for g in range(NSA_KV):
            variant(nt, g)

    for k in range(-(-nct // CMP_VARIANT_STEP)):
        pl.when(td // CMP_VARIANT_STEP == k)(functools.partial(both_groups, min((k + 1) * CMP_VARIANT_STEP, nct)))


def _nsa_cmp(qt, kca, vcta, ovl, slopes, nslc):
    kv, nqb = qt.shape[0], qt.shape[1]
    nct = kca.shape[1]
    nsub = nslc // SUB_BLOCKS
    return pl.pallas_call(
        _nsa_cmp_kernel,
        out_shape=[
            jax.ShapeDtypeStruct((kv, nqb, NSA_DH, QCOLS), F32),
            jax.ShapeDtypeStruct((kv, nqb, nslc, Q_BLOCK), F32),
            jax.ShapeDtypeStruct((kv, nqb, 8, nsub), F32),
        ],
        grid=(nqb,),
        in_specs=[
            pl.BlockSpec((kv, 1, NSA_DH, QCOLS), lambda b: (0, b, 0, 0)),
            pl.BlockSpec((kv, nct, CMP_TILE, KAUG), lambda b: (0, 0, 0, 0)),
            pl.BlockSpec((kv, nct, VAUG, CMP_TILE), lambda b: (0, 0, 0, 0)),
            pl.BlockSpec((2 * CMP_TILE_SLC, CMP_TILE), lambda b: (0, 0)),
            pl.BlockSpec((kv, 1, QCOLS), lambda b: (0, 0, 0)),
        ],
        out_specs=[
            pl.BlockSpec((kv, 1, NSA_DH, QCOLS), lambda b: (0, b, 0, 0)),
            pl.BlockSpec((kv, 1, nslc, Q_BLOCK), lambda b: (0, b, 0, 0)),
            pl.BlockSpec((kv, 1, 8, nsub), lambda b: (0, b, 0, 0)),
        ],
        scratch_shapes=[pltpu.VMEM((CMP_TILE, QCOLS), F32), pltpu.VMEM((nslc, Q_BLOCK), F32)],
        compiler_params=_cparams(("arbitrary",)),
        name="nsa_cmp_topk",
    )(qt, kca, vcta, ovl, slopes)


def _nsa_attn_kernel(cnt_ref, lst_ref, qt_ref, ksa_ref, vsta_ref, kwa_ref, vwta_ref, selb_ref, oc_ref, gl_ref,
                     slope_ref, o_ref, addlo_ref, addhi_ref, z_ref, m_ref, acc_ref):
    g = pl.program_id(0)
    qb = pl.program_id(1)
    nqb = pl.num_programs(1)
    nsub = ksa_ref.shape[1]
    slope = slope_ref[0]

    @pl.when(qb == 0)
    def _():
        cl = lax.broadcasted_iota(jnp.int32, (SUB, QCOLS), 0)
        q = lax.broadcasted_iota(jnp.int32, (SUB, QCOLS), 1) & (Q_BLOCK - 1)
        addlo_ref[...] = jnp.where(cl <= q, 0.0, NEG)
        addhi_ref[...] = jnp.where(cl > q, 0.0, NEG)

    t0 = qb * Q_BLOCK
    q = qt_ref[0, 0] * (NSA_DH ** -0.5)
    m_ref[...] = jnp.full_like(m_ref, NEG)
    acc_ref[...] = jnp.zeros_like(acc_ref)
    r8 = lax.broadcasted_iota(jnp.int32, (SEL_TILE_BLOCKS, QCOLS), 0)
    slope_rows = jnp.where(r8 < 2, slope, 0.0)
    zero_rows = jnp.zeros((KAUG - COL_SEL - 16, QCOLS), BF16)
    q_win = jnp.concatenate([q, jnp.concatenate([jnp.zeros_like(slope_rows), slope_rows], axis=0).astype(BF16),
                             zero_rows], axis=0)

    def q_sel(a):
        grp = a // (SEL_TILE_BLOCKS // SUB_BLOCKS)
        sb = selb_ref[0, 0, pl.ds(pl.multiple_of(grp * SEL_TILE_BLOCKS, SEL_TILE_BLOCKS), SEL_TILE_BLOCKS), :]
        sb = jnp.concatenate([sb] * NSA_REP, axis=1)
        return jnp.concatenate([q, jnp.concatenate([sb, slope_rows], axis=0).astype(BF16), zero_rows], axis=0)

    def colmax(s):
        return jnp.max(s, axis=0, keepdims=True)

    def weigh(v_ref, subs, slots, shifts):
        pv = None
        for a, slot, shift in zip(subs, slots, shifts):
            p = jnp.exp(z_ref[slot] - shift).astype(BF16)
            t = _dot(v_ref[0, a], p)
            pv = t if pv is None else pv + t
        return pv

    base = (g * nqb + qb) * nsub
    count = cnt_ref[g * nqb + qb]

    def far_group(gi, carry):
        subs, offs, mg = [], [], None
        for u in range(FAR_GROUP):
            e = gi * FAR_GROUP + u
            a = lst_ref[base + e]
            s = _dot(ksa_ref[0, a], q_sel(a))
            z_ref[u] = s
            off = slope * (t0 - a * SUB).astype(F32) + jnp.where(e < count, 0.0, -NEG)
            cand = colmax(s) - off
            mg = cand if mg is None else jnp.maximum(mg, cand)
            subs.append(a)
            offs.append(off)
        m_old = m_ref[...]
        m_new = jnp.maximum(m_old, mg)
        pv = weigh(vsta_ref, subs, range(FAR_GROUP), [m_new + off for off in offs])
        acc_ref[...] = jnp.exp(m_old - m_new) * acc_ref[...] + pv
        m_ref[...] = m_new
        return carry

    lax.fori_loop(0, (count + FAR_GROUP - 1) // FAR_GROUP, far_group, 0)

    first = qb - WINDOW // Q_BLOCK
    near = [jnp.maximum(first + i, 0) for i in range(WIN_TILES)]
    gone = [jnp.where(first + i >= 0, 0.0, -NEG) for i in range(WIN_TILES)]
    offs_s, offs_w, mg, mw = [], [], None, None
    for i in range(WIN_TILES):
        s = _dot(ksa_ref[0, near[i]], q_sel(near[i]))
        w = _dot(kwa_ref[0, near[i]], q_win)
        if i == 0:
            w = w + addhi_ref[...]
        if i == WIN_TILES - 1:
            s = s + addlo_ref[...]
            w = w + addlo_ref[...]
        z_ref[i] = s
        z_ref[WIN_TILES + i] = w
        off_s = slope * float((WIN_TILES - 1 - i) * SUB) + gone[i]
        off_w = gone[i] - slope * float(i * SUB)
        cs, cw = colmax(s) - off_s, colmax(w) - off_w
        mg = cs if mg is None else jnp.maximum(mg, cs)
        mw = cw if mw is None else jnp.maximum(mw, cw)
        offs_s.append(off_s)
        offs_w.append(off_w)
    m_old = m_ref[...]
    m_new = jnp.maximum(m_old, mg)
    acc = jnp.exp(m_old - m_new) * acc_ref[...] + weigh(vsta_ref, near, range(WIN_TILES), [m_new + o for o in offs_s])
    o_sel = acc[0:NSA_DH] / acc[NSA_DH:NSA_DH + 1]
    pvw = weigh(vwta_ref, near, range(WIN_TILES, 2 * WIN_TILES), [mw + o for o in offs_w])
    o_win = pvw[0:NSA_DH] / pvw[NSA_DH:NSA_DH + 1]

    gates = jax.nn.sigmoid(gl_ref[0, 0])
    out = gates[0:1] * oc_ref[0, 0] + gates[1:2] * o_sel + gates[2:3] * o_win
    out_t = out.T
    heads = [out_t[r * Q_BLOCK:(r + 1) * Q_BLOCK, :] for r in range(NSA_REP)]
    o_ref[...] = jnp.concatenate(heads, axis=1).astype(o_ref.dtype)


def _nsa_attn(far_count, far_list, qt, ksa, vsta, kwa, vwta, selb, oc, gl, slopes):
    kv, nqb = qt.shape[0], qt.shape[1]
    nsub = ksa.shape[1]
    nslc = selb.shape[2]
    blk = lambda g, b, c, l: (g, b, 0, 0)
    grp = lambda g, b, c, l: (g, 0, 0, 0)
    grid_spec = pltpu.PrefetchScalarGridSpec(
        num_scalar_prefetch=2,
        grid=(kv, nqb),
        in_specs=[
            pl.BlockSpec((1, 1, NSA_DH, QCOLS), blk),
            pl.BlockSpec((1, nsub, SUB, KAUG), grp),
            pl.BlockSpec((1, nsub, VAUG, SUB), grp),
            pl.BlockSpec((1, nsub, SUB, KAUG), grp),
            pl.BlockSpec((1, nsub, VAUG, SUB), grp),
            pl.BlockSpec((1, 1, nslc, Q_BLOCK), blk),
            pl.BlockSpec((1, 1, NSA_DH, QCOLS), blk),
            pl.BlockSpec((1, 1, N_NSA_BRANCH, QCOLS), blk),
            pl.BlockSpec((1, 1, QCOLS), lambda g, b, c, l: (g, 0, 0)),
        ],
        out_specs=pl.BlockSpec((Q_BLOCK, NSA_REP * NSA_DH), lambda g, b, c, l: (b, g)),
        scratch_shapes=[
            pltpu.VMEM((SUB, QCOLS), F32),
            pltpu.VMEM((SUB, QCOLS), F32),
            pltpu.VMEM((2 * WIN_TILES, SUB, QCOLS), F32),
            pltpu.VMEM((1, QCOLS), F32),
            pltpu.VMEM((VAUG, QCOLS), F32),
        ],
    )
    return pl.pallas_call(
        _nsa_attn_kernel,
        out_shape=jax.ShapeDtypeStruct((nsub * SUB, NSA_HEADS * NSA_DH), BF16),
        grid_spec=grid_spec,
        compiler_params=_cparams(("arbitrary", "arbitrary")),
        name="nsa_attn",
    )(far_count, far_list, qt, ksa, vsta, kwa, vwta, selb, oc, gl, slopes)


def _far_lists(picks, nqb):
    nsub = picks.shape[-1]
    a = jnp.arange(nsub)[None, None, :]
    far = a < (jnp.arange(nqb)[None, :, None] - WINDOW // Q_BLOCK)
    active = (picks > 0) & far
    count = jnp.sum(active, axis=-1).astype(jnp.int32)
    order = jnp.argsort(jnp.where(active, 0, 1), axis=-1, stable=True).astype(jnp.int32)
    lst = jnp.where(a < count[..., None], order, 0)
    return count.reshape(-1), lst.reshape(-1)


def _merge_kernel(x_ref, g_ref, ya_ref, yb_ref, yc_ref, wg_ref, bg_ref, wb_ref, wo_ref, o_ref):
    x = x_ref[...]
    h = _rms(x, g_ref[...]).astype(BF16)
    mix = None
    for mi, y_ref in enumerate((ya_ref, yb_ref, yc_ref)):
        cs = slice(mi * D_MODEL, (mi + 1) * D_MODEL)
        gate = jax.nn.sigmoid(_dot(h, wg_ref[:, cs]) + bg_ref[:, cs])
        term = gate * _dot(y_ref[...], wb_ref[mi])
        mix = term if mix is None else mix + term
    o_ref[...] = x + _dot(mix.astype(BF16), wo_ref[...])


def _merge(x, g, ya, yb, yc, wg, bg, wb, wo, tm=256):
    s = x.shape[0]
    row = lambda i: (i, 0)
    fixed2 = lambda i: (0, 0)
    return pl.pallas_call(
        _merge_kernel,
        out_shape=jax.ShapeDtypeStruct((s, D_MODEL), F32),
        grid=(s // tm,),
        in_specs=[
            pl.BlockSpec((tm, D_MODEL), row),
            pl.BlockSpec((1, D_MODEL), fixed2),
            pl.BlockSpec((tm, MIX_WIDTH), row),
            pl.BlockSpec((tm, MIX_WIDTH), row),
            pl.BlockSpec((tm, MIX_WIDTH), row),
            pl.BlockSpec((D_MODEL, N_MIXERS * D_MODEL), fixed2),
            pl.BlockSpec((1, N_MIXERS * D_MODEL), fixed2),
            pl.BlockSpec((N_MIXERS, MIX_WIDTH, D_MODEL), lambda i: (0, 0, 0)),
            pl.BlockSpec((D_MODEL, D_MODEL), fixed2),
        ],
        out_specs=pl.BlockSpec((tm, D_MODEL), row),
        compiler_params=_cparams(("arbitrary",)),
        name="merge",
    )(x, g, ya, yb, yc, wg, bg, wb, wo)


def _alibi_slope_cols():
    h = jnp.arange(1, NSA_HEADS + 1, dtype=F32)
    slopes = (2.0 ** (-8.0 * h / NSA_HEADS)).reshape(NSA_KV, NSA_REP)
    return jnp.repeat(slopes, Q_BLOCK, axis=1).reshape(NSA_KV, 1, QCOLS)


def _overlap_local():
    cl = jnp.arange(CMP_TILE)[None, :]
    jl = jnp.arange(2 * CMP_TILE_SLC)[:, None]
    ov = (cl * CMP_STRIDE < (jl + 1) * SLC_BLOCK) & (cl * CMP_STRIDE + CMP_LEN > jl * SLC_BLOCK)
    return ov.astype(BF16)


def _nsa(pcv, pgl, ksa, kwa, qt, vsta, vwta, cmp_pos, cmp_w1, cmp_w2):
    s = pcv.shape[0]
    nqb = s // Q_BLOCK
    slopes = _alibi_slope_cols()
    kca = _compress(pcv, *_compress_weights(cmp_pos[0], cmp_w1[0], cmp_w2[0], False), False)
    vcta = _compress(pcv, *_compress_weights(cmp_pos[1], cmp_w1[1], cmp_w2[1], True), True)
    oc, selb, picks = _nsa_cmp(qt, kca, vcta, _overlap_local(), slopes, s // SLC_BLOCK)
    far_count, far_list = _far_lists(picks[:, :, 0, :], nqb)
    ngl = N_NSA_BRANCH * NSA_REP
    gl = pgl.reshape(nqb, Q_BLOCK, NSA_KV, KAUG)[..., :ngl].reshape(nqb, Q_BLOCK, NSA_KV, NSA_REP, N_NSA_BRANCH)
    gl = gl.transpose(2, 0, 4, 3, 1).reshape(NSA_KV, nqb, N_NSA_BRANCH, QCOLS)
    ksa = ksa.reshape(NSA_KV, s // SUB, SUB, KAUG)
    kwa = kwa.reshape(NSA_KV, s // SUB, SUB, KAUG)
    return _nsa_attn(far_count, far_list, qt, ksa, vsta, kwa, vwta, selb, oc, gl, slopes)


def kernel(x, ffn1_norm, ffn1_w1, ffn1_w2, mix_norm, w_in, gm_ln_g, gm_ln_b, gm_ws, gm_bs, ret_gn_g, ret_gn_b,
           cmp_pos, cmp_w1, cmp_w2, w_branch_out, w_merge_gate, b_merge_gate, w_o, ffn2_norm, ffn2_w1, ffn2_w2,
           final_norm):
    bsz, s, _ = x.shape
    depth = ffn1_w1.shape[0]
    row = lambda v: v.reshape(1, -1)
    fin = row(final_norm)
    f1w1, f1w2, f2w1, f2w2 = (_to_bf16(w) for w in (ffn1_w1, ffn1_w2, ffn2_w1, ffn2_w2))
    w_in_b, wg_b, wb_b, wo_b = (_to_bf16(w) for w in (w_in, w_merge_gate, w_branch_out, w_o))
    kcs, kcw, vones = _proj_constants()
    outs = []
    for b in range(bsz):
        xb = x[b]
        for l in range(depth):
            xb = _ffn(xb, row(ffn1_norm[l]), f1w1[l], f1w2[l], fin, False)
            wn, wt = _proj_weights(w_in_b[l])
            pa, pcv, pgl, ksa, kwa, qt, vsta, vwta = _proj(xb, row(mix_norm[l]), wn, wt, kcs, kcw, vones)
            y_a = _gmlp(pa, row(gm_ln_g[l]), row(gm_ln_b[l]), gm_ws[l], gm_bs[l].T)
            y_b = _retention(pa, row(ret_gn_g[l]), row(ret_gn_b[l]))
            y_c = _nsa(pcv, pgl, ksa, kwa, qt, vsta, vwta, cmp_pos[l], cmp_w1[l], cmp_w2[l])
            xb = _merge(xb, row(mix_norm[l]), y_a, y_b, y_c, wg_b[l], row(b_merge_gate[l]), wb_b[l], wo_b[l])
            xb = _ffn(xb, row(ffn2_norm[l]), f2w1[l], f2w2[l], fin, l == depth - 1)
        outs.append(xb)
    return jnp.stack(outs)
```

```python
import functools
import math

import jax
import jax.numpy as jnp
from jax import lax
from jax.experimental import pallas as pl
from jax.experimental.pallas import tpu as pltpu

F32 = jnp.float32
BF16 = jnp.bfloat16

D_MODEL = 1024
D_FF = 2816
NORM_EPS = 1e-6
FFN_RES_WEIGHT = 0.5
GM_WIDTH = 512
GM_GROUPS = 4
GM_CHUNK = 128
RET_HEADS = 4
RET_DK = 64
RET_DV = 128
RET_CHUNK = 128
NSA_HEADS = 8
NSA_KV = 2
NSA_REP = NSA_HEADS // NSA_KV
NSA_DH = 64
CMP_LEN = 32
CMP_STRIDE = 16
CMP_HIDDEN = 128
SLC_BLOCK = 64
N_SELECT = 16
WINDOW = 512
Q_BLOCK = 128
N_MIXERS = 3
MIX_WIDTH = 512
IN_SPLITS = (512, 512, 256, 256, 512, 512, 512, 768, 24)
D_IN = sum(IN_SPLITS)
D_IN_PAD = 4096
BIG = 1e9
NEG = -1e30
PICKED = -(2.0 ** 127)

OFF_GM_U, OFF_GM_V, OFF_RQ, OFF_RK, OFF_RV, OFF_RG, OFF_NQ, OFF_NKV, OFF_NG = (
    0, 512, 1024, 1280, 1536, 2048, 2560, 3072, 3840)

QCOLS = NSA_REP * Q_BLOCK
SEL_TILE = 512
SEL_TILE_BLOCKS = SEL_TILE // SLC_BLOCK
SUB = 128
SUB_BLOCKS = SUB // SLC_BLOCK
FAR_GROUP = 4
KAUG = 128
COL_SEL = NSA_DH
COL_POS = NSA_DH + SEL_TILE_BLOCKS
VAUG = 80
WIN_TILES = (WINDOW + Q_BLOCK) // SUB
FAR_SLOT = 2 * WIN_TILES
CMP_TILE = 128
CMP_TILE_TOKENS = CMP_TILE * CMP_STRIDE
CMP_TILE_SLC = CMP_TILE_TOKENS // SLC_BLOCK
CMP_VARIANT_STEP = 2
N_NSA_BRANCH = 3
KVW = NSA_KV * NSA_DH
PA_WIDTH = OFF_NQ
WN_CV = PA_WIDTH
WN_GL = WN_CV + 2 * KVW
WN_KS = WN_GL + NSA_KV * KAUG
WN_WIDTH = WN_KS + 2 * NSA_KV * KAUG
WT_ROWS = NSA_HEADS * NSA_DH + 2 * KVW
VMEM_LIMIT = 56 * 1024 * 1024
CAST_BLOCK_BYTES = 4 * 1024 * 1024


def _cparams(sem, vmem=VMEM_LIMIT):
    return pltpu.CompilerParams(dimension_semantics=sem, vmem_limit_bytes=vmem)


def _cast_kernel(w_ref, o_ref):
    o_ref[...] = w_ref[...].astype(o_ref.dtype)


def _to_bf16(w):
    w2 = w.reshape(-1, w.shape[-1])
    r, c = w2.shape
    tr = 8
    while r % (2 * tr) == 0 and 2 * tr * c * 4 <= CAST_BLOCK_BYTES:
        tr *= 2
    out = pl.pallas_call(
        _cast_kernel,
        out_shape=jax.ShapeDtypeStruct((r, c), BF16),
        grid=(r // tr,),
        in_specs=[pl.BlockSpec((tr, c), lambda i: (i, 0))],
        out_specs=pl.BlockSpec((tr, c), lambda i: (i, 0)),
        compiler_params=_cparams(("arbitrary",)),
        name="cast_bf16",
    )(w2)
    return out.reshape(w.shape)


def _rms(x, g):
    return x * lax.rsqrt(jnp.mean(x * x, axis=-1, keepdims=True) + NORM_EPS) * g


def _dot(a, b):
    return jnp.dot(a, b, preferred_element_type=F32)


def _ffn_kernel(x_ref, g_ref, w1_ref, w2_ref, fin_ref, o_ref, act_ref, *, final_norm, tf):
    x = x_ref[...]
    h = _rms(x, g_ref[...]).astype(BF16)
    for f in range(D_FF // tf):
        a = _dot(h, w1_ref[:, f * tf:(f + 1) * tf])
        b = _dot(h, w1_ref[:, D_FF + f * tf:D_FF + (f + 1) * tf])
        act_ref[:, f * tf:(f + 1) * tf] = (jax.nn.silu(a) * b).astype(BF16)
    y = x + FFN_RES_WEIGHT * _dot(act_ref[...], w2_ref[...])
    if final_norm:
        y = _rms(y, fin_ref[...])
    o_ref[...] = y


def _ffn(x, g, w1, w2, layer, fin, final_norm, tm=512, tf=256):
    s = x.shape[0]
    fixed = lambda i: (0, 0)
    resident = pl.Buffered(1)
    return pl.pallas_call(
        functools.partial(_ffn_kernel, final_norm=final_norm, tf=tf),
        out_shape=jax.ShapeDtypeStruct((s, D_MODEL), F32),
        grid=(s // tm,),
        in_specs=[
            pl.BlockSpec((tm, D_MODEL), lambda i: (i, 0)),
            pl.BlockSpec((1, D_MODEL), fixed),
            pl.BlockSpec((None, D_MODEL, 2 * D_FF), lambda i: (layer, 0, 0), pipeline_mode=resident),
            pl.BlockSpec((None, D_FF, D_MODEL), lambda i: (layer, 0, 0), pipeline_mode=resident),
            pl.BlockSpec((1, D_MODEL), fixed),
        ],
        out_specs=pl.BlockSpec((tm, D_MODEL), lambda i: (i, 0)),
        scratch_shapes=[pltpu.VMEM((tm, D_FF), BF16)],
        compiler_params=_cparams(("arbitrary",)),
        name="ffn",
    )(x, g, w1, w2, fin)


def _proj_kernel(x_ref, g_ref, wn_ref, wt_ref, kcs_ref, kcw_ref, vones_ref,
                 pa_ref, pcv_ref, pgl_ref, ksa_ref, kwa_ref, qt_ref, vsta_ref, vwta_ref):
    tm = x_ref.shape[0]
    h = _rms(x_ref[...], g_ref[...]).astype(BF16)
    step = 512
    for c in range(0, PA_WIDTH, step):
        pa_ref[:, c:c + step] = _dot(h, wn_ref[:, c:c + step])
    pcv_ref[...] = _dot(h, wn_ref[:, WN_CV:WN_CV + 2 * KVW])
    pgl_ref[...] = _dot(h, wn_ref[:, WN_GL:WN_GL + 2 * KAUG])
    kk = _dot(h, wn_ref[:, WN_KS:WN_KS + 4 * KAUG])
    for g in range(NSA_KV):
        ksa_ref[g] = (kk[:, g * KAUG:(g + 1) * KAUG] + kcs_ref[...]).astype(BF16)
        kwa_ref[g] = (kk[:, (NSA_KV + g) * KAUG:(NSA_KV + g + 1) * KAUG] + kcw_ref[...]).astype(BF16)
    pt = lax.dot_general(wt_ref[...], h, (((1,), (1,)), ((), ())), preferred_element_type=F32)
    nq = NSA_HEADS * NSA_DH
    for g in range(NSA_KV):
        for b in range(tm // Q_BLOCK):
            for r in range(NSA_REP):
                hd = g * NSA_REP + r
                qt_ref[g, b, :, r * Q_BLOCK:(r + 1) * Q_BLOCK] = (
                    pt[hd * NSA_DH:(hd + 1) * NSA_DH, b * Q_BLOCK:(b + 1) * Q_BLOCK].astype(BF16))
            for v_ref, first_row in ((vsta_ref, nq), (vwta_ref, nq + KVW)):
                rows = slice(first_row + g * NSA_DH, first_row + (g + 1) * NSA_DH)
                v_ref[g, b, 0:NSA_DH, :] = pt[rows, b * SUB:(b + 1) * SUB].astype(BF16)
                v_ref[g, b, NSA_DH:VAUG, :] = vones_ref[...]


def _proj(x, g, wn, wt, kcs, kcw, vones, tm=512):
    s = x.shape[0]
    fixed = lambda i: (0, 0)
    return pl.pallas_call(
        _proj_kernel,
        out_shape=[
            jax.ShapeDtypeStruct((s, PA_WIDTH), F32),
            jax.ShapeDtypeStruct((s, 2 * KVW), F32),
            jax.ShapeDtypeStruct((s, 2 * KAUG), F32),
            jax.ShapeDtypeStruct((NSA_KV, s, KAUG), BF16),
            jax.ShapeDtypeStruct((NSA_KV, s, KAUG), BF16),
            jax.ShapeDtypeStruct((NSA_KV, s // Q_BLOCK, NSA_DH, QCOLS), BF16),
            jax.ShapeDtypeStruct((NSA_KV, s // SUB, VAUG, SUB), BF16),
            jax.ShapeDtypeStruct((NSA_KV, s // SUB, VAUG, SUB), BF16),
        ],
        grid=(s // tm,),
        in_specs=[
            pl.BlockSpec((tm, D_MODEL), lambda i: (i, 0)),
            pl.BlockSpec((1, D_MODEL), fixed),
            pl.BlockSpec((D_MODEL, WN_WIDTH), fixed),
            pl.BlockSpec((WT_ROWS, D_MODEL), fixed),
            pl.BlockSpec((tm, KAUG), fixed),
            pl.BlockSpec((tm, KAUG), fixed),
            pl.BlockSpec((VAUG - NSA_DH, SUB), fixed),
        ],
        out_specs=[
            pl.BlockSpec((tm, PA_WIDTH), lambda i: (i, 0)),
            pl.BlockSpec((tm, 2 * KVW), lambda i: (i, 0)),
            pl.BlockSpec((tm, 2 * KAUG), lambda i: (i, 0)),
            pl.BlockSpec((NSA_KV, tm, KAUG), lambda i: (0, i, 0)),
            pl.BlockSpec((NSA_KV, tm, KAUG), lambda i: (0, i, 0)),
            pl.BlockSpec((NSA_KV, tm // Q_BLOCK, NSA_DH, QCOLS), lambda i: (0, i, 0, 0)),
            pl.BlockSpec((NSA_KV, tm // SUB, VAUG, SUB), lambda i: (0, i, 0, 0)),
            pl.BlockSpec((NSA_KV, tm // SUB, VAUG, SUB), lambda i: (0, i, 0, 0)),
        ],
        compiler_params=_cparams(("arbitrary",)),
        name="proj",
    )(x, g, wn, wt, kcs, kcw, vones)


def _proj_weights(w_in):
    w = w_in.astype(BF16)
    z = lambda n: jnp.zeros((D_MODEL, n), BF16)
    kv = lambda i, g: w[:, OFF_NKV + i * KVW + g * NSA_DH:OFF_NKV + i * KVW + (g + 1) * NSA_DH]
    ngl = N_NSA_BRANCH * NSA_REP
    cols = [w[:, 0:PA_WIDTH], w[:, OFF_NKV:OFF_NKV + 2 * KVW]]
    for g in range(NSA_KV):
        cols += [w[:, OFF_NG + g * ngl:OFF_NG + (g + 1) * ngl], z(KAUG - ngl)]
    for i in (2, 4):
        for g in range(NSA_KV):
            cols += [kv(i, g), z(KAUG - NSA_DH)]
    wn = jnp.concatenate(cols, axis=1)
    wt = jnp.concatenate([w[:, OFF_NQ:OFF_NQ + NSA_HEADS * NSA_DH], kv(3, 0), kv(3, 1), kv(5, 0), kv(5, 1)], axis=1).T
    return wn, wt


def _proj_constants(tm=512):
    pos = jnp.arange(tm)
    zero = jnp.zeros((tm, NSA_DH), F32)

    def pos_cols(p):
        return jnp.stack([p // 16 * 16, p % 16], axis=1).astype(F32)

    onehot = (pos[:, None] // SLC_BLOCK % SEL_TILE_BLOCKS == jnp.arange(SEL_TILE_BLOCKS)[None, :]).astype(F32)
    tail = jnp.zeros((tm, KAUG - COL_POS - 2), F32)
    kcs = jnp.concatenate([zero, onehot, pos_cols(pos % SUB), tail], axis=1)
    kcw = jnp.concatenate([zero, jnp.zeros_like(onehot), pos_cols(pos % SUB), tail], axis=1)
    r = jnp.arange(VAUG - NSA_DH)[:, None]
    vones = jnp.broadcast_to(jnp.where(r == 0, 1.0, 0.0), (VAUG - NSA_DH, SUB)).astype(BF16)
    return kcs, kcw, vones


def _gmlp_kernel(u_ref, v_ref, lg_ref, lb_ref, ws_ref, bst_ref, o_ref, *, chunks):
    cg = GM_WIDTH // GM_GROUPS
    u = jax.nn.gelu(u_ref[...])
    v = jax.nn.gelu(v_ref[...])
    mu = jnp.mean(v, axis=-1, keepdims=True)
    vc = v - mu
    var = jnp.mean(vc * vc, axis=-1, keepdims=True)
    vn = (vc * lax.rsqrt(var + NORM_EPS) * lg_ref[...] + lb_ref[...]).astype(BF16)
    row = lax.broadcasted_iota(jnp.int32, (GM_CHUNK, GM_CHUNK), 0)
    col = lax.broadcasted_iota(jnp.int32, (GM_CHUNK, GM_CHUNK), 1)
    causal = row >= col
    bst = bst_ref[...]
    for gi in range(GM_GROUPS):
        w = jnp.where(causal, ws_ref[gi], 0.0).astype(BF16)
        bias = bst[:, gi:gi + 1]
        for c in range(chunks):
            rs = slice(c * GM_CHUNK, (c + 1) * GM_CHUNK)
            cs = slice(gi * cg, (gi + 1) * cg)
            sv = _dot(w, vn[rs, cs]) + bias
            o_ref[rs, cs] = (u[rs, cs] * sv).astype(o_ref.dtype)


def _gmlp(proj, ln_g, ln_b, ws, bst, tm=512):
    s = proj.shape[0]
    return pl.pallas_call(
        functools.partial(_gmlp_kernel, chunks=tm // GM_CHUNK),
        out_shape=jax.ShapeDtypeStruct((s, GM_WIDTH), BF16),
        grid=(s // tm,),
        in_specs=[
            pl.BlockSpec((tm, GM_WIDTH), lambda i: (i, OFF_GM_U // GM_WIDTH)),
            pl.BlockSpec((tm, GM_WIDTH), lambda i: (i, OFF_GM_V // GM_WIDTH)),
            pl.BlockSpec((1, GM_WIDTH), lambda i: (0, 0)),
            pl.BlockSpec((1, GM_WIDTH), lambda i: (0, 0)),
            pl.BlockSpec((GM_GROUPS, GM_CHUNK, GM_CHUNK), lambda i: (0, 0, 0)),
            pl.BlockSpec((GM_CHUNK, GM_GROUPS), lambda i: (0, 0)),
        ],
        out_specs=pl.BlockSpec((tm, GM_WIDTH), lambda i: (i, 0)),
        compiler_params=_cparams(("arbitrary",)),
        name="gmlp",
    )(proj, proj, ln_g, ln_b, ws, bst)


def _ret_kernel(q_ref, k_ref, v_ref, g_ref, gg_ref, gb_ref, o_ref, st_ref, *, chunks):
    @pl.when(pl.program_id(0) == 0)
    def _():
        st_ref[...] = jnp.zeros_like(st_ref)

    c = RET_CHUNK
    ri = lax.broadcasted_iota(jnp.int32, (c, c), 0)
    ci = lax.broadcasted_iota(jnp.int32, (c, c), 1)
    diff = (ri - ci).astype(F32)
    pos_k = lax.broadcasted_iota(jnp.int32, (c, RET_DK), 0).astype(F32)
    pos_v = lax.broadcasted_iota(jnp.int32, (c, RET_DV), 0).astype(F32)
    for h in range(RET_HEADS):
        log_gamma = math.log(1.0 - 2.0 ** (-5.0 - h))
        intra_decay = jnp.where(diff >= 0, jnp.exp(log_gamma * jnp.maximum(diff, 0.0)), 0.0)
        k_decay = jnp.exp(log_gamma * (c - 1.0 - pos_k))
        q_decay = jnp.exp(log_gamma * (pos_v + 1.0))
        chunk_decay = math.exp(log_gamma * c)
        gam = gg_ref[:, h * RET_DV:(h + 1) * RET_DV]
        bet = gb_ref[:, h * RET_DV:(h + 1) * RET_DV]
        for n in range(chunks):
            rs = slice(n * c, (n + 1) * c)
            q = q_ref[rs, h * RET_DK:(h + 1) * RET_DK].astype(BF16)
            kf = k_ref[rs, h * RET_DK:(h + 1) * RET_DK] * (RET_DK ** -0.5)
            k = kf.astype(BF16)
            v = v_ref[rs, h * RET_DV:(h + 1) * RET_DV].astype(BF16)
            scores = lax.dot_general(q, k, (((1,), (1,)), ((), ())), preferred_element_type=F32) * intra_decay
            intra = _dot(scores.astype(BF16), v)
            state = st_ref[h]
            cross = _dot(q, state.astype(BF16)) * q_decay
            kd = (kf * k_decay).astype(BF16)
            kv = lax.dot_general(kd, v, (((0,), (0,)), ((), ())), preferred_element_type=F32)
            st_ref[h] = state * chunk_decay + kv
            y = intra + cross
            mu = jnp.mean(y, axis=-1, keepdims=True)
            yc = y - mu
            var = jnp.mean(yc * yc, axis=-1, keepdims=True)
            yn = yc * lax.rsqrt(var + NORM_EPS) * gam + bet
            gate = g_ref[rs, h * RET_DV:(h + 1) * RET_DV]
            o_ref[rs, h * RET_DV:(h + 1) * RET_DV] = (jax.nn.silu(gate) * yn).astype(o_ref.dtype)


def _retention(proj, gn_g, gn_b, tm=512):
    s = proj.shape[0]
    hk = RET_HEADS * RET_DK
    hv = RET_HEADS * RET_DV
    return pl.pallas_call(
        functools.partial(_ret_kernel, chunks=tm // RET_CHUNK),
        out_shape=jax.ShapeDtypeStruct((s, hv), BF16),
        grid=(s // tm,),
        in_specs=[
            pl.BlockSpec((tm, hk), lambda i: (i, OFF_RQ // hk)),
            pl.BlockSpec((tm, hk), lambda i: (i, OFF_RK // hk)),
            pl.BlockSpec((tm, hv), lambda i: (i, OFF_RV // hv)),
            pl.BlockSpec((tm, hv), lambda i: (i, OFF_RG // hv)),
            pl.BlockSpec((1, hv), lambda i: (0, 0)),
            pl.BlockSpec((1, hv), lambda i: (0, 0)),
        ],
        out_specs=pl.BlockSpec((tm, hv), lambda i: (i, 0)),
        scratch_shapes=[pltpu.VMEM((RET_HEADS, RET_DK, RET_DV), F32)],
        compiler_params=_cparams(("arbitrary",)),
        name="retention",
    )(proj, proj, proj, proj, gn_g, gn_b)


def _compress_kernel(t_ref, pos_ref, w1_ref, w2_ref, o_ref, xlo_ref, xhi_ref, *, values):
    n = t_ref.shape[0] // CMP_STRIDE
    for i in range(CMP_STRIDE):
        x = t_ref[pl.ds(i, n, stride=CMP_STRIDE), :]
        xlo_ref[:, i * KVW:(i + 1) * KVW] = (x + pos_ref[0:1, i * KVW:(i + 1) * KVW]).astype(BF16)
        xhi_ref[:, i * KVW:(i + 1) * KVW] = (x + pos_ref[1:2, i * KVW:(i + 1) * KVW]).astype(BF16)
    lo = _dot(xlo_ref[...], w1_ref[0])
    hi = _dot(xhi_ref[...], w1_ref[1])
    hi_next = jnp.concatenate([hi[1:], hi[:1]], axis=0)
    rowi = lax.broadcasted_iota(jnp.int32, lo.shape, 0)
    hid = jnp.where(rowi < n - 1, jax.nn.gelu(lo + hi_next), 0.0).astype(BF16)
    for g in range(NSA_KV):
        if values:
            vt = lax.dot_general(w2_ref[g], hid, (((1,), (1,)), ((), ())), preferred_element_type=F32)
            r = lax.broadcasted_iota(jnp.int32, vt.shape, 0)
            vt = jnp.where(r == NSA_DH, 1.0, vt).astype(BF16)
            for c in range(n // CMP_TILE):
                o_ref[g, c] = vt[:, c * CMP_TILE:(c + 1) * CMP_TILE]
        else:
            k = _dot(hid, w2_ref[g])
            col = lax.broadcasted_iota(jnp.int32, k.shape, 1)
            blk = (lax.broadcasted_iota(jnp.int32, k.shape, 0) & (CMP_TILE - 1)) * CMP_STRIDE
            k = jnp.where(col == NSA_DH, blk.astype(F32), k).astype(BF16)
            for c in range(n // CMP_TILE):
                o_ref[g, c] = k[c * CMP_TILE:(c + 1) * CMP_TILE, :]


def _compress(pcv, posrows, w1, w2, values):
    s = pcv.shape[0]
    n = s // CMP_STRIDE
    nct = n // CMP_TILE
    out_tile = (VAUG, CMP_TILE) if values else (CMP_TILE, KAUG)
    width = CMP_STRIDE * KVW
    return pl.pallas_call(
        functools.partial(_compress_kernel, values=values),
        out_shape=jax.ShapeDtypeStruct((NSA_KV, nct) + out_tile, BF16),
        grid=(1,),
        in_specs=[
            pl.BlockSpec((s, KVW), lambda i: (0, 1 if values else 0)),
            pl.BlockSpec((2, width), lambda i: (0, 0)),
            pl.BlockSpec((2, width, NSA_KV * CMP_HIDDEN), lambda i: (0, 0, 0)),
            pl.BlockSpec((NSA_KV,) + w2.shape[1:], lambda i: (0, 0, 0)),
        ],
        out_specs=pl.BlockSpec((NSA_KV, nct) + out_tile, lambda i: (0, 0, 0, 0)),
        scratch_shapes=[pltpu.VMEM((n, width), BF16), pltpu.VMEM((n, width), BF16)],
        compiler_params=_cparams(("arbitrary",)),
        name="nsa_compress_v" if values else "nsa_compress_k",
    )(pcv, posrows, w1, w2)


def _compress_weights(pos, w1, w2, values):
    posrows = jnp.tile(pos.reshape(2, CMP_STRIDE, 1, NSA_DH), (1, 1, NSA_KV, 1)).reshape(2, CMP_STRIDE * KVW)
    w = w1.astype(BF16).reshape(2, CMP_STRIDE, NSA_DH, CMP_HIDDEN)
    eye = jnp.eye(NSA_KV, dtype=BF16)
    w1b = jnp.einsum('hidc,gk->higdkc', w, eye).reshape(2, CMP_STRIDE * KVW, NSA_KV * CMP_HIDDEN)
    w2g = jnp.einsum('cd,gk->gkcd', w2.astype(BF16), eye).reshape(NSA_KV, NSA_KV * CMP_HIDDEN, NSA_DH)
    if values:
        w2g = jnp.pad(w2g.transpose(0, 2, 1), ((0, 0), (0, VAUG - NSA_DH), (0, 0)))
    else:
        w2g = jnp.pad(w2g, ((0, 0), (0, 0), (0, KAUG - NSA_DH)))
    return posrows, w1b, w2g


def _nsa_cmp_kernel(qt_ref, kca_ref, vcta_ref, ovl_ref, slope_ref, oc_ref, selb_ref, any_ref, dmask_ref, jf_ref):
    qb = pl.program_id(0)
    nct = kca_ref.shape[1]
    nslc = selb_ref.shape[2]
    nsub = nslc // SUB_BLOCKS

    @pl.when(qb == 0)
    def _():
        cl = lax.broadcasted_iota(jnp.int32, (CMP_TILE, QCOLS), 0)
        q = lax.broadcasted_iota(jnp.int32, (CMP_TILE, QCOLS), 1) & (Q_BLOCK - 1)
        dmask_ref[...] = (cl * CMP_STRIDE - q).astype(F32)
        jf_ref[...] = lax.broadcasted_iota(jnp.int32, jf_ref.shape, 0).astype(F32)

    t0 = qb * Q_BLOCK
    r16 = lax.broadcasted_iota(jnp.int32, (16, QCOLS), 0)
    td = t0 // CMP_TILE_TOKENS

    def colmax(s):
        return jnp.max(s, axis=0, keepdims=True)

    def variant(nt, g):
        slope = slope_ref[g]
        qa = jnp.concatenate([qt_ref[g, 0] * (NSA_DH ** -0.5), jnp.where(r16 == 0, slope, 0.0).astype(BF16),
                              jnp.zeros((KAUG - NSA_DH - 16, QCOLS), BF16)], axis=0)
        nr = nt * CMP_TILE_SLC
        offs = [slope * (t0 - ti * CMP_TILE_TOKENS).astype(F32) for ti in range(nt)]
        zs, m = [], None
        for ti in range(nt):
            s = _dot(kca_ref[g, ti], qa)
            if ti >= nt - CMP_VARIANT_STEP - 1:
                lim = (t0 - ti * CMP_TILE_TOKENS - (CMP_LEN - 1)).astype(F32)
                s = jnp.where(dmask_ref[...] <= lim, s, NEG)
            zs.append(s)
            cand = colmax(s) - offs[ti]
            m = cand if m is None else jnp.maximum(m, cand)
        acc, pieces = None, []
        for ti in range(nt):
            p = jnp.exp(zs[ti] - (m + offs[ti])).astype(BF16)
            t = _dot(vcta_ref[g, ti], p)
            acc = t if acc is None else acc + t
            pieces.append(_dot(ovl_ref[...], p))
        inv = jnp.where(m > 0.5 * NEG, 1.0 / acc[NSA_DH:NSA_DH + 1], 0.0)
        oc_ref[g, 0] = acc[0:NSA_DH] * inv
        blocks = []
        for b in range(nt):
            part = pieces[b][0:CMP_TILE_SLC]
            if b >= 1:
                part = part + pieces[b - 1][CMP_TILE_SLC:2 * CMP_TILE_SLC]
            blocks.append(part)
        imp4 = jnp.concatenate(blocks, axis=0) * inv
        imp = imp4[:, 0:Q_BLOCK]
        for r in range(1, NSA_REP):
            imp = imp + imp4[:, r * Q_BLOCK:(r + 1) * Q_BLOCK]
        return imp

    def both_groups(nt):
        nr = nt * CMP_TILE_SLC
        width = NSA_KV * Q_BLOCK
        imp = jnp.concatenate([variant(nt, g) for g in range(NSA_KV)], axis=1)
        j = lax.broadcasted_iota(jnp.int32, (nr, width), 0)
        qq = lax.broadcasted_iota(jnp.int32, (nr, width), 1) & (Q_BLOCK - 1)
        cur = 2 * qb + jnp.where(qq >= SLC_BLOCK, 1, 0)
        forced = (j == 0) | (j == cur) | (j == cur - 1)
        imp = jnp.where(j > cur, -BIG, imp)
        imp = jnp.where(forced, PICKED, imp)
        jf = jf_ref[0:nr, :]
        for _ in range(max(min(N_SELECT, nslc) - 3, 0)):
            best = jnp.max(imp, axis=0, keepdims=True)
            first = jnp.min(jnp.where(imp == best, jf, float(nslc)), axis=0, keepdims=True)
            imp = jnp.where(jf == first, PICKED, imp)
        sel = imp == PICKED
        selb = jnp.where(sel, 0.0, NEG)
        ti = lax.broadcasted_iota(jnp.int32, (nsub, nr), 0) * SUB_BLOCKS
        tj = lax.broadcasted_iota(jnp.int32, (nsub, nr), 1)
        member = jnp.where((tj >= ti) & (tj < ti + SUB_BLOCKS), 1.0, 0.0).astype(BF16)
        count = _dot(member, jnp.where(sel, 1.0, 0.0).astype(BF16)).astype(BF16)
        for g in range(NSA_KV):
            lanes = slice(g * Q_BLOCK, (g + 1) * Q_BLOCK)
            selb_ref[g, 0, 0:nr, :] = selb[:, lanes]
            if nr < nslc:
                selb_ref[g, 0, nr:nslc, :] = jnp.full((nslc - nr, Q_BLOCK), NEG, F32)
            any_ref[g, 0] = lax.dot_general(jnp.ones((8, Q_BLOCK), BF16), count[:, lanes],
                                            (((1,), (1,)), ((), ())), preferred_element_type=F32)

    for k in range(-(-nct // CMP_VARIANT_STEP)):
        pl.when(td // CMP_VARIANT_STEP == k)(functools.partial(both_groups, min((k + 1) * CMP_VARIANT_STEP, nct)))


def _nsa_cmp(qt, kca, vcta, ovl, slopes, nslc):
    kv, nqb = qt.shape[0], qt.shape[1]
    nct = kca.shape[1]
    nsub = nslc // SUB_BLOCKS
    return pl.pallas_call(
        _nsa_cmp_kernel,
        out_shape=[
            jax.ShapeDtypeStruct((kv, nqb, NSA_DH, QCOLS), F32),
            jax.ShapeDtypeStruct((kv, nqb, nslc, Q_BLOCK), F32),
            jax.ShapeDtypeStruct((kv, nqb, 8, nsub), F32),
        ],
        grid=(nqb,),
        in_specs=[
            pl.BlockSpec((kv, 1, NSA_DH, QCOLS), lambda b: (0, b, 0, 0)),
            pl.BlockSpec((kv, nct, CMP_TILE, KAUG), lambda b: (0, 0, 0, 0)),
            pl.BlockSpec((kv, nct, VAUG, CMP_TILE), lambda b: (0, 0, 0, 0)),
            pl.BlockSpec((2 * CMP_TILE_SLC, CMP_TILE), lambda b: (0, 0)),
            pl.BlockSpec((kv, 1, QCOLS), lambda b: (0, 0, 0)),
        ],
        out_specs=[
            pl.BlockSpec((kv, 1, NSA_DH, QCOLS), lambda b: (0, b, 0, 0)),
            pl.BlockSpec((kv, 1, nslc, Q_BLOCK), lambda b: (0, b, 0, 0)),
            pl.BlockSpec((kv, 1, 8, nsub), lambda b: (0, b, 0, 0)),
        ],
        scratch_shapes=[pltpu.VMEM((CMP_TILE, QCOLS), F32), pltpu.VMEM((nslc, kv * Q_BLOCK), F32)],
        compiler_params=_cparams(("arbitrary",)),
        name="nsa_cmp_topk",
    )(qt, kca, vcta, ovl, slopes)


def _nsa_attn_kernel(cnt_ref, lst_ref, qt_ref, ksa_ref, vsta_ref, kwa_ref, vwta_ref, selb_ref, oc_ref, gl_ref,
                     slope_ref, o_ref, addlo_ref, addhi_ref, z_ref, m_ref, acc_ref, car_ref):
    g = pl.program_id(0)
    qb = pl.program_id(1)
    nqb = pl.num_programs(1)
    nsub = ksa_ref.shape[1]
    slope = slope_ref[0]

    @pl.when(qb == 0)
    def _():
        cl = lax.broadcasted_iota(jnp.int32, (SUB, QCOLS), 0)
        q = lax.broadcasted_iota(jnp.int32, (SUB, QCOLS), 1) & (Q_BLOCK - 1)
        addlo_ref[...] = jnp.where(cl <= q, 0.0, NEG)
        addhi_ref[...] = jnp.where(cl > q, 0.0, NEG)

    t0 = qb * Q_BLOCK
    q = qt_ref[0, 0] * (NSA_DH ** -0.5)
    r8 =lax.broadcasted_iota(jnp.int32, (SEL_TILE_BLOCKS, QCOLS), 0)
    slope_rows = jnp.where(r8 < 2, slope, 0.0)
    zero_rows = jnp.zeros((KAUG - COL_SEL - 16, QCOLS), BF16)
    q_win = jnp.concatenate([q, jnp.concatenate([jnp.zeros_like(slope_rows), slope_rows], axis=0).astype(BF16),
                             zero_rows], axis=0)

    def q_sel(a):
        grp = a // (SEL_TILE_BLOCKS // SUB_BLOCKS)
        sb = selb_ref[0, 0, pl.ds(pl.multiple_of(grp * SEL_TILE_BLOCKS, SEL_TILE_BLOCKS), SEL_TILE_BLOCKS), :]
        sb = jnp.concatenate([sb] * NSA_REP, axis=1)
        return jnp.concatenate([q, jnp.concatenate([sb, slope_rows], axis=0).astype(BF16), zero_rows], axis=0)

    def colmax(s):
        return jnp.max(s, axis=0, keepdims=True)

    def weigh(v_ref, subs, slots, shifts):
        pv = None
        for a, slot, shift in zip(subs, slots, shifts):
            p = jnp.exp(z_ref[slot] - shift).astype(BF16)
            t = _dot(v_ref[0, a], p)
            pv = t if pv is None else pv + t
        return pv

    first = qb - WINDOW // Q_BLOCK
    near = [jnp.maximum(first + i, 0) for i in range(WIN_TILES)]
    gone = [jnp.where(first + i >= 0, 0.0, -NEG) for i in range(WIN_TILES)]
    offs_s, offs_w, mg, mw = [], [], None, None
    for i in range(WIN_TILES):
        s = _dot(ksa_ref[0, near[i]], q_sel(near[i]))
        w = _dot(kwa_ref[0, near[i]], q_win)
        if i == 0:
            w = w + addhi_ref[...]
        if i == WIN_TILES - 1:
            s = s + addlo_ref[...]
            w = w + addlo_ref[...]
        z_ref[i] = s
        z_ref[WIN_TILES + i] = w
        off_s = slope * float((WIN_TILES - 1 - i) * SUB) + gone[i]
        off_w = gone[i] - slope * float(i * SUB)
        cs, cw = colmax(s) - off_s, colmax(w) - off_w
        mg = cs if mg is None else jnp.maximum(mg, cs)
        mw = cw if mw is None else jnp.maximum(mw, cw)
        offs_s.append(off_s)
        offs_w.append(off_w)
    m_ref[...] = mg
    acc_ref[...] = weigh(vsta_ref, near, range(WIN_TILES), [mg + o for o in offs_s])
    pvw = weigh(vwta_ref, near, range(WIN_TILES, 2 * WIN_TILES), [mw + o for o in offs_w])
    o_win = pvw[0:NSA_DH] / pvw[NSA_DH:NSA_DH + 1]

    base = (g * nqb + qb) * nsub
    count = cnt_ref[g * nqb + qb]

    def far_scores(gi, half):
        offs, mg = [], None
        for u in range(FAR_GROUP):
            e = gi * FAR_GROUP + u
            a = lst_ref[base + e]
            s = _dot(ksa_ref[0, a], q_sel(a))
            z_ref[FAR_SLOT + half * FAR_GROUP + u] = s
            off = slope * (t0 - a * SUB).astype(F32) + jnp.where(e < count, 0.0, -NEG)
            cand = colmax(s) - off
            mg = cand if mg is None else jnp.maximum(mg, cand)
            offs.append(off)
        return mg, tuple(offs)

    def keep(half, mg, offs):
        car_ref[half, 0:1, :] = mg
        for u in range(FAR_GROUP):
            car_ref[half, u + 1:u + 2, :] = offs[u]

    def far_stage(gi, half):
        mg = car_ref[half, 0:1, :]
        offs = [car_ref[half, u + 1:u + 2, :] for u in range(FAR_GROUP)]
        nxt = far_scores(gi + 1, 1 - half)
        subs = [lst_ref[base + gi * FAR_GROUP + u] for u in range(FAR_GROUP)]
        slots = [FAR_SLOT + half * FAR_GROUP + u for u in range(FAR_GROUP)]
        m_old = m_ref[...]
        m_new = jnp.maximum(m_old, mg)
        pv = weigh(vsta_ref, subs, slots, [m_new + off for off in offs])
        acc_ref[...] = jnp.exp(m_old - m_new) * acc_ref[...] + pv
        m_ref[...] = m_new
        keep(1 - half, *nxt)

    ngroups = (count + FAR_GROUP - 1) // FAR_GROUP

    def far_pair(pi, carry):
        far_stage(2 * pi, 0)

        @pl.when(2 * pi + 1 < ngroups)
        def _():
            far_stage(2 * pi + 1, 1)

        return carry

    keep(0, *far_scores(0, 0))
    lax.fori_loop(0, (ngroups + 1) // 2, far_pair, 0)
    acc = acc_ref[...]
    o_sel = acc[0:NSA_DH] / acc[NSA_DH:NSA_DH + 1]

    gates = jax.nn.sigmoid(gl_ref[0, 0])
    out = gates[0:1] * oc_ref[0, 0] + gates[1:2] * o_sel + gates[2:3] * o_win
    out_t = out.T
    heads = [out_t[r * Q_BLOCK:(r + 1) * Q_BLOCK, :] for r in range(NSA_REP)]
    o_ref[...] = jnp.concatenate(heads, axis=1).astype(o_ref.dtype)


def _nsa_attn(far_count, far_list, qt, ksa, vsta, kwa, vwta, selb, oc, gl, slopes):
    kv, nqb = qt.shape[0], qt.shape[1]
    nsub = ksa.shape[1]
    nslc = selb.shape[2]
    blk = lambda g, b, c, l: (g, b, 0, 0)
    grp = lambda g, b, c, l: (g, 0, 0, 0)
    grid_spec = pltpu.PrefetchScalarGridSpec(
        num_scalar_prefetch=2,
        grid=(kv, nqb),
        in_specs=[
            pl.BlockSpec((1, 1, NSA_DH, QCOLS), blk),
            pl.BlockSpec((1, nsub, SUB, KAUG), grp),
            pl.BlockSpec((1, nsub, VAUG, SUB), grp),
            pl.BlockSpec((1, nsub, SUB, KAUG), grp),
            pl.BlockSpec((1, nsub, VAUG, SUB), grp),
            pl.BlockSpec((1, 1, nslc, Q_BLOCK), blk),
            pl.BlockSpec((1, 1, NSA_DH, QCOLS), blk),
            pl.BlockSpec((1, 1, N_NSA_BRANCH, QCOLS), blk),
            pl.BlockSpec((1, 1, QCOLS), lambda g, b, c, l: (g, 0, 0)),
        ],
        out_specs=pl.BlockSpec((Q_BLOCK, NSA_REP * NSA_DH), lambda g, b, c, l: (b, g)),
        scratch_shapes=[
            pltpu.VMEM((SUB, QCOLS), F32),
            pltpu.VMEM((SUB, QCOLS), F32),
            pltpu.VMEM((FAR_SLOT + 2 * FAR_GROUP, SUB, QCOLS), F32),
            pltpu.VMEM((1, QCOLS), F32),
            pltpu.VMEM((VAUG, QCOLS), F32),
            pltpu.VMEM((2, 8, QCOLS), F32),
        ],
    )
    return pl.pallas_call(
        _nsa_attn_kernel,
        out_shape=jax.ShapeDtypeStruct((nsub * SUB, NSA_HEADS * NSA_DH), BF16),
        grid_spec=grid_spec,
        compiler_params=_cparams(("arbitrary", "arbitrary")),
        name="nsa_attn",
    )(far_count, far_list, qt, ksa, vsta, kwa, vwta, selb, oc, gl, slopes)


def _far_lists(picks, nqb):
    nsub = picks.shape[-1]
    a = jnp.arange(nsub)[None, None, :]
    far = a < (jnp.arange(nqb)[None, :, None] - WINDOW // Q_BLOCK)
    active = (picks > 0) & far
    count = jnp.sum(active, axis=-1).astype(jnp.int32)
    order = jnp.argsort(jnp.where(active, 0, 1), axis=-1, stable=True).astype(jnp.int32)
    lst = jnp.where(a < count[..., None], order, 0).reshape(-1)
    return count.reshape(-1), jnp.concatenate([lst, jnp.zeros((2 * FAR_GROUP,), jnp.int32)])


def _merge_kernel(x_ref, g_ref, ya_ref, yb_ref, yc_ref, wg_ref, bg_ref, wb_ref, wo_ref, o_ref):
    x = x_ref[...]
    h = _rms(x, g_ref[...]).astype(BF16)
    mix = None
    for mi, y_ref in enumerate((ya_ref, yb_ref, yc_ref)):
        cs = slice(mi * D_MODEL, (mi + 1) * D_MODEL)
        gate = jax.nn.sigmoid(_dot(h, wg_ref[:, cs]) + bg_ref[:, cs])
        term = gate * _dot(y_ref[...], wb_ref[mi])
        mix = term if mix is None else mix + term
    o_ref[...] = x + _dot(mix.astype(BF16), wo_ref[...])


def _merge(x, g, ya, yb, yc, wg, bg, wb, wo, layer, tm=512):
    s = x.shape[0]
    row = lambda i: (i, 0)
    fixed2 = lambda i: (0, 0)
    return pl.pallas_call(
        _merge_kernel,
        out_shape=jax.ShapeDtypeStruct((s, D_MODEL), F32),
        grid=(s // tm,),
        in_specs=[
            pl.BlockSpec((tm, D_MODEL), row),
            pl.BlockSpec((1, D_MODEL), fixed2),
            pl.BlockSpec((tm, MIX_WIDTH), row),
            pl.BlockSpec((tm, MIX_WIDTH), row),
            pl.BlockSpec((tm, MIX_WIDTH), row),
            pl.BlockSpec((None, D_MODEL, N_MIXERS * D_MODEL), lambda i: (layer, 0, 0), pipeline_mode=pl.Buffered(1)),
            pl.BlockSpec((1, N_MIXERS * D_MODEL), fixed2),
            pl.BlockSpec((None, N_MIXERS, MIX_WIDTH, D_MODEL), lambda i: (layer, 0, 0, 0), pipeline_mode=pl.Buffered(1)),
            pl.BlockSpec((None, D_MODEL, D_MODEL), lambda i: (layer, 0, 0), pipeline_mode=pl.Buffered(1)),
        ],
        out_specs=pl.BlockSpec((tm, D_MODEL), row),
        compiler_params=_cparams(("arbitrary",)),
        name="merge",
    )(x, g, ya, yb, yc, wg, bg, wb, wo)


def _alibi_slope_cols():
    h = jnp.arange(1, NSA_HEADS + 1, dtype=F32)
    slopes = (2.0 ** (-8.0 * h / NSA_HEADS)).reshape(NSA_KV, NSA_REP)
    return jnp.repeat(slopes, Q_BLOCK, axis=1).reshape(NSA_KV, 1, QCOLS)


def _overlap_local():
    cl = jnp.arange(CMP_TILE)[None, :]
    jl = jnp.arange(2 * CMP_TILE_SLC)[:, None]
    ov = (cl * CMP_STRIDE < (jl + 1) * SLC_BLOCK) & (cl * CMP_STRIDE + CMP_LEN > jl * SLC_BLOCK)
    return ov.astype(BF16)


def _nsa(pcv, pgl, ksa, kwa, qt, vsta, vwta, cmp_pos, cmp_w1, cmp_w2):
    s = pcv.shape[0]
    nqb = s // Q_BLOCK
    slopes = _alibi_slope_cols()
    kca = _compress(pcv, *_compress_weights(cmp_pos[0], cmp_w1[0], cmp_w2[0], False), False)
    vcta = _compress(pcv, *_compress_weights(cmp_pos[1], cmp_w1[1], cmp_w2[1], True), True)
    oc, selb, picks = _nsa_cmp(qt, kca, vcta, _overlap_local(), slopes, s // SLC_BLOCK)
    far_count, far_list = _far_lists(picks[:, :, 0, :], nqb)
    ngl = N_NSA_BRANCH * NSA_REP
    gl = pgl.reshape(nqb, Q_BLOCK, NSA_KV, KAUG)[..., :ngl].reshape(nqb, Q_BLOCK, NSA_KV, NSA_REP, N_NSA_BRANCH)
    gl = gl.transpose(2, 0, 4, 3, 1).reshape(NSA_KV, nqb, N_NSA_BRANCH, QCOLS)
    ksa = ksa.reshape(NSA_KV, s // SUB, SUB, KAUG)
    kwa = kwa.reshape(NSA_KV, s // SUB, SUB, KAUG)
    return _nsa_attn(far_count, far_list, qt, ksa, vsta, kwa, vwta, selb, oc, gl, slopes)


def kernel(x, ffn1_norm, ffn1_w1, ffn1_w2, mix_norm, w_in, gm_ln_g, gm_ln_b, gm_ws, gm_bs, ret_gn_g, ret_gn_b,
           cmp_pos, cmp_w1, cmp_w2, w_branch_out, w_merge_gate, b_merge_gate, w_o, ffn2_norm, ffn2_w1, ffn2_w2,
           final_norm):
    bsz, s, _ = x.shape
    depth = ffn1_w1.shape[0]
    row = lambda v: v.reshape(1, -1)
    fin = row(final_norm)
    f1w1, f1w2, f2w1, f2w2 = (_to_bf16(w) for w in (ffn1_w1, ffn1_w2, ffn2_w1, ffn2_w2))
    w_in_b, wg_b, wb_b, wo_b = (_to_bf16(w) for w in (w_in, w_merge_gate, w_branch_out, w_o))
    kcs, kcw, vones = _proj_constants()
    outs = []
    for b in range(bsz):
        xb = x[b]
        for l in range(depth):
            xb = _ffn(xb, row(ffn1_norm[l]), f1w1, f1w2, l, fin, False)
            wn, wt = _proj_weights(w_in_b[l])
            pa, pcv, pgl, ksa, kwa, qt, vsta, vwta = _proj(xb, row(mix_norm[l]), wn, wt, kcs, kcw, vones)
            y_a = _gmlp(pa, row(gm_ln_g[l]), row(gm_ln_b[l]), gm_ws[l], gm_bs[l].T)
            y_b = _retention(pa, row(ret_gn_g[l]), row(ret_gn_b[l]))
            y_c = _nsa(pcv, pgl, ksa, kwa, qt, vsta, vwta, cmp_pos[l], cmp_w1[l], cmp_w2[l])
            xb = _merge(xb, row(mix_norm[l]), y_a, y_b, y_c, wg_b, row(b_merge_gate[l]), wb_b, wo_b, l)
            xb = _ffn(xb, row(ffn2_norm[l]), f2w1, f2w2, l, fin, l == depth - 1)
        outs.append(xb)
    return jnp.stack(outs)
```

```python
import functools
import math

import jax
import jax.numpy as jnp
from jax import lax
from jax.experimental import pallas as pl
from jax.experimental.pallas import tpu as pltpu

F32 = jnp.float32
BF16 = jnp.bfloat16

D_MODEL = 1024
D_FF = 2816
NORM_EPS = 1e-6
FFN_RES_WEIGHT = 0.5
GM_WIDTH = 512
GM_GROUPS = 4
GM_CHUNK = 128
RET_HEADS = 4
RET_DK = 64
RET_DV = 128
RET_CHUNK = 128
NSA_HEADS = 8
NSA_KV = 2
NSA_REP = NSA_HEADS // NSA_KV
NSA_DH = 64
CMP_LEN = 32
CMP_STRIDE = 16
CMP_HIDDEN = 128
SLC_BLOCK = 64
N_SELECT = 16
WINDOW = 512
Q_BLOCK = 128
N_MIXERS = 3
MIX_WIDTH = 512
IN_SPLITS = (512, 512, 256, 256, 512, 512, 512, 768, 24)
D_IN = sum(IN_SPLITS)
D_IN_PAD = 4096
BIG = 1e9
NEG = -1e30
PICKED = -(2.0 ** 127)

OFF_GM_U, OFF_GM_V, OFF_RQ, OFF_RK, OFF_RV, OFF_RG, OFF_NQ, OFF_NKV, OFF_NG = (
    0, 512, 1024, 1280, 1536, 2048, 2560, 3072, 3840)

QCOLS = NSA_REP * Q_BLOCK
SEL_TILE = 512
SEL_TILE_BLOCKS = SEL_TILE // SLC_BLOCK
SUB = 128
SUB_BLOCKS = SUB // SLC_BLOCK
FAR_GROUP = 4
KAUG = 128
COL_SEL = NSA_DH
COL_POS = NSA_DH + SEL_TILE_BLOCKS
VAUG = 80
WIN_TILES = (WINDOW + Q_BLOCK) // SUB
FAR_SLOT = 2 * WIN_TILES
CMP_TILE = 128
CMP_TILE_TOKENS = CMP_TILE * CMP_STRIDE
CMP_TILE_SLC = CMP_TILE_TOKENS // SLC_BLOCK
CMP_VARIANT_STEP = 1
N_NSA_BRANCH = 3
KVW = NSA_KV * NSA_DH
PA_WIDTH = OFF_NQ
WN_CV = PA_WIDTH
WN_GL = WN_CV + 2 * KVW
WN_KS = WN_GL + NSA_KV * KAUG
WN_WIDTH = WN_KS + 2 * NSA_KV * KAUG
WT_ROWS = NSA_HEADS * NSA_DH + 2 * KVW
VMEM_LIMIT = 56 * 1024 * 1024
CAST_BLOCK_BYTES = 4 * 1024 * 1024


def _cparams(sem, vmem=VMEM_LIMIT):
    return pltpu.CompilerParams(dimension_semantics=sem, vmem_limit_bytes=vmem)


def _cast_kernel(w_ref, o_ref):
    o_ref[...] = w_ref[...].astype(o_ref.dtype)


def _to_bf16(w):
    w2 = w.reshape(-1, w.shape[-1])
    r, c = w2.shape
    tr = 8
    while r % (2 * tr) == 0 and 2 * tr * c * 4 <= CAST_BLOCK_BYTES:
        tr *= 2
    out = pl.pallas_call(
        _cast_kernel,
        out_shape=jax.ShapeDtypeStruct((r, c), BF16),
        grid=(r // tr,),
        in_specs=[pl.BlockSpec((tr, c), lambda i: (i, 0))],
        out_specs=pl.BlockSpec((tr, c), lambda i: (i, 0)),
        compiler_params=_cparams(("arbitrary",)),
        name="cast_bf16",
    )(w2)
    return out.reshape(w.shape)


def _rms(x, g):
    return x * lax.rsqrt(jnp.mean(x * x, axis=-1, keepdims=True) + NORM_EPS) * g


def _dot(a, b):
    return jnp.dot(a, b, preferred_element_type=F32)


def _ffn_kernel(x_ref, g_ref, w1_ref, w2_ref, fin_ref, o_ref, act_ref, *, final_norm, tf):
    x = x_ref[...]
    h = _rms(x, g_ref[...]).astype(BF16)
    for f in range(D_FF // tf):
        a = _dot(h, w1_ref[:, f * tf:(f + 1) * tf])
        b = _dot(h, w1_ref[:, D_FF + f * tf:D_FF + (f + 1) * tf])
        act_ref[:, f * tf:(f + 1) * tf] = (jax.nn.silu(a) * b).astype(BF16)
    y = x + FFN_RES_WEIGHT * _dot(act_ref[...], w2_ref[...])
    if final_norm:
        y = _rms(y, fin_ref[...])
    o_ref[...] = y


def _ffn(x, g, w1, w2, layer, fin, final_norm, tm=512, tf=256):
    s = x.shape[0]
    fixed = lambda i: (0, 0)
    resident = pl.Buffered(1)
    return pl.pallas_call(
        functools.partial(_ffn_kernel, final_norm=final_norm, tf=tf),
        out_shape=jax.ShapeDtypeStruct((s, D_MODEL), F32),
        grid=(s // tm,),
        in_specs=[
            pl.BlockSpec((tm, D_MODEL), lambda i: (i, 0)),
            pl.BlockSpec((1, D_MODEL), fixed),
            pl.BlockSpec((None, D_MODEL, 2 * D_FF), lambda i: (layer, 0, 0), pipeline_mode=resident),
            pl.BlockSpec((None, D_FF, D_MODEL), lambda i: (layer, 0, 0), pipeline_mode=resident),
            pl.BlockSpec((1, D_MODEL), fixed),
        ],
        out_specs=pl.BlockSpec((tm, D_MODEL), lambda i: (i, 0)),
        scratch_shapes=[pltpu.VMEM((tm, D_FF), BF16)],
        compiler_params=_cparams(("arbitrary",)),
        name="ffn",
    )(x, g, w1, w2, fin)


def _proj_kernel(x_ref, g_ref, wn_ref, wt_ref, kcs_ref, kcw_ref, vones_ref,
                 pa_ref, pcv_ref, pgl_ref, ksa_ref, kwa_ref, qt_ref, vsta_ref, vwta_ref):
    tm = x_ref.shape[0]
    h = _rms(x_ref[...], g_ref[...]).astype(BF16)
    step = 512
    for c in range(0, PA_WIDTH, step):
        pa_ref[:, c:c + step] = _dot(h, wn_ref[:, c:c + step])
    pcv_ref[...] = _dot(h, wn_ref[:, WN_CV:WN_CV + 2 * KVW])
    pgl_ref[...] = _dot(h, wn_ref[:, WN_GL:WN_GL + 2 * KAUG])
    kk = _dot(h, wn_ref[:, WN_KS:WN_KS + 4 * KAUG])
    for g in range(NSA_KV):
        ksa_ref[g] = (kk[:, g * KAUG:(g + 1) * KAUG] + kcs_ref[...]).astype(BF16)
        kwa_ref[g] = (kk[:, (NSA_KV + g) * KAUG:(NSA_KV + g + 1) * KAUG] + kcw_ref[...]).astype(BF16)
    pt = lax.dot_general(wt_ref[...], h, (((1,), (1,)), ((), ())), preferred_element_type=F32)
    nq = NSA_HEADS * NSA_DH
    for g in range(NSA_KV):
        for b in range(tm // Q_BLOCK):
            for r in range(NSA_REP):
                hd = g * NSA_REP + r
                qt_ref[g, b, :, r * Q_BLOCK:(r + 1) * Q_BLOCK] = (
                    pt[hd * NSA_DH:(hd + 1) * NSA_DH, b * Q_BLOCK:(b + 1) * Q_BLOCK].astype(BF16))
            for v_ref, first_row in ((vsta_ref, nq), (vwta_ref, nq + KVW)):
                rows = slice(first_row + g * NSA_DH, first_row + (g + 1) * NSA_DH)
                v_ref[g, b, 0:NSA_DH, :] = pt[rows, b * SUB:(b + 1) * SUB].astype(BF16)
                v_ref[g, b, NSA_DH:VAUG, :] = vones_ref[...]


def _proj(x, g, wn, wt, layer, kcs, kcw, vones, tm=512):
    s = x.shape[0]
    fixed = lambda i: (0, 0)
    resident = pl.Buffered(1)
    return pl.pallas_call(
        _proj_kernel,
        out_shape=[
            jax.ShapeDtypeStruct((s, PA_WIDTH), F32),
            jax.ShapeDtypeStruct((s, 2 * KVW), F32),
            jax.ShapeDtypeStruct((s, 2 * KAUG), F32),
            jax.ShapeDtypeStruct((NSA_KV, s, KAUG), BF16),
            jax.ShapeDtypeStruct((NSA_KV, s, KAUG), BF16),
            jax.ShapeDtypeStruct((NSA_KV, s // Q_BLOCK, NSA_DH, QCOLS), BF16),
            jax.ShapeDtypeStruct((NSA_KV, s // SUB, VAUG, SUB), BF16),
            jax.ShapeDtypeStruct((NSA_KV, s // SUB, VAUG, SUB), BF16),
        ],
        grid=(s // tm,),
        in_specs=[
            pl.BlockSpec((tm, D_MODEL), lambda i: (i, 0)),
            pl.BlockSpec((1, D_MODEL), fixed),
            pl.BlockSpec((None, D_MODEL, WN_WIDTH), lambda i: (layer, 0, 0), pipeline_mode=resident),
            pl.BlockSpec((None, WT_ROWS, D_MODEL), lambda i: (layer, 0, 0), pipeline_mode=resident),
            pl.BlockSpec((tm, KAUG), fixed),
            pl.BlockSpec((tm, KAUG), fixed),
            pl.BlockSpec((VAUG - NSA_DH, SUB), fixed),
        ],
        out_specs=[
            pl.BlockSpec((tm, PA_WIDTH), lambda i: (i, 0)),
            pl.BlockSpec((tm, 2 * KVW), lambda i: (i, 0)),
            pl.BlockSpec((tm, 2 * KAUG), lambda i: (i, 0)),
            pl.BlockSpec((NSA_KV, tm, KAUG), lambda i: (0, i, 0)),
            pl.BlockSpec((NSA_KV, tm, KAUG), lambda i: (0, i, 0)),
            pl.BlockSpec((NSA_KV, tm // Q_BLOCK, NSA_DH, QCOLS), lambda i: (0, i, 0, 0)),
            pl.BlockSpec((NSA_KV, tm // SUB, VAUG, SUB), lambda i: (0, i, 0, 0)),
            pl.BlockSpec((NSA_KV, tm // SUB, VAUG, SUB), lambda i: (0, i, 0, 0)),
        ],
        compiler_params=_cparams(("arbitrary",)),
        name="proj",
    )(x, g, wn, wt, kcs, kcw, vones)


def _proj_weights(w_in):
    w = w_in.astype(BF16)
    z = lambda n: jnp.zeros(w.shape[:-1] + (n,), BF16)
    kv = lambda i, g: w[..., OFF_NKV + i * KVW + g * NSA_DH:OFF_NKV + i * KVW + (g + 1) * NSA_DH]
    ngl = N_NSA_BRANCH * NSA_REP
    cols = [w[..., 0:PA_WIDTH], w[..., OFF_NKV:OFF_NKV + 2 * KVW]]
    for g in range(NSA_KV):
        cols += [w[..., OFF_NG + g * ngl:OFF_NG + (g + 1) * ngl], z(KAUG - ngl)]
    for i in (2, 4):
        for g in range(NSA_KV):
            cols += [kv(i, g), z(KAUG - NSA_DH)]
    wn = jnp.concatenate(cols, axis=-1)
    wt = jnp.concatenate([w[..., OFF_NQ:OFF_NQ + NSA_HEADS * NSA_DH], kv(3, 0), kv(3, 1), kv(5, 0), kv(5, 1)], axis=-1)
    return wn, jnp.swapaxes(wt, -1, -2)


def _proj_constants(tm=512):
    pos = jnp.arange(tm)
    zero = jnp.zeros((tm, NSA_DH), F32)

    def pos_cols(p):
        return jnp.stack([p // 16 * 16, p % 16], axis=1).astype(F32)

    onehot = (pos[:, None] // SLC_BLOCK % SEL_TILE_BLOCKS == jnp.arange(SEL_TILE_BLOCKS)[None, :]).astype(F32)
    tail = jnp.zeros((tm, KAUG - COL_POS - 2), F32)
    kcs = jnp.concatenate([zero, onehot, pos_cols(pos % SUB), tail], axis=1)
    kcw = jnp.concatenate([zero, jnp.zeros_like(onehot), pos_cols(pos % SUB), tail], axis=1)
    r = jnp.arange(VAUG - NSA_DH)[:, None]
    vones = jnp.broadcast_to(jnp.where(r == 0, 1.0, 0.0), (VAUG - NSA_DH, SUB)).astype(BF16)
    return kcs, kcw, vones


def _gmlp_kernel(u_ref, v_ref, lg_ref, lb_ref, ws_ref, bst_ref, o_ref, *, chunks):
    cg = GM_WIDTH // GM_GROUPS
    u = jax.nn.gelu(u_ref[...])
    v = jax.nn.gelu(v_ref[...])
    mu = jnp.mean(v, axis=-1, keepdims=True)
    vc = v - mu
    var = jnp.mean(vc * vc, axis=-1, keepdims=True)
    vn = (vc * lax.rsqrt(var + NORM_EPS) * lg_ref[...] + lb_ref[...]).astype(BF16)
    row = lax.broadcasted_iota(jnp.int32, (GM_CHUNK, GM_CHUNK), 0)
    col = lax.broadcasted_iota(jnp.int32, (GM_CHUNK, GM_CHUNK), 1)
    causal = row >= col
    bst = bst_ref[...]
    for gi in range(GM_GROUPS):
        w = jnp.where(causal, ws_ref[gi], 0.0).astype(BF16)
        bias = bst[:, gi:gi + 1]
        for c in range(chunks):
            rs = slice(c * GM_CHUNK, (c + 1) * GM_CHUNK)
            cs = slice(gi * cg, (gi + 1) * cg)
            sv = _dot(w, vn[rs, cs]) + bias
            o_ref[rs, cs] = (u[rs, cs] * sv).astype(o_ref.dtype)


def _gmlp(proj, ln_g, ln_b, ws, bst, tm=512):
    s = proj.shape[0]
    return pl.pallas_call(
        functools.partial(_gmlp_kernel, chunks=tm // GM_CHUNK),
        out_shape=jax.ShapeDtypeStruct((s, GM_WIDTH), BF16),
        grid=(s // tm,),
        in_specs=[
            pl.BlockSpec((tm, GM_WIDTH), lambda i: (i, OFF_GM_U // GM_WIDTH)),
            pl.BlockSpec((tm, GM_WIDTH), lambda i: (i, OFF_GM_V // GM_WIDTH)),
            pl.BlockSpec((1, GM_WIDTH), lambda i: (0, 0)),
            pl.BlockSpec((1, GM_WIDTH), lambda i: (0, 0)),
            pl.BlockSpec((GM_GROUPS, GM_CHUNK, GM_CHUNK), lambda i: (0, 0, 0)),
            pl.BlockSpec((GM_CHUNK, GM_GROUPS), lambda i: (0, 0)),
        ],
        out_specs=pl.BlockSpec((tm, GM_WIDTH), lambda i: (i, 0)),
        compiler_params=_cparams(("arbitrary",)),
        name="gmlp",
    )(proj, proj, ln_g, ln_b, ws, bst)


def _ret_kernel(q_ref, k_ref, v_ref, g_ref, gg_ref, gb_ref, o_ref, st_ref, *, chunks):
    @pl.when(pl.program_id(0) == 0)
    def _():
        st_ref[...] = jnp.zeros_like(st_ref)

    c = RET_CHUNK
    ri = lax.broadcasted_iota(jnp.int32, (c, c), 0)
    ci = lax.broadcasted_iota(jnp.int32, (c, c), 1)
    diff = (ri - ci).astype(F32)
    pos_k = lax.broadcasted_iota(jnp.int32, (c, RET_DK), 0).astype(F32)
    pos_v = lax.broadcasted_iota(jnp.int32, (c, RET_DV), 0).astype(F32)
    for h in range(RET_HEADS):
        log_gamma = math.log(1.0 - 2.0 ** (-5.0 - h))
        intra_decay = jnp.where(diff >= 0, jnp.exp(log_gamma * jnp.maximum(diff, 0.0)), 0.0)
        k_decay = jnp.exp(log_gamma * (c - 1.0 - pos_k))
        q_decay = jnp.exp(log_gamma * (pos_v + 1.0))
        chunk_decay = math.exp(log_gamma * c)
        gam = gg_ref[:, h * RET_DV:(h + 1) * RET_DV]
        bet = gb_ref[:, h * RET_DV:(h + 1) * RET_DV]
        for n in range(chunks):
            rs = slice(n * c, (n + 1) * c)
            q = q_ref[rs, h * RET_DK:(h + 1) * RET_DK].astype(BF16)
            kf = k_ref[rs, h * RET_DK:(h + 1) * RET_DK] * (RET_DK ** -0.5)
            k = kf.astype(BF16)
            v = v_ref[rs, h * RET_DV:(h + 1) * RET_DV].astype(BF16)
            scores = lax.dot_general(q, k, (((1,), (1,)), ((), ())), preferred_element_type=F32) * intra_decay
            intra = _dot(scores.astype(BF16), v)
            state = st_ref[h]
            cross = _dot(q, state.astype(BF16)) * q_decay
            kd = (kf * k_decay).astype(BF16)
            kv = lax.dot_general(kd, v, (((0,), (0,)), ((), ())), preferred_element_type=F32)
            st_ref[h] = state * chunk_decay + kv
            y = intra + cross
            mu = jnp.mean(y, axis=-1, keepdims=True)
            yc = y - mu
            var = jnp.mean(yc * yc, axis=-1, keepdims=True)
            yn = yc * lax.rsqrt(var + NORM_EPS) * gam + bet
            gate = g_ref[rs, h * RET_DV:(h + 1) * RET_DV]
            o_ref[rs, h * RET_DV:(h + 1) * RET_DV] = (jax.nn.silu(gate) * yn).astype(o_ref.dtype)


def _retention(proj, gn_g, gn_b, tm=512):
    s = proj.shape[0]
    hk = RET_HEADS * RET_DK
    hv = RET_HEADS * RET_DV
    return pl.pallas_call(
        functools.partial(_ret_kernel, chunks=tm // RET_CHUNK),
        out_shape=jax.ShapeDtypeStruct((s, hv), BF16),
        grid=(s // tm,),
        in_specs=[
            pl.BlockSpec((tm, hk), lambda i: (i, OFF_RQ // hk)),
            pl.BlockSpec((tm, hk), lambda i: (i, OFF_RK // hk)),
            pl.BlockSpec((tm, hv), lambda i: (i, OFF_RV // hv)),
            pl.BlockSpec((tm, hv), lambda i: (i, OFF_RG // hv)),
            pl.BlockSpec((1, hv), lambda i: (0, 0)),
            pl.BlockSpec((1, hv), lambda i: (0, 0)),
        ],
        out_specs=pl.BlockSpec((tm, hv), lambda i: (i, 0)),
        scratch_shapes=[pltpu.VMEM((RET_HEADS, RET_DK, RET_DV), F32)],
        compiler_params=_cparams(("arbitrary",)),
        name="retention",
    )(proj, proj, proj, proj, gn_g, gn_b)


def _compress_kernel(t_ref, pos_ref, w1_ref, w2_ref, o_ref, xlo_ref, xhi_ref, *, values):
    n = t_ref.shape[0] // CMP_STRIDE
    for i in range(CMP_STRIDE):
        x = t_ref[pl.ds(i, n, stride=CMP_STRIDE), :]
        xlo_ref[:, i * KVW:(i + 1) * KVW] = (x + pos_ref[0:1, i * KVW:(i + 1) * KVW]).astype(BF16)
        xhi_ref[:, i * KVW:(i + 1) * KVW] = (x + pos_ref[1:2, i * KVW:(i + 1) * KVW]).astype(BF16)
    lo = _dot(xlo_ref[...], w1_ref[0])
    hi = _dot(xhi_ref[...], w1_ref[1])
    hi_next = jnp.concatenate([hi[1:], hi[:1]], axis=0)
    rowi = lax.broadcasted_iota(jnp.int32, lo.shape, 0)
    hid = jnp.where(rowi < n - 1, jax.nn.gelu(lo + hi_next), 0.0).astype(BF16)
    for g in range(NSA_KV):
        if values:
            vt = lax.dot_general(w2_ref[g], hid, (((1,), (1,)), ((), ())), preferred_element_type=F32)
            r = lax.broadcasted_iota(jnp.int32, vt.shape, 0)
            vt = jnp.where(r == NSA_DH, 1.0, vt).astype(BF16)
            for c in range(n // CMP_TILE):
                o_ref[g, c] = vt[:, c * CMP_TILE:(c + 1) * CMP_TILE]
        else:
            k = _dot(hid, w2_ref[g])
            col = lax.broadcasted_iota(jnp.int32, k.shape, 1)
            blk = (lax.broadcasted_iota(jnp.int32, k.shape, 0) & (CMP_TILE - 1)) * CMP_STRIDE
            k = jnp.where(col == NSA_DH, blk.astype(F32), k).astype(BF16)
            for c in range(n // CMP_TILE):
                o_ref[g, c] = k[c * CMP_TILE:(c + 1) * CMP_TILE, :]


def _compress(pcv, posrows, w1, w2, values):
    s = pcv.shape[0]
    n = s // CMP_STRIDE
    nct = n // CMP_TILE
    out_tile = (VAUG, CMP_TILE) if values else (CMP_TILE, KAUG)
    width = CMP_STRIDE * KVW
    return pl.pallas_call(
        functools.partial(_compress_kernel, values=values),
        out_shape=jax.ShapeDtypeStruct((NSA_KV, nct) + out_tile, BF16),
        grid=(1,),
        in_specs=[
            pl.BlockSpec((s, KVW), lambda i: (0, 1 if values else 0)),
            pl.BlockSpec((2, width), lambda i: (0, 0)),
            pl.BlockSpec((2, width, NSA_KV * CMP_HIDDEN), lambda i: (0, 0, 0)),
            pl.BlockSpec((NSA_KV,) + w2.shape[1:], lambda i: (0, 0, 0)),
        ],
        out_specs=pl.BlockSpec((NSA_KV, nct) + out_tile, lambda i: (0, 0, 0, 0)),
        scratch_shapes=[pltpu.VMEM((n, width), BF16), pltpu.VMEM((n, width), BF16)],
        compiler_params=_cparams(("arbitrary",)),
        name="nsa_compress_v" if values else "nsa_compress_k",
    )(pcv, posrows, w1, w2)


def _compress_weights(pos, w1, w2, values):
    posrows = jnp.tile(pos.reshape(2, CMP_STRIDE, 1, NSA_DH), (1, 1, NSA_KV, 1)).reshape(2, CMP_STRIDE * KVW)
    w = w1.astype(BF16).reshape(2, CMP_STRIDE, NSA_DH, CMP_HIDDEN)
    eye = jnp.eye(NSA_KV, dtype=BF16)
    w1b = jnp.einsum('hidc,gk->higdkc', w, eye).reshape(2, CMP_STRIDE * KVW, NSA_KV * CMP_HIDDEN)
    w2g = jnp.einsum('cd,gk->gkcd', w2.astype(BF16), eye).reshape(NSA_KV, NSA_KV * CMP_HIDDEN, NSA_DH)
    if values:
        w2g = jnp.pad(w2g.transpose(0, 2, 1), ((0, 0), (0, VAUG - NSA_DH), (0, 0)))
    else:
        w2g = jnp.pad(w2g, ((0, 0), (0, 0), (0, KAUG - NSA_DH)))
    return posrows, w1b, w2g


def _nsa_cmp_kernel(qt_ref, kca_ref, vo_ref, slope_ref, oc_ref, selb_ref, any_ref, dmask_ref, jf_ref):
    qb = pl.program_id(0)
    nct = kca_ref.shape[1]
    nslc = selb_ref.shape[2]
    nsub = nslc // SUB_BLOCKS

    @pl.when(qb == 0)
    def _():
        cl = lax.broadcasted_iota(jnp.int32, (CMP_TILE, QCOLS), 0)
        q = lax.broadcasted_iota(jnp.int32, (CMP_TILE, QCOLS), 1) & (Q_BLOCK - 1)
        dmask_ref[...] = (cl * CMP_STRIDE - q).astype(F32)
        jf_ref[...] = lax.broadcasted_iota(jnp.int32, jf_ref.shape, 0).astype(F32)

    t0 = qb * Q_BLOCK
    r16 = lax.broadcasted_iota(jnp.int32, (16, QCOLS), 0)
    td = t0 // CMP_TILE_TOKENS

    def colmax(s):
        return jnp.max(s, axis=0, keepdims=True)

    def variant(nt, g):
        slope = slope_ref[g]
        qa = jnp.concatenate([qt_ref[g, 0] * (NSA_DH ** -0.5), jnp.where(r16 == 0, slope, 0.0).astype(BF16),
                              jnp.zeros((KAUG - NSA_DH - 16, QCOLS), BF16)], axis=0)
        nr = nt * CMP_TILE_SLC
        offs = [slope * (t0 - ti * CMP_TILE_TOKENS).astype(F32) for ti in range(nt)]
        zs, m = [], None
        for ti in range(nt):
            s = _dot(kca_ref[g, ti], qa)
            if ti >= nt - CMP_VARIANT_STEP - 1:
                lim = (t0 - ti * CMP_TILE_TOKENS - (CMP_LEN - 1)).astype(F32)
                s = jnp.where(dmask_ref[...] <= lim, s, NEG)
            zs.append(s)
            cand = colmax(s) - offs[ti]
            m = cand if m is None else jnp.maximum(m, cand)
        acc, pieces = None, []
        for ti in range(nt):
            p = jnp.exp(zs[ti] - (m + offs[ti])).astype(BF16)
            t = _dot(vo_ref[g, ti], p)
            acc = t[0:VAUG] if acc is None else acc + t[0:VAUG]
            pieces.append(t[VAUG:VAUG + 2 * CMP_TILE_SLC])
        inv = jnp.where(m > 0.5 * NEG, 1.0 / acc[NSA_DH:NSA_DH + 1], 0.0)
        oc_ref[g, 0] = acc[0:NSA_DH] * inv
        blocks = []
        for b in range(nt):
            part = pieces[b][0:CMP_TILE_SLC]
            if b >= 1:
                part = part + pieces[b - 1][CMP_TILE_SLC:2 * CMP_TILE_SLC]
            blocks.append(part)
        imp4 = jnp.concatenate(blocks, axis=0) * inv
        imp = imp4[:, 0:Q_BLOCK]
        for r in range(1, NSA_REP):
            imp = imp + imp4[:, r * Q_BLOCK:(r + 1) * Q_BLOCK]
        return imp

    def both_groups(nt):
        nr = nt * CMP_TILE_SLC
        width = NSA_KV * Q_BLOCK
        imp = jnp.concatenate([variant(nt, g) for g in range(NSA_KV)], axis=1)
        j = lax.broadcasted_iota(jnp.int32, (nr, width), 0)
        qq = lax.broadcasted_iota(jnp.int32, (nr, width), 1) & (Q_BLOCK - 1)
        cur = 2 * qb + jnp.where(qq >= SLC_BLOCK, 1, 0)
        forced = (j == 0) | (j == cur) | (j == cur - 1)
        imp = jnp.where(j > cur, -BIG, imp)
        imp = jnp.where(forced, PICKED, imp)
        jf = jf_ref[0:nr, :]
        for _ in range(max(min(N_SELECT, nslc) - 3, 0)):
            best = jnp.max(imp, axis=0, keepdims=True)
            first = jnp.min(jnp.where(imp == best, jf, float(nslc)), axis=0, keepdims=True)
            imp = jnp.where(jf == first, PICKED, imp)
        sel = imp == PICKED
        selb = jnp.where(sel, 0.0, NEG)
        ti = lax.broadcasted_iota(jnp.int32, (nsub, nr), 0) * SUB_BLOCKS
        tj = lax.broadcasted_iota(jnp.int32, (nsub, nr), 1)
        member = jnp.where((tj >= ti) & (tj < ti + SUB_BLOCKS), 1.0, 0.0).astype(BF16)
        count = _dot(member, jnp.where(sel, 1.0, 0.0).astype(BF16)).astype(BF16)
        for g in range(NSA_KV):
            lanes = slice(g * Q_BLOCK, (g + 1) * Q_BLOCK)
            selb_ref[g, 0, 0:nr, :] = selb[:, lanes]
            if nr < nslc:
                selb_ref[g, 0, nr:nslc, :] = jnp.full((nslc - nr, Q_BLOCK), NEG, F32)
            any_ref[g, 0] = lax.dot_general(jnp.ones((8, Q_BLOCK), BF16), count[:, lanes],
                                            (((1,), (1,)), ((), ())), preferred_element_type=F32)

    for k in range(-(-nct // CMP_VARIANT_STEP)):
        pl.when(td // CMP_VARIANT_STEP == k)(functools.partial(both_groups, min((k + 1) * CMP_VARIANT_STEP, nct)))


def _nsa_cmp(qt, kca, vcta, ovl, slopes, nslc):
    kv, nqb = qt.shape[0], qt.shape[1]
    nct = kca.shape[1]
    nsub = nslc // SUB_BLOCKS
    vo = jnp.concatenate([vcta, jnp.broadcast_to(ovl, (kv, nct) + ovl.shape)], axis=2)
    vo_rows = vo.shape[2]
    return pl.pallas_call(
        _nsa_cmp_kernel,
        out_shape=[
            jax.ShapeDtypeStruct((kv, nqb, NSA_DH, QCOLS), F32),
            jax.ShapeDtypeStruct((kv, nqb, nslc, Q_BLOCK), F32),
            jax.ShapeDtypeStruct((kv, nqb, 8, nsub), F32),
        ],
        grid=(nqb,),
        in_specs=[
            pl.BlockSpec((kv, 1, NSA_DH, QCOLS), lambda b: (0, b, 0, 0)),
            pl.BlockSpec((kv, nct, CMP_TILE, KAUG), lambda b: (0, 0, 0, 0)),
            pl.BlockSpec((kv, nct, vo_rows, CMP_TILE), lambda b: (0, 0, 0, 0)),
            pl.BlockSpec((kv, 1, QCOLS), lambda b: (0, 0, 0)),
        ],
        out_specs=[
            pl.BlockSpec((kv, 1, NSA_DH, QCOLS), lambda b: (0, b, 0, 0)),
            pl.BlockSpec((kv, 1, nslc, Q_BLOCK), lambda b: (0, b, 0, 0)),
            pl.BlockSpec((kv, 1, 8, nsub), lambda b: (0, b, 0, 0)),
        ],
        scratch_shapes=[pltpu.VMEM((CMP_TILE, QCOLS), F32), pltpu.VMEM((nslc, kv * Q_BLOCK), F32)],
        compiler_params=_cparams(("arbitrary",)),
        name="nsa_cmp_topk",
    )(qt, kca, vo, slopes)


def _nsa_attn_kernel(cnt_ref, lst_ref, qt_ref, ksa_ref, vsta_ref, kwa_ref, vwta_ref, selb_ref, oc_ref, gl_ref,
                     slope_ref, o_ref, addlo_ref, addhi_ref, z_ref, m_ref, acc_ref, car_ref):
    g = pl.program_id(0)
    qb = pl.program_id(1)
    nqb = pl.num_programs(1)
    nsub = ksa_ref.shape[1]
    slope = slope_ref[0]

    @pl.when(qb == 0)
    def _():
        cl = lax.broadcasted_iota(jnp.int32, (SUB, QCOLS), 0)
        q = lax.broadcasted_iota(jnp.int32, (SUB, QCOLS), 1) & (Q_BLOCK - 1)
        addlo_ref[...] = jnp.where(cl <= q, 0.0, NEG)
        addhi_ref[...] = jnp.where(cl > q, 0.0, NEG)

    t0 = qb * Q_BLOCK
    q = qt_ref[0, 0] * (NSA_DH ** -0.5)
    r8 =lax.broadcasted_iota(jnp.int32, (SEL_TILE_BLOCKS, QCOLS), 0)
    slope_rows = jnp.where(r8 < 2, slope, 0.0)
    zero_rows = jnp.zeros((KAUG - COL_SEL - 16, QCOLS), BF16)
    q_win = jnp.concatenate([q, jnp.concatenate([jnp.zeros_like(slope_rows), slope_rows], axis=0).astype(BF16),
                             zero_rows], axis=0)

    def q_sel(a):
        grp = a // (SEL_TILE_BLOCKS // SUB_BLOCKS)
        sb = selb_ref[0, 0, pl.ds(pl.multiple_of(grp * SEL_TILE_BLOCKS, SEL_TILE_BLOCKS), SEL_TILE_BLOCKS), :]
        sb = jnp.concatenate([sb] * NSA_REP, axis=1)
        return jnp.concatenate([q, jnp.concatenate([sb, slope_rows], axis=0).astype(BF16), zero_rows], axis=0)

    def colmax(s):
        return jnp.max(s, axis=0, keepdims=True)

    def weigh(v_ref, subs, slots, shifts):
        pv = None
        for a, slot, shift in zip(subs, slots, shifts):
            p = jnp.exp(z_ref[slot] - shift).astype(BF16)
            t = _dot(v_ref[0, a], p)
            pv = t if pv is None else pv + t
        return pv

    first = qb - WINDOW // Q_BLOCK
    near = [jnp.maximum(first + i, 0) for i in range(WIN_TILES)]
    gone = [jnp.where(first + i >= 0, 0.0, -NEG) for i in range(WIN_TILES)]
    offs_s, offs_w, mg, mw = [], [], None, None
    for i in range(WIN_TILES):
        s = _dot(ksa_ref[0, near[i]], q_sel(near[i]))
        w = _dot(kwa_ref[0, near[i]], q_win)
        if i == 0:
            w = w + addhi_ref[...]
        if i == WIN_TILES - 1:
            s = s + addlo_ref[...]
            w = w + addlo_ref[...]
        z_ref[i] = s
        z_ref[WIN_TILES + i] = w
        off_s = slope * float((WIN_TILES - 1 - i) * SUB) + gone[i]
        off_w = gone[i] - slope * float(i * SUB)
        cs, cw = colmax(s) - off_s, colmax(w) - off_w
        mg = cs if mg is None else jnp.maximum(mg, cs)
        mw = cw if mw is None else jnp.maximum(mw, cw)
        offs_s.append(off_s)
        offs_w.append(off_w)
    m_ref[...] = mg
    acc_ref[...] = weigh(vsta_ref, near, range(WIN_TILES), [mg + o for o in offs_s])
    pvw = weigh(vwta_ref, near, range(WIN_TILES, 2 * WIN_TILES), [mw + o for o in offs_w])
    o_win = pvw[0:NSA_DH] / pvw[NSA_DH:NSA_DH + 1]

    base = (g * nqb + qb) * nsub
    count = cnt_ref[g * nqb + qb]

    def far_scores(gi, half):
        offs, mg = [], None
        for u in range(FAR_GROUP):
            e = gi * FAR_GROUP + u
            a = lst_ref[base + e]
            s = _dot(ksa_ref[0, a], q_sel(a))
            z_ref[FAR_SLOT + half * FAR_GROUP + u] = s
            off = slope * (t0 - a * SUB).astype(F32) + jnp.where(e < count, 0.0, -NEG)
            cand = colmax(s) - off
            mg = cand if mg is None else jnp.maximum(mg, cand)
            offs.append(off)
        return mg, tuple(offs)

    def keep(half, mg, offs):
        car_ref[half, 0:1, :] = mg
        for u in range(FAR_GROUP):
            car_ref[half, u + 1:u + 2, :] = offs[u]

    def far_stage(gi, half):
        mg = car_ref[half, 0:1, :]
        offs = [car_ref[half, u + 1:u + 2, :] for u in range(FAR_GROUP)]
        nxt = far_scores(gi + 1, 1 - half)
        subs = [lst_ref[base + gi * FAR_GROUP + u] for u in range(FAR_GROUP)]
        slots = [FAR_SLOT + half * FAR_GROUP + u for u in range(FAR_GROUP)]
        m_old = m_ref[...]
        m_new = jnp.maximum(m_old, mg)
        pv = weigh(vsta_ref, subs, slots, [m_new + off for off in offs])
        acc_ref[...] = jnp.exp(m_old - m_new) * acc_ref[...] + pv
        m_ref[...] = m_new
        keep(1 - half, *nxt)

    ngroups = (count + FAR_GROUP - 1) // FAR_GROUP

    def far_pair(pi, carry):
        far_stage(2 * pi, 0)

        @pl.when(2 * pi + 1 < ngroups)
        def _():
            far_stage(2 * pi + 1, 1)

        return carry

    keep(0, *far_scores(0, 0))
    lax.fori_loop(0, (ngroups + 1) // 2, far_pair, 0)
    acc = acc_ref[...]
    o_sel = acc[0:NSA_DH] / acc[NSA_DH:NSA_DH + 1]

    gl_t = gl_ref[...].T
    gates = [jax.nn.sigmoid(jnp.concatenate(
        [gl_t[r * N_NSA_BRANCH + br:r * N_NSA_BRANCH + br + 1, :] for r in range(NSA_REP)], axis=1))
        for br in range(N_NSA_BRANCH)]
    out = gates[0] * oc_ref[0, 0] + gates[1] * o_sel + gates[2] * o_win
    out_t = out.T
    heads = [out_t[r * Q_BLOCK:(r + 1) * Q_BLOCK, :] for r in range(NSA_REP)]
    o_ref[...] = jnp.concatenate(heads, axis=1).astype(o_ref.dtype)


def _nsa_attn(far_count, far_list, qt, ksa, vsta, kwa, vwta, selb, oc, gl, slopes):
    kv, nqb = qt.shape[0], qt.shape[1]
    nsub = ksa.shape[1]
    nslc = selb.shape[2]
    blk = lambda g, b, c, l: (g, b, 0, 0)
    grp = lambda g, b, c, l: (g, 0, 0, 0)
    grid_spec = pltpu.PrefetchScalarGridSpec(
        num_scalar_prefetch=2,
        grid=(kv, nqb),
        in_specs=[
            pl.BlockSpec((1, 1, NSA_DH, QCOLS), blk),
            pl.BlockSpec((1, nsub, SUB, KAUG), grp),
            pl.BlockSpec((1, nsub, VAUG, SUB), grp),
            pl.BlockSpec((1, nsub, SUB, KAUG), grp),
            pl.BlockSpec((1, nsub, VAUG, SUB), grp),
            pl.BlockSpec((1, 1, nslc, Q_BLOCK), blk),
            pl.BlockSpec((1, 1, NSA_DH, QCOLS), blk),
            pl.BlockSpec((Q_BLOCK, KAUG), lambda g, b, c, l: (b, g)),
            pl.BlockSpec((1, 1, QCOLS), lambda g, b, c, l: (g, 0, 0)),
        ],
        out_specs=pl.BlockSpec((Q_BLOCK, NSA_REP * NSA_DH), lambda g, b, c, l: (b, g)),
        scratch_shapes=[
            pltpu.VMEM((SUB, QCOLS), F32),
            pltpu.VMEM((SUB, QCOLS), F32),
            pltpu.VMEM((FAR_SLOT + 2 * FAR_GROUP, SUB, QCOLS), F32),
            pltpu.VMEM((1, QCOLS), F32),
            pltpu.VMEM((VAUG, QCOLS), F32),
            pltpu.VMEM((2, 8, QCOLS), F32),
        ],
    )
    return pl.pallas_call(
        _nsa_attn_kernel,
        out_shape=jax.ShapeDtypeStruct((nsub * SUB, NSA_HEADS * NSA_DH), BF16),
        grid_spec=grid_spec,
        compiler_params=_cparams(("arbitrary", "arbitrary")),
        name="nsa_attn",
    )(far_count, far_list, qt, ksa, vsta, kwa, vwta, selb, oc, gl, slopes)


def _far_lists(picks, nqb):
    nsub = picks.shape[-1]
    a = jnp.arange(nsub)[None, None, :]
    far = a < (jnp.arange(nqb)[None, :, None] - WINDOW // Q_BLOCK)
    active = (picks > 0) & far
    count = jnp.sum(active, axis=-1).astype(jnp.int32)
    order = jnp.argsort(jnp.where(active, 0, 1), axis=-1, stable=True).astype(jnp.int32)
    lst = jnp.where(a < count[..., None], order, 0).reshape(-1)
    return count.reshape(-1), jnp.concatenate([lst, jnp.zeros((2 * FAR_GROUP,), jnp.int32)])


def _merge_kernel(x_ref, g_ref, ya_ref, yb_ref, yc_ref, wg_ref, bg_ref, wb_ref, wo_ref, o_ref):
    x = x_ref[...]
    h = _rms(x, g_ref[...]).astype(BF16)
    mix = None
    for mi, y_ref in enumerate((ya_ref, yb_ref, yc_ref)):
        cs = slice(mi * D_MODEL, (mi + 1) * D_MODEL)
        gate = jax.nn.sigmoid(_dot(h, wg_ref[:, cs]) + bg_ref[:, cs])
        term = gate * _dot(y_ref[...], wb_ref[mi])
        mix = term if mix is None else mix + term
    o_ref[...] = x + _dot(mix.astype(BF16), wo_ref[...])


def _merge(x, g, ya, yb, yc, wg, bg, wb, wo, layer, tm=512):
    s = x.shape[0]
    row = lambda i: (i, 0)
    fixed2 = lambda i: (0, 0)
    return pl.pallas_call(
        _merge_kernel,
        out_shape=jax.ShapeDtypeStruct((s, D_MODEL), F32),
        grid=(s // tm,),
        in_specs=[
            pl.BlockSpec((tm, D_MODEL), row),
            pl.BlockSpec((1, D_MODEL), fixed2),
            pl.BlockSpec((tm, MIX_WIDTH), row),
            pl.BlockSpec((tm, MIX_WIDTH), row),
            pl.BlockSpec((tm, MIX_WIDTH), row),
            pl.BlockSpec((None, D_MODEL, N_MIXERS * D_MODEL), lambda i: (layer, 0, 0), pipeline_mode=pl.Buffered(1)),
            pl.BlockSpec((1, N_MIXERS * D_MODEL), fixed2),
            pl.BlockSpec((None, N_MIXERS, MIX_WIDTH, D_MODEL), lambda i: (layer, 0, 0, 0), pipeline_mode=pl.Buffered(1)),
            pl.BlockSpec((None, D_MODEL, D_MODEL), lambda i: (layer, 0, 0), pipeline_mode=pl.Buffered(1)),
        ],
        out_specs=pl.BlockSpec((tm, D_MODEL), row),
        compiler_params=_cparams(("arbitrary",)),
        name="merge",
    )(x, g, ya, yb, yc, wg, bg, wb, wo)


def _alibi_slope_cols():
    h = jnp.arange(1, NSA_HEADS + 1, dtype=F32)
    slopes = (2.0 ** (-8.0 * h / NSA_HEADS)).reshape(NSA_KV, NSA_REP)
    return jnp.repeat(slopes, Q_BLOCK, axis=1).reshape(NSA_KV, 1, QCOLS)


def _overlap_local():
    cl = jnp.arange(CMP_TILE)[None, :]
    jl = jnp.arange(2 * CMP_TILE_SLC)[:, None]
    ov = (cl * CMP_STRIDE < (jl + 1) * SLC_BLOCK) & (cl * CMP_STRIDE + CMP_LEN > jl * SLC_BLOCK)
    return ov.astype(BF16)


def _nsa(pcv, pgl, ksa, kwa, qt, vsta, vwta, cmp_pos, cmp_w1, cmp_w2):
    s = pcv.shape[0]
    nqb = s // Q_BLOCK
    slopes = _alibi_slope_cols()
    kca = _compress(pcv, *_compress_weights(cmp_pos[0], cmp_w1[0], cmp_w2[0], False), False)
    vcta = _compress(pcv, *_compress_weights(cmp_pos[1], cmp_w1[1], cmp_w2[1], True), True)
    oc, selb, picks = _nsa_cmp(qt, kca, vcta, _overlap_local(), slopes, s // SLC_BLOCK)
    far_count, far_list = _far_lists(picks[:, :, 0, :], nqb)
    ksa = ksa.reshape(NSA_KV, s // SUB, SUB, KAUG)
    kwa = kwa.reshape(NSA_KV, s // SUB, SUB, KAUG)
    return _nsa_attn(far_count, far_list, qt, ksa, vsta, kwa, vwta, selb, oc, pgl, slopes)


def kernel(x, ffn1_norm, ffn1_w1, ffn1_w2, mix_norm, w_in, gm_ln_g, gm_ln_b, gm_ws, gm_bs, ret_gn_g, ret_gn_b,
           cmp_pos, cmp_w1, cmp_w2, w_branch_out, w_merge_gate, b_merge_gate, w_o, ffn2_norm, ffn2_w1, ffn2_w2,
           final_norm):
    bsz, s, _ = x.shape
    depth = ffn1_w1.shape[0]
    row = lambda v: v.reshape(1, -1)
    fin = row(final_norm)
    f1w1, f1w2, f2w1, f2w2 = (_to_bf16(w) for w in (ffn1_w1, ffn1_w2, ffn2_w1, ffn2_w2))
    w_in_b, wg_b, wb_b, wo_b = (_to_bf16(w) for w in (w_in, w_merge_gate, w_branch_out, w_o))
    kcs, kcw, vones = _proj_constants()
    wn, wt = _proj_weights(w_in_b)
    outs = []
    for b in range(bsz):
        xb = x.reshape(s, D_MODEL) if bsz == 1 else x[b]
        for l in range(depth):
            xb = _ffn(xb, row(ffn1_norm[l]), f1w1, f1w2, l, fin, False)
            pa, pcv, pgl, ksa, kwa, qt, vsta, vwta = _proj(xb, row(mix_norm[l]), wn, wt, l, kcs, kcw, vones)
            y_a = _gmlp(pa, row(gm_ln_g[l]), row(gm_ln_b[l]), gm_ws[l], gm_bs[l].T)
            y_b = _retention(pa, row(ret_gn_g[l]), row(ret_gn_b[l]))
            y_c = _nsa(pcv, pgl, ksa, kwa, qt, vsta, vwta, cmp_pos[l], cmp_w1[l], cmp_w2[l])
            xb = _merge(xb, row(mix_norm[l]), y_a, y_b, y_c, wg_b, row(b_merge_gate[l]), wb_b, wo_b, l)
            xb = _ffn(xb, row(ffn2_norm[l]), f2w1, f2w2, l, fin, l == depth - 1)
        outs.append(xb)
    return outs[0].reshape(1, s, D_MODEL) if bsz == 1 else jnp.stack(outs)
```

```python
import functools
import math

import jax
import jax.numpy as jnp
from jax import lax
from jax.experimental import pallas as pl
from jax.experimental.pallas import tpu as pltpu

F32 = jnp.float32
BF16 = jnp.bfloat16

D_MODEL = 1024
D_FF = 2816
NORM_EPS = 1e-6
FFN_RES_WEIGHT = 0.5
GM_WIDTH = 512
GM_GROUPS = 4
GM_CHUNK = 128
RET_HEADS = 4
RET_DK = 64
RET_DV = 128
RET_CHUNK = 128
NSA_HEADS = 8
NSA_KV = 2
NSA_REP = NSA_HEADS // NSA_KV
NSA_DH = 64
CMP_LEN = 32
CMP_STRIDE = 16
CMP_HIDDEN = 128
SLC_BLOCK = 64
N_SELECT = 16
WINDOW = 512
Q_BLOCK = 128
N_MIXERS = 3
MIX_WIDTH = 512
IN_SPLITS = (512, 512, 256, 256, 512, 512, 512, 768, 24)
D_IN = sum(IN_SPLITS)
D_IN_PAD = 4096
BIG = 1e9
NEG = -1e30
PICKED = -(2.0 ** 127)

OFF_GM_U, OFF_GM_V, OFF_RQ, OFF_RK, OFF_RV, OFF_RG, OFF_NQ, OFF_NKV, OFF_NG = (
    0, 512, 1024, 1280, 1536, 2048, 2560, 3072, 3840)

QCOLS = NSA_REP * Q_BLOCK
SEL_TILE = 512
SEL_TILE_BLOCKS = SEL_TILE // SLC_BLOCK
SUB = 128
SUB_BLOCKS = SUB // SLC_BLOCK
FAR_GROUP = 4
KAUG = 128
COL_SEL = NSA_DH
COL_POS = NSA_DH + SEL_TILE_BLOCKS
VAUG = 80
WIN_TILES = (WINDOW + Q_BLOCK) // SUB
FAR_SLOT = 2 * WIN_TILES
CMP_TILE = 128
CMP_TILE_TOKENS = CMP_TILE * CMP_STRIDE
CMP_TILE_SLC = CMP_TILE_TOKENS // SLC_BLOCK
CMP_VARIANT_STEP = 1
N_NSA_BRANCH = 3
KVW = NSA_KV * NSA_DH
PA_WIDTH = OFF_NQ
WN_CV = PA_WIDTH
WN_GL = WN_CV + 2 * KVW
WN_KS = WN_GL + NSA_KV * KAUG
WN_WIDTH = WN_KS + 2 * NSA_KV * KAUG
WT_ROWS = NSA_HEADS * NSA_DH + 2 * KVW
VMEM_LIMIT = 56 * 1024 * 1024
CAST_BLOCK_BYTES = 4 * 1024 * 1024


def _cparams(sem, vmem=VMEM_LIMIT):
    return pltpu.CompilerParams(dimension_semantics=sem, vmem_limit_bytes=vmem)


def _cast_kernel(w_ref, o_ref):
    o_ref[...] = w_ref[...].astype(o_ref.dtype)


def _to_bf16(w):
    nl, r, c = w.shape
    tr = 16
    while r % (2 * tr) == 0 and 2 * tr * c * 4 <= CAST_BLOCK_BYTES:
        tr *= 2
    return pl.pallas_call(
        _cast_kernel,
        out_shape=jax.ShapeDtypeStruct(w.shape, BF16),
        grid=(nl, r // tr),
        in_specs=[pl.BlockSpec((None, tr, c), lambda l, i: (l, i, 0))],
        out_specs=pl.BlockSpec((None, tr, c), lambda l, i: (l, i, 0)),
        compiler_params=_cparams(("arbitrary", "arbitrary")),
        name="cast_bf16",
    )(w)


def _cast_along(nsteps, w):
    nl, r, c = w.shape
    per_layer = nsteps // nl
    tr = r // per_layer
    assert nl * per_layer == nsteps and tr * per_layer == r and tr % 16 == 0, (w.shape, nsteps)
    return pl.BlockSpec((None, tr, c), lambda i: (i // per_layer, i % per_layer, 0))


def _rms(x, g):
    return x * lax.rsqrt(jnp.mean(x * x, axis=-1, keepdims=True) + NORM_EPS) * g


def _dot(a, b):
    return jnp.dot(a, b, preferred_element_type=F32)


def _ffn_kernel(x_ref, g_ref, w1_ref, w2_ref, fin_ref, *rest, final_norm, tf, ncast):
    cast_in, o_ref, cast_out, act_ref = rest[:ncast], rest[ncast], rest[ncast + 1:2 * ncast + 1], rest[-1]
    for src, dst in zip(cast_in, cast_out):
        dst[...] = src[...].astype(dst.dtype)
    x = x_ref[...]
    h = _rms(x, g_ref[...]).astype(BF16)
    for f in range(D_FF // tf):
        a = _dot(h, w1_ref[:, f * tf:(f + 1) * tf])
        b = _dot(h, w1_ref[:, D_FF + f * tf:D_FF + (f + 1) * tf])
        act_ref[:, f * tf:(f + 1) * tf] = (jax.nn.silu(a) * b).astype(BF16)
    y = x + FFN_RES_WEIGHT * _dot(act_ref[...], w2_ref[...])
    if final_norm:
        y = _rms(y, fin_ref[...])
    o_ref[...] = y


def _ffn(x, g, w1, w2, layer, fin, final_norm, cast=(), tm=512, tf=256):
    s = x.shape[0]
    nsteps = s // tm
    fixed = lambda i: (0, 0)
    resident = pl.Buffered(1)
    cast_specs = [_cast_along(nsteps, w) for w in cast]
    outs = pl.pallas_call(
        functools.partial(_ffn_kernel, final_norm=final_norm, tf=tf, ncast=len(cast)),
        out_shape=[jax.ShapeDtypeStruct((s, D_MODEL), F32)] + [jax.ShapeDtypeStruct(w.shape, BF16) for w in cast],
        grid=(nsteps,),
        in_specs=[
            pl.BlockSpec((tm, D_MODEL), lambda i: (i, 0)),
            pl.BlockSpec((1, D_MODEL), fixed),
            pl.BlockSpec((None, D_MODEL, 2 * D_FF), lambda i: (layer, 0, 0), pipeline_mode=resident),
            pl.BlockSpec((None, D_FF, D_MODEL), lambda i: (layer, 0, 0), pipeline_mode=resident),
            pl.BlockSpec((1, D_MODEL), fixed),
        ] + cast_specs,
        out_specs=[pl.BlockSpec((tm, D_MODEL), lambda i: (i, 0))] + cast_specs,
        scratch_shapes=[pltpu.VMEM((tm, D_FF), BF16)],
        compiler_params=_cparams(("arbitrary",)),
        name="ffn",
    )(x, g, w1, w2, fin, *cast)
    return outs[0] if not cast else tuple(outs)


def _proj_kernel(x_ref, g_ref, wn_ref, wt_ref, kcs_ref, kcw_ref, vones_ref,
                 pa_ref, pcv_ref, pgl_ref, ksa_ref, kwa_ref, qt_ref, vsta_ref, vwta_ref):
    tm = x_ref.shape[0]
    h = _rms(x_ref[...], g_ref[...]).astype(BF16)
    step = 512
    for c in range(0, PA_WIDTH, step):
        pa_ref[:, c:c + step] = _dot(h, wn_ref[:, c:c + step])
    pcv_ref[...] = _dot(h, wn_ref[:, WN_CV:WN_CV + 2 * KVW])
    pgl_ref[...] = _dot(h, wn_ref[:, WN_GL:WN_GL + 2 * KAUG])
    kk = _dot(h, wn_ref[:, WN_KS:WN_KS + 4 * KAUG])
    for g in range(NSA_KV):
        ksa_ref[g] = (kk[:, g * KAUG:(g + 1) * KAUG] + kcs_ref[...]).astype(BF16)
        kwa_ref[g] = (kk[:, (NSA_KV + g) * KAUG:(NSA_KV + g + 1) * KAUG] + kcw_ref[...]).astype(BF16)
    pt = lax.dot_general(wt_ref[...], h, (((1,), (1,)), ((), ())), preferred_element_type=F32)
    nq = NSA_HEADS * NSA_DH
    for g in range(NSA_KV):
        for b in range(tm // Q_BLOCK):
            for r in range(NSA_REP):
                hd = g * NSA_REP + r
                qt_ref[g, b, :, r * Q_BLOCK:(r + 1) * Q_BLOCK] = (
                    pt[hd * NSA_DH:(hd + 1) * NSA_DH, b * Q_BLOCK:(b + 1) * Q_BLOCK].astype(BF16))
            for v_ref, first_row in ((vsta_ref, nq), (vwta_ref, nq + KVW)):
                rows = slice(first_row + g * NSA_DH, first_row + (g + 1) * NSA_DH)
                v_ref[g, b, 0:NSA_DH, :] = pt[rows, b * SUB:(b + 1) * SUB].astype(BF16)
                v_ref[g, b, NSA_DH:VAUG, :] = vones_ref[...]


def _proj(x, g, wn, wt, layer, kcs, kcw, vones, tm=512):
    s = x.shape[0]
    fixed = lambda i: (0, 0)
    resident = pl.Buffered(1)
    return pl.pallas_call(
        _proj_kernel,
        out_shape=[
            jax.ShapeDtypeStruct((s, PA_WIDTH), F32),
            jax.ShapeDtypeStruct((s, 2 * KVW), F32),
            jax.ShapeDtypeStruct((s, 2 * KAUG), F32),
            jax.ShapeDtypeStruct((NSA_KV, s, KAUG), BF16),
            jax.ShapeDtypeStruct((NSA_KV, s, KAUG), BF16),
            jax.ShapeDtypeStruct((NSA_KV, s // Q_BLOCK, NSA_DH, QCOLS), BF16),
            jax.ShapeDtypeStruct((NSA_KV, s // SUB, VAUG, SUB), BF16),
            jax.ShapeDtypeStruct((NSA_KV, s // SUB, VAUG, SUB), BF16),
        ],
        grid=(s // tm,),
        in_specs=[
            pl.BlockSpec((tm, D_MODEL), lambda i: (i, 0)),
            pl.BlockSpec((1, D_MODEL), fixed),
            pl.BlockSpec((None, D_MODEL, WN_WIDTH), lambda i: (layer, 0, 0), pipeline_mode=resident),
            pl.BlockSpec((None, WT_ROWS, D_MODEL), lambda i: (layer, 0, 0), pipeline_mode=resident),
            pl.BlockSpec((tm, KAUG), fixed),
            pl.BlockSpec((tm, KAUG), fixed),
            pl.BlockSpec((VAUG - NSA_DH, SUB), fixed),
        ],
        out_specs=[
            pl.BlockSpec((tm, PA_WIDTH), lambda i: (i, 0)),
            pl.BlockSpec((tm, 2 * KVW), lambda i: (i, 0)),
            pl.BlockSpec((tm, 2 * KAUG), lambda i: (i, 0)),
            pl.BlockSpec((NSA_KV, tm, KAUG), lambda i: (0, i, 0)),
            pl.BlockSpec((NSA_KV, tm, KAUG), lambda i: (0, i, 0)),
            pl.BlockSpec((NSA_KV, tm // Q_BLOCK, NSA_DH, QCOLS), lambda i: (0, i, 0, 0)),
            pl.BlockSpec((NSA_KV, tm // SUB, VAUG, SUB), lambda i: (0, i, 0, 0)),
            pl.BlockSpec((NSA_KV, tm // SUB, VAUG, SUB), lambda i: (0, i, 0, 0)),
        ],
        compiler_params=_cparams(("arbitrary",)),
        name="proj",
    )(x, g, wn, wt, kcs, kcw, vones)


def _proj_weights(w_in):
    w = w_in.astype(BF16)
    z = lambda n: jnp.zeros(w.shape[:-1] + (n,), BF16)
    kv = lambda i, g: w[..., OFF_NKV + i * KVW + g * NSA_DH:OFF_NKV + i * KVW + (g + 1) * NSA_DH]
    ngl = N_NSA_BRANCH * NSA_REP
    cols = [w[..., 0:PA_WIDTH], w[..., OFF_NKV:OFF_NKV + 2 * KVW]]
    for g in range(NSA_KV):
        cols += [w[..., OFF_NG + g * ngl:OFF_NG + (g + 1) * ngl], z(KAUG - ngl)]
    for i in (2, 4):
        for g in range(NSA_KV):
            cols += [kv(i, g), z(KAUG - NSA_DH)]
    wn = jnp.concatenate(cols, axis=-1)
    wt = jnp.concatenate([w[..., OFF_NQ:OFF_NQ + NSA_HEADS * NSA_DH], kv(3, 0), kv(3, 1), kv(5, 0), kv(5, 1)], axis=-1)
    return wn, jnp.swapaxes(wt, -1, -2)


def _proj_constants(tm=512):
    pos = jnp.arange(tm)
    zero = jnp.zeros((tm, NSA_DH), F32)

    def pos_cols(p):
        return jnp.stack([p // 16 * 16, p % 16], axis=1).astype(F32)

    onehot = (pos[:, None] // SLC_BLOCK % SEL_TILE_BLOCKS == jnp.arange(SEL_TILE_BLOCKS)[None, :]).astype(F32)
    tail = jnp.zeros((tm, KAUG - COL_POS - 2), F32)
    kcs = jnp.concatenate([zero, onehot, pos_cols(pos % SUB), tail], axis=1)
    kcw = jnp.concatenate([zero, jnp.zeros_like(onehot), pos_cols(pos % SUB), tail], axis=1)
    r = jnp.arange(VAUG - NSA_DH)[:, None]
    vones = jnp.broadcast_to(jnp.where(r == 0, 1.0, 0.0), (VAUG - NSA_DH, SUB)).astype(BF16)
    return kcs, kcw, vones


def _gmlp_kernel(u_ref, v_ref, lg_ref, lb_ref, ws_ref, bst_ref, o_ref, *, chunks):
    cg = GM_WIDTH // GM_GROUPS
    u = jax.nn.gelu(u_ref[...])
    v = jax.nn.gelu(v_ref[...])
    mu = jnp.mean(v, axis=-1, keepdims=True)
    vc = v - mu
    var = jnp.mean(vc * vc, axis=-1, keepdims=True)
    vn = (vc * lax.rsqrt(var + NORM_EPS) * lg_ref[...] + lb_ref[...]).astype(BF16)
    row = lax.broadcasted_iota(jnp.int32, (GM_CHUNK, GM_CHUNK), 0)
    col = lax.broadcasted_iota(jnp.int32, (GM_CHUNK, GM_CHUNK), 1)
    causal = row >= col
    bst = bst_ref[...]
    for gi in range(GM_GROUPS):
        w = jnp.where(causal, ws_ref[gi], 0.0).astype(BF16)
        bias = bst[:, gi:gi + 1]
        for c in range(chunks):
            rs = slice(c * GM_CHUNK, (c + 1) * GM_CHUNK)
            cs = slice(gi * cg, (gi + 1) * cg)
            sv = _dot(w, vn[rs, cs]) + bias
            o_ref[rs, cs] = (u[rs, cs] * sv).astype(o_ref.dtype)


def _gmlp(proj, ln_g, ln_b, ws, bst, tm=512):
    s = proj.shape[0]
    return pl.pallas_call(
        functools.partial(_gmlp_kernel, chunks=tm // GM_CHUNK),
        out_shape=jax.ShapeDtypeStruct((s, GM_WIDTH), BF16),
        grid=(s // tm,),
        in_specs=[
            pl.BlockSpec((tm, GM_WIDTH), lambda i: (i, OFF_GM_U // GM_WIDTH)),
            pl.BlockSpec((tm, GM_WIDTH), lambda i: (i, OFF_GM_V // GM_WIDTH)),
            pl.BlockSpec((1, GM_WIDTH), lambda i: (0, 0)),
            pl.BlockSpec((1, GM_WIDTH), lambda i: (0, 0)),
            pl.BlockSpec((GM_GROUPS, GM_CHUNK, GM_CHUNK), lambda i: (0, 0, 0)),
            pl.BlockSpec((GM_CHUNK, GM_GROUPS), lambda i: (0, 0)),
        ],
        out_specs=pl.BlockSpec((tm, GM_WIDTH), lambda i: (i, 0)),
        compiler_params=_cparams(("arbitrary",)),
        name="gmlp",
    )(proj, proj, ln_g, ln_b, ws, bst)


def _ret_kernel(q_ref, k_ref, v_ref, g_ref, gg_ref, gb_ref, o_ref, st_ref, *, chunks):
    @pl.when(pl.program_id(0) == 0)
    def _():
        st_ref[...] = jnp.zeros_like(st_ref)

    c = RET_CHUNK
    ri = lax.broadcasted_iota(jnp.int32, (c, c), 0)
    ci = lax.broadcasted_iota(jnp.int32, (c, c), 1)
    diff = (ri - ci).astype(F32)
    pos_k = lax.broadcasted_iota(jnp.int32, (c, RET_DK), 0).astype(F32)
    pos_v = lax.broadcasted_iota(jnp.int32, (c, RET_DV), 0).astype(F32)
    for h in range(RET_HEADS):
        log_gamma = math.log(1.0 - 2.0 ** (-5.0 - h))
        intra_decay = jnp.where(diff >= 0, jnp.exp(log_gamma * jnp.maximum(diff, 0.0)), 0.0)
        k_decay = jnp.exp(log_gamma * (c - 1.0 - pos_k))
        q_decay = jnp.exp(log_gamma * (pos_v + 1.0))
        chunk_decay = math.exp(log_gamma * c)
        gam = gg_ref[:, h * RET_DV:(h + 1) * RET_DV]
        bet = gb_ref[:, h * RET_DV:(h + 1) * RET_DV]
        for n in range(chunks):
            rs = slice(n * c, (n + 1) * c)
            q = q_ref[rs, h * RET_DK:(h + 1) * RET_DK].astype(BF16)
            kf = k_ref[rs, h * RET_DK:(h + 1) * RET_DK] * (RET_DK ** -0.5)
            k = kf.astype(BF16)
            v = v_ref[rs, h * RET_DV:(h + 1) * RET_DV].astype(BF16)
            scores = lax.dot_general(q, k, (((1,), (1,)), ((), ())), preferred_element_type=F32) * intra_decay
            intra = _dot(scores.astype(BF16), v)
            state = st_ref[h]
            cross = _dot(q, state.astype(BF16)) * q_decay
            kd = (kf * k_decay).astype(BF16)
            kv = lax.dot_general(kd, v, (((0,), (0,)), ((), ())), preferred_element_type=F32)
            st_ref[h] = state * chunk_decay + kv
            y = intra + cross
            mu = jnp.mean(y, axis=-1, keepdims=True)
            yc = y - mu
            var = jnp.mean(yc * yc, axis=-1, keepdims=True)
            yn = yc * lax.rsqrt(var + NORM_EPS) * gam + bet
            gate = g_ref[rs, h * RET_DV:(h + 1) * RET_DV]
            o_ref[rs, h * RET_DV:(h + 1) * RET_DV] = (jax.nn.silu(gate) * yn).astype(o_ref.dtype)


def _retention(proj, gn_g, gn_b, tm=512):
    s = proj.shape[0]
    hk = RET_HEADS * RET_DK
    hv = RET_HEADS * RET_DV
    return pl.pallas_call(
        functools.partial(_ret_kernel, chunks=tm // RET_CHUNK),
        out_shape=jax.ShapeDtypeStruct((s, hv), BF16),
        grid=(s // tm,),
        in_specs=[
            pl.BlockSpec((tm, hk), lambda i: (i, OFF_RQ // hk)),
            pl.BlockSpec((tm, hk), lambda i: (i, OFF_RK // hk)),
            pl.BlockSpec((tm, hv), lambda i: (i, OFF_RV // hv)),
            pl.BlockSpec((tm, hv), lambda i: (i, OFF_RG // hv)),
            pl.BlockSpec((1, hv), lambda i: (0, 0)),
            pl.BlockSpec((1, hv), lambda i: (0, 0)),
        ],
        out_specs=pl.BlockSpec((tm, hv), lambda i: (i, 0)),
        scratch_shapes=[pltpu.VMEM((RET_HEADS, RET_DK, RET_DV), F32)],
        compiler_params=_cparams(("arbitrary",)),
        name="retention",
    )(proj, proj, proj, proj, gn_g, gn_b)


def _compress_kernel(t_ref, pos_ref, w1_ref, w2_ref, o_ref, xlo_ref, xhi_ref, *, values):
    n = t_ref.shape[0] // CMP_STRIDE
    for i in range(CMP_STRIDE):
        x = t_ref[pl.ds(i, n, stride=CMP_STRIDE), :]
        xlo_ref[:, i * KVW:(i + 1) * KVW] = (x + pos_ref[0:1, i * KVW:(i + 1) * KVW]).astype(BF16)
        xhi_ref[:, i * KVW:(i + 1) * KVW] = (x + pos_ref[1:2, i * KVW:(i + 1) * KVW]).astype(BF16)
    lo = _dot(xlo_ref[...], w1_ref[0])
    hi = _dot(xhi_ref[...], w1_ref[1])
    hi_next = jnp.concatenate([hi[1:], hi[:1]], axis=0)
    rowi = lax.broadcasted_iota(jnp.int32, lo.shape, 0)
    hid = jnp.where(rowi < n - 1, jax.nn.gelu(lo + hi_next), 0.0).astype(BF16)
    for g in range(NSA_KV):
        if values:
            vt = lax.dot_general(w2_ref[g], hid, (((1,), (1,)), ((), ())), preferred_element_type=F32)
            r = lax.broadcasted_iota(jnp.int32, vt.shape, 0)
            vt = jnp.where(r == NSA_DH, 1.0, vt).astype(BF16)
            for c in range(n // CMP_TILE):
                o_ref[g, c] = vt[:, c * CMP_TILE:(c + 1) * CMP_TILE]
        else:
            k = _dot(hid, w2_ref[g])
            col = lax.broadcasted_iota(jnp.int32, k.shape, 1)
            blk = (lax.broadcasted_iota(jnp.int32, k.shape, 0) & (CMP_TILE - 1)) * CMP_STRIDE
            k = jnp.where(col == NSA_DH, blk.astype(F32), k).astype(BF16)
            for c in range(n // CMP_TILE):
                o_ref[g, c] = k[c * CMP_TILE:(c + 1) * CMP_TILE, :]


def _compress(pcv, posrows, w1, w2, values):
    s = pcv.shape[0]
    n = s // CMP_STRIDE
    nct = n // CMP_TILE
    out_tile = (VAUG, CMP_TILE) if values else (CMP_TILE, KAUG)
    width = CMP_STRIDE * KVW
    return pl.pallas_call(
        functools.partial(_compress_kernel, values=values),
        out_shape=jax.ShapeDtypeStruct((NSA_KV, nct) + out_tile, BF16),
        grid=(1,),
        in_specs=[
            pl.BlockSpec((s, KVW), lambda i: (0, 1 if values else 0)),
            pl.BlockSpec((2, width), lambda i: (0, 0)),
            pl.BlockSpec((2, width, NSA_KV * CMP_HIDDEN), lambda i: (0, 0, 0)),
            pl.BlockSpec((NSA_KV,) + w2.shape[1:], lambda i: (0, 0, 0)),
        ],
        out_specs=pl.BlockSpec((NSA_KV, nct) + out_tile, lambda i: (0, 0, 0, 0)),
        scratch_shapes=[pltpu.VMEM((n, width), BF16), pltpu.VMEM((n, width), BF16)],
        compiler_params=_cparams(("arbitrary",)),
        name="nsa_compress_v" if values else "nsa_compress_k",
    )(pcv, posrows, w1, w2)


def _compress_weights(pos, w1, w2, values):
    posrows = jnp.tile(pos.reshape(2, CMP_STRIDE, 1, NSA_DH), (1, 1, NSA_KV, 1)).reshape(2, CMP_STRIDE * KVW)
    w = w1.astype(BF16).reshape(2, CMP_STRIDE, NSA_DH, CMP_HIDDEN)
    eye = jnp.eye(NSA_KV, dtype=BF16)
    w1b = jnp.einsum('hidc,gk->higdkc', w, eye).reshape(2, CMP_STRIDE * KVW, NSA_KV * CMP_HIDDEN)
    w2g = jnp.einsum('cd,gk->gkcd', w2.astype(BF16), eye).reshape(NSA_KV, NSA_KV * CMP_HIDDEN, NSA_DH)
    if values:
        w2g = jnp.pad(w2g.transpose(0, 2, 1), ((0, 0), (0, VAUG - NSA_DH), (0, 0)))
    else:
        w2g = jnp.pad(w2g, ((0, 0), (0, 0), (0, KAUG - NSA_DH)))
    return posrows, w1b, w2g


def _nsa_cmp_kernel(qt_ref, kca_ref, vo_ref, slope_ref, oc_ref, selb_ref, any_ref, dmask_ref, jf_ref):
    qb = pl.program_id(0)
    nct = kca_ref.shape[1]
    nslc = selb_ref.shape[2]
    nsub = nslc // SUB_BLOCKS

    @pl.when(qb == 0)
    def _():
        cl = lax.broadcasted_iota(jnp.int32, (CMP_TILE, QCOLS), 0)
        q = lax.broadcasted_iota(jnp.int32, (CMP_TILE, QCOLS), 1) & (Q_BLOCK - 1)
        dmask_ref[...] = (cl * CMP_STRIDE - q).astype(F32)
        jf_ref[...] = lax.broadcasted_iota(jnp.int32, jf_ref.shape, 0).astype(F32)

    t0 = qb * Q_BLOCK
    r16 = lax.broadcasted_iota(jnp.int32, (16, QCOLS), 0)
    td = t0 // CMP_TILE_TOKENS

    def colmax(s):
        return jnp.max(s, axis=0, keepdims=True)

    def variant(nt, g):
        slope = slope_ref[g]
        qa = jnp.concatenate([qt_ref[g, 0] * (NSA_DH ** -0.5), jnp.where(r16 == 0, slope, 0.0).astype(BF16),
                              jnp.zeros((KAUG - NSA_DH - 16, QCOLS), BF16)], axis=0)
        nr = nt * CMP_TILE_SLC
        offs = [slope * (t0 - ti * CMP_TILE_TOKENS).astype(F32) for ti in range(nt)]
        zs, m = [], None
        for ti in range(nt):
            s = _dot(kca_ref[g, ti], qa)
            if ti >= nt - CMP_VARIANT_STEP - 1:
                lim = (t0 - ti * CMP_TILE_TOKENS - (CMP_LEN - 1)).astype(F32)
                s = jnp.where(dmask_ref[...] <= lim, s, NEG)
            zs.append(s)
            cand = colmax(s) - offs[ti]
            m = cand if m is None else jnp.maximum(m, cand)
        acc, pieces = None, []
        for ti in range(nt):
            p = jnp.exp(zs[ti] - (m + offs[ti])).astype(BF16)
            t = _dot(vo_ref[g, ti], p)
            acc = t[0:VAUG] if acc is None else acc + t[0:VAUG]
            pieces.append(t[VAUG:VAUG + 2 * CMP_TILE_SLC])
        inv = jnp.where(m > 0.5 * NEG, 1.0 / acc[NSA_DH:NSA_DH + 1], 0.0)
        oc_ref[g, 0] = acc[0:NSA_DH] * inv
        blocks = []
        for b in range(nt):
            part = pieces[b][0:CMP_TILE_SLC]
            if b >= 1:
                part = part + pieces[b - 1][CMP_TILE_SLC:2 * CMP_TILE_SLC]
            blocks.append(part)
        imp4 = jnp.concatenate(blocks, axis=0) * inv
        imp = imp4[:, 0:Q_BLOCK]
        for r in range(1, NSA_REP):
            imp = imp + imp4[:, r * Q_BLOCK:(r + 1) * Q_BLOCK]
        return imp

    def both_groups(nt):
        nr = nt * CMP_TILE_SLC
        width = NSA_KV * Q_BLOCK
        imp = jnp.concatenate([variant(nt, g) for g in range(NSA_KV)], axis=1)
        j = lax.broadcasted_iota(jnp.int32, (nr, width), 0)
        qq = lax.broadcasted_iota(jnp.int32, (nr, width), 1) & (Q_BLOCK - 1)
        cur = 2 * qb + jnp.where(qq >= SLC_BLOCK, 1, 0)
        forced = (j == 0) | (j == cur) | (j == cur - 1)
        imp = jnp.where(j > cur, -BIG, imp)
        imp = jnp.where(forced, PICKED, imp)
        jf = jf_ref[0:nr, :]
        for _ in range(max(min(N_SELECT, nslc) - 3, 0)):
            best = jnp.max(imp, axis=0, keepdims=True)
            first = jnp.min(jnp.where(imp == best, jf, float(nslc)), axis=0, keepdims=True)
            imp = jnp.where(jf == first, PICKED, imp)
        sel = imp == PICKED
        selb = jnp.where(sel, 0.0, NEG)
        ti = lax.broadcasted_iota(jnp.int32, (nsub, nr), 0) * SUB_BLOCKS
        tj = lax.broadcasted_iota(jnp.int32, (nsub, nr), 1)
        member = jnp.where((tj >= ti) & (tj < ti + SUB_BLOCKS), 1.0, 0.0).astype(BF16)
        count = _dot(member, jnp.where(sel, 1.0, 0.0).astype(BF16)).astype(BF16)
        for g in range(NSA_KV):
            lanes = slice(g * Q_BLOCK, (g + 1) * Q_BLOCK)
            selb_ref[g, 0, 0:nr, :] = selb[:, lanes]
            if nr < nslc:
                selb_ref[g, 0, nr:nslc, :] = jnp.full((nslc - nr, Q_BLOCK), NEG, F32)
            any_ref[g, 0] = lax.dot_general(jnp.ones((8, Q_BLOCK), BF16), count[:, lanes],
                                            (((1,), (1,)), ((), ())), preferred_element_type=F32)

    for k in range(-(-nct // CMP_VARIANT_STEP)):
        pl.when(td // CMP_VARIANT_STEP == k)(functools.partial(both_groups, min((k + 1) * CMP_VARIANT_STEP, nct)))


def _nsa_cmp(qt, kca, vcta, ovl, slopes, nslc):
    kv, nqb = qt.shape[0], qt.shape[1]
    nct = kca.shape[1]
    nsub = nslc // SUB_BLOCKS
    vo = jnp.concatenate([vcta, jnp.broadcast_to(ovl, (kv, nct) + ovl.shape)], axis=2)
    vo_rows = vo.shape[2]
    return pl.pallas_call(
        _nsa_cmp_kernel,
        out_shape=[
            jax.ShapeDtypeStruct((kv, nqb, NSA_DH, QCOLS), F32),
            jax.ShapeDtypeStruct((kv, nqb, nslc, Q_BLOCK), F32),
            jax.ShapeDtypeStruct((kv, nqb, 8, nsub), F32),
        ],
        grid=(nqb,),
        in_specs=[
            pl.BlockSpec((kv, 1, NSA_DH, QCOLS), lambda b: (0, b, 0, 0)),
            pl.BlockSpec((kv, nct, CMP_TILE, KAUG), lambda b: (0, 0, 0, 0)),
            pl.BlockSpec((kv, nct, vo_rows, CMP_TILE), lambda b: (0, 0, 0, 0)),
            pl.BlockSpec((kv, 1, QCOLS), lambda b: (0, 0, 0)),
        ],
        out_specs=[
            pl.BlockSpec((kv, 1, NSA_DH, QCOLS), lambda b: (0, b, 0, 0)),
            pl.BlockSpec((kv, 1, nslc, Q_BLOCK), lambda b: (0, b, 0, 0)),
            pl.BlockSpec((kv, 1, 8, nsub), lambda b: (0, b, 0, 0)),
        ],
        scratch_shapes=[pltpu.VMEM((CMP_TILE, QCOLS), F32), pltpu.VMEM((nslc, kv * Q_BLOCK), F32)],
        compiler_params=_cparams(("arbitrary",)),
        name="nsa_cmp_topk",
    )(qt, kca, vo, slopes)


def _nsa_attn_kernel(cnt_ref, lst_ref, qt_ref, ksa_ref, vsta_ref, kwa_ref, vwta_ref, selb_ref, oc_ref, gl_ref,
                     slope_ref, o_ref, addlo_ref, addhi_ref, z_ref, m_ref, acc_ref, car_ref):
    g = pl.program_id(0)
    qb = pl.program_id(1)
    nqb = pl.num_programs(1)
    nsub = ksa_ref.shape[1]
    slope = slope_ref[0]

    @pl.when(qb == 0)
    def _():
        cl = lax.broadcasted_iota(jnp.int32, (SUB, QCOLS), 0)
        q = lax.broadcasted_iota(jnp.int32, (SUB, QCOLS), 1) & (Q_BLOCK - 1)
        addlo_ref[...] = jnp.where(cl <= q, 0.0, NEG)
        addhi_ref[...] = jnp.where(cl > q, 0.0, NEG)

    t0 = qb * Q_BLOCK
    q = qt_ref[0, 0] * (NSA_DH ** -0.5)
    r8 =lax.broadcasted_iota(jnp.int32, (SEL_TILE_BLOCKS, QCOLS), 0)
    slope_rows = jnp.where(r8 < 2, slope, 0.0)
    zero_rows = jnp.zeros((KAUG - COL_SEL - 16, QCOLS), BF16)
    q_win = jnp.concatenate([q, jnp.concatenate([jnp.zeros_like(slope_rows), slope_rows], axis=0).astype(BF16),
                             zero_rows], axis=0)

    def q_sel(a):
        grp = a // (SEL_TILE_BLOCKS // SUB_BLOCKS)
        sb = selb_ref[0, 0, pl.ds(pl.multiple_of(grp * SEL_TILE_BLOCKS, SEL_TILE_BLOCKS), SEL_TILE_BLOCKS), :]
        sb = jnp.concatenate([sb] * NSA_REP, axis=1)
        return jnp.concatenate([q, jnp.concatenate([sb, slope_rows], axis=0).astype(BF16), zero_rows], axis=0)

    def colmax(s):
        return jnp.max(s, axis=0, keepdims=True)

    def weigh(v_ref, subs, slots, shifts):
        pv = None
        for a, slot, shift in zip(subs, slots, shifts):
            p = jnp.exp(z_ref[slot] - shift).astype(BF16)
            t = _dot(v_ref[0, a], p)
            pv = t if pv is None else pv + t
        return pv

    first = qb - WINDOW // Q_BLOCK
    near = [jnp.maximum(first + i, 0) for i in range(WIN_TILES)]
    gone = [jnp.where(first + i >= 0, 0.0, -NEG) for i in range(WIN_TILES)]
    offs_s, offs_w, mg, mw = [], [], None, None
    for i in range(WIN_TILES):
        s = _dot(ksa_ref[0, near[i]], q_sel(near[i]))
        w = _dot(kwa_ref[0, near[i]], q_win)
        if i == 0:
            w = w + addhi_ref[...]
        if i == WIN_TILES - 1:
            s = s + addlo_ref[...]
            w = w + addlo_ref[...]
        z_ref[i] = s
        z_ref[WIN_TILES + i] = w
        off_s = slope * float((WIN_TILES - 1 - i) * SUB) + gone[i]
        off_w = gone[i] - slope * float(i * SUB)
        cs, cw = colmax(s) - off_s, colmax(w) - off_w
        mg = cs if mg is None else jnp.maximum(mg, cs)
        mw = cw if mw is None else jnp.maximum(mw, cw)
        offs_s.append(off_s)
        offs_w.append(off_w)
    m_ref[...] = mg
    acc_ref[...] = weigh(vsta_ref, near, range(WIN_TILES), [mg + o for o in offs_s])
    pvw = weigh(vwta_ref, near, range(WIN_TILES, 2 * WIN_TILES), [mw + o for o in offs_w])
    o_win = pvw[0:NSA_DH] / pvw[NSA_DH:NSA_DH + 1]

    base = (g * nqb + qb) * nsub
    count = cnt_ref[g * nqb + qb]

    def far_scores(gi, half):
        offs, mg = [], None
        for u in range(FAR_GROUP):
            e = gi * FAR_GROUP + u
            a = lst_ref[base + e]
            s = _dot(ksa_ref[0, a], q_sel(a))
            z_ref[FAR_SLOT + half * FAR_GROUP + u] = s
            off = slope * (t0 - a * SUB).astype(F32) + jnp.where(e < count, 0.0, -NEG)
            cand = colmax(s) - off
            mg = cand if mg is None else jnp.maximum(mg, cand)
            offs.append(off)
        return mg, tuple(offs)

    def keep(half, mg, offs):
        car_ref[half, 0:1, :] = mg
        for u in range(FAR_GROUP):
            car_ref[half, u + 1:u + 2, :] = offs[u]

    def far_stage(gi, half):
        mg = car_ref[half, 0:1, :]
        offs = [car_ref[half, u + 1:u + 2, :] for u in range(FAR_GROUP)]
        nxt = far_scores(gi + 1, 1 - half)
        subs = [lst_ref[base + gi * FAR_GROUP + u] for u in range(FAR_GROUP)]
        slots = [FAR_SLOT + half * FAR_GROUP + u for u in range(FAR_GROUP)]
        m_old = m_ref[...]
        m_new = jnp.maximum(m_old, mg)
        pv = weigh(vsta_ref, subs, slots, [m_new + off for off in offs])
        acc_ref[...] = jnp.exp(m_old - m_new) * acc_ref[...] + pv
        m_ref[...] = m_new
        keep(1 - half, *nxt)

    ngroups = (count + FAR_GROUP - 1) // FAR_GROUP

    def far_pair(pi, carry):
        far_stage(2 * pi, 0)

        @pl.when(2 * pi + 1 < ngroups)
        def _():
            far_stage(2 * pi + 1, 1)

        return carry

    keep(0, *far_scores(0, 0))
    lax.fori_loop(0, (ngroups + 1) // 2, far_pair, 0)
    acc = acc_ref[...]
    o_sel = acc[0:NSA_DH] / acc[NSA_DH:NSA_DH + 1]

    gl_t = gl_ref[...].T
    gates = [jax.nn.sigmoid(jnp.concatenate(
        [gl_t[r * N_NSA_BRANCH + br:r * N_NSA_BRANCH + br + 1, :] for r in range(NSA_REP)], axis=1))
        for br in range(N_NSA_BRANCH)]
    out = gates[0] * oc_ref[0, 0] + gates[1] * o_sel + gates[2] * o_win
    out_t = out.T
    heads = [out_t[r * Q_BLOCK:(r + 1) * Q_BLOCK, :] for r in range(NSA_REP)]
    o_ref[...] = jnp.concatenate(heads, axis=1).astype(o_ref.dtype)


def _nsa_attn(far_count, far_list, qt, ksa, vsta, kwa, vwta, selb, oc, gl, slopes):
    kv, nqb = qt.shape[0], qt.shape[1]
    nsub = ksa.shape[1]
    nslc = selb.shape[2]
    blk = lambda g, b, c, l: (g, b, 0, 0)
    grp = lambda g, b, c, l: (g, 0, 0, 0)
    grid_spec = pltpu.PrefetchScalarGridSpec(
        num_scalar_prefetch=2,
        grid=(kv, nqb),
        in_specs=[
            pl.BlockSpec((1, 1, NSA_DH, QCOLS), blk),
            pl.BlockSpec((1, nsub, SUB, KAUG), grp),
            pl.BlockSpec((1, nsub, VAUG, SUB), grp),
            pl.BlockSpec((1, nsub, SUB, KAUG), grp),
            pl.BlockSpec((1, nsub, VAUG, SUB), grp),
            pl.BlockSpec((1, 1, nslc, Q_BLOCK), blk),
            pl.BlockSpec((1, 1, NSA_DH, QCOLS), blk),
            pl.BlockSpec((Q_BLOCK, KAUG), lambda g, b, c, l: (b, g)),
            pl.BlockSpec((1, 1, QCOLS), lambda g, b, c, l: (g, 0, 0)),
        ],
        out_specs=pl.BlockSpec((Q_BLOCK, NSA_REP * NSA_DH), lambda g, b, c, l: (b, g)),
        scratch_shapes=[
            pltpu.VMEM((SUB, QCOLS), F32),
            pltpu.VMEM((SUB, QCOLS), F32),
            pltpu.VMEM((FAR_SLOT + 2 * FAR_GROUP, SUB, QCOLS), F32),
            pltpu.VMEM((1, QCOLS), F32),
            pltpu.VMEM((VAUG, QCOLS), F32),
            pltpu.VMEM((2, 8, QCOLS), F32),
        ],
    )
    return pl.pallas_call(
        _nsa_attn_kernel,
        out_shape=jax.ShapeDtypeStruct((nsub * SUB, NSA_HEADS * NSA_DH), BF16),
        grid_spec=grid_spec,
        compiler_params=_cparams(("arbitrary", "arbitrary")),
        name="nsa_attn",
    )(far_count, far_list, qt, ksa, vsta, kwa, vwta, selb, oc, gl, slopes)


def _far_lists(picks, nqb):
    nsub = picks.shape[-1]
    a = jnp.arange(nsub)[None, None, :]
    far = a < (jnp.arange(nqb)[None, :, None] - WINDOW // Q_BLOCK)
    active = (picks > 0) & far
    count = jnp.sum(active, axis=-1).astype(jnp.int32)
    order = jnp.argsort(jnp.where(active, 0, 1), axis=-1, stable=True).astype(jnp.int32)
    lst = jnp.where(a < count[..., None], order, 0).reshape(-1)
    return count.reshape(-1), jnp.concatenate([lst, jnp.zeros((2 * FAR_GROUP,), jnp.int32)])


def _merge_kernel(x_ref, g_ref, ya_ref, yb_ref, yc_ref, wg_ref, bg_ref, wb_ref, wo_ref, o_ref):
    x = x_ref[...]
    h = _rms(x, g_ref[...]).astype(BF16)
    mix = None
    for mi, y_ref in enumerate((ya_ref, yb_ref, yc_ref)):
        cs = slice(mi * D_MODEL, (mi + 1) * D_MODEL)
        gate = jax.nn.sigmoid(_dot(h, wg_ref[:, cs]) + bg_ref[:, cs])
        term = gate * _dot(y_ref[...], wb_ref[mi])
        mix = term if mix is None else mix + term
    o_ref[...] = x + _dot(mix.astype(BF16), wo_ref[...])


def _merge(x, g, ya, yb, yc, wg, bg, wb, wo, layer, tm=512):
    s = x.shape[0]
    row = lambda i: (i, 0)
    fixed2 = lambda i: (0, 0)
    return pl.pallas_call(
        _merge_kernel,
        out_shape=jax.ShapeDtypeStruct((s, D_MODEL), F32),
        grid=(s // tm,),
        in_specs=[
            pl.BlockSpec((tm, D_MODEL), row),
            pl.BlockSpec((1, D_MODEL), fixed2),
            pl.BlockSpec((tm, MIX_WIDTH), row),
            pl.BlockSpec((tm, MIX_WIDTH), row),
            pl.BlockSpec((tm, MIX_WIDTH), row),
            pl.BlockSpec((None, D_MODEL, N_MIXERS * D_MODEL), lambda i: (layer, 0, 0), pipeline_mode=pl.Buffered(1)),
            pl.BlockSpec((1, N_MIXERS * D_MODEL), fixed2),
            pl.BlockSpec((None, N_MIXERS, MIX_WIDTH, D_MODEL), lambda i: (layer, 0, 0, 0), pipeline_mode=pl.Buffered(1)),
            pl.BlockSpec((None, D_MODEL, D_MODEL), lambda i: (layer, 0, 0), pipeline_mode=pl.Buffered(1)),
        ],
        out_specs=pl.BlockSpec((tm, D_MODEL), row),
        compiler_params=_cparams(("arbitrary",)),
        name="merge",
    )(x, g, ya, yb, yc, wg, bg, wb, wo)


def _alibi_slope_cols():
    h = jnp.arange(1, NSA_HEADS + 1, dtype=F32)
    slopes = (2.0 ** (-8.0 * h / NSA_HEADS)).reshape(NSA_KV, NSA_REP)
    return jnp.repeat(slopes, Q_BLOCK, axis=1).reshape(NSA_KV, 1, QCOLS)


def _overlap_local():
    cl = jnp.arange(CMP_TILE)[None, :]
    jl = jnp.arange(2 * CMP_TILE_SLC)[:, None]
    ov = (cl * CMP_STRIDE < (jl + 1) * SLC_BLOCK) & (cl * CMP_STRIDE + CMP_LEN > jl * SLC_BLOCK)
    return ov.astype(BF16)


def _nsa(pcv, pgl, ksa, kwa, qt, vsta, vwta, cmp_pos, cmp_w1, cmp_w2):
    s = pcv.shape[0]
    nqb = s // Q_BLOCK
    slopes = _alibi_slope_cols()
    kca = _compress(pcv, *_compress_weights(cmp_pos[0], cmp_w1[0], cmp_w2[0], False), False)
    vcta = _compress(pcv, *_compress_weights(cmp_pos[1], cmp_w1[1], cmp_w2[1], True), True)
    oc, selb, picks = _nsa_cmp(qt, kca, vcta, _overlap_local(), slopes, s // SLC_BLOCK)
    far_count, far_list = _far_lists(picks[:, :, 0, :], nqb)
    ksa = ksa.reshape(NSA_KV, s // SUB, SUB, KAUG)
    kwa = kwa.reshape(NSA_KV, s // SUB, SUB, KAUG)
    return _nsa_attn(far_count, far_list, qt, ksa, vsta, kwa, vwta, selb, oc, pgl, slopes)


def kernel(x, ffn1_norm, ffn1_w1, ffn1_w2, mix_norm, w_in, gm_ln_g, gm_ln_b, gm_ws, gm_bs, ret_gn_g, ret_gn_b,
           cmp_pos, cmp_w1, cmp_w2, w_branch_out, w_merge_gate, b_merge_gate, w_o, ffn2_norm, ffn2_w1, ffn2_w2,
           final_norm):
    bsz, s, _ = x.shape
    depth = ffn1_w1.shape[0]
    row = lambda v: v.reshape(1, -1)
    fin = row(final_norm)
    f1w1, f1w2 = _to_bf16(ffn1_w1), _to_bf16(ffn1_w2)
    later = (ffn2_w1, ffn2_w2, w_in, w_merge_gate, w_branch_out.reshape(depth, N_MIXERS * MIX_WIDTH, D_MODEL), w_o)
    converted = None
    kcs, kcw, vones = _proj_constants()
    outs = []
    for b in range(bsz):
        xb = x.reshape(s, D_MODEL) if bsz == 1 else x[b]
        for l in range(depth):
            if converted is None:
                xb, *converted = _ffn(xb, row(ffn1_norm[l]), f1w1, f1w2, l, fin, False, cast=later)
                f2w1, f2w2, w_in_b, wg_b, wb_b, wo_b = converted
                wb_b = wb_b.reshape(w_branch_out.shape)
                wn, wt = _proj_weights(w_in_b)
            else:
                xb = _ffn(xb, row(ffn1_norm[l]), f1w1, f1w2, l, fin, False)
            pa, pcv, pgl, ksa, kwa, qt, vsta, vwta = _proj(xb, row(mix_norm[l]), wn, wt, l, kcs, kcw, vones)
            y_a = _gmlp(pa, row(gm_ln_g[l]), row(gm_ln_b[l]), gm_ws[l], gm_bs[l].T)
            y_b = _retention(pa, row(ret_gn_g[l]), row(ret_gn_b[l]))
            y_c = _nsa(pcv, pgl, ksa, kwa, qt, vsta, vwta, cmp_pos[l], cmp_w1[l], cmp_w2[l])
            xb = _merge(xb, row(mix_norm[l]), y_a, y_b, y_c, wg_b, row(b_merge_gate[l]), wb_b, wo_b, l)
            xb = _ffn(xb, row(ffn2_norm[l]), f2w1, f2w2, l, fin, l == depth - 1)
        outs.append(xb)
    return outs[0].reshape(1, s, D_MODEL) if bsz == 1 else jnp.stack(outs)
```

```python
import functools
import math

import jax
import jax.numpy as jnp
from jax import lax
from jax.experimental import pallas as pl
from jax.experimental.pallas import tpu as pltpu

F32 = jnp.float32
BF16 = jnp.bfloat16

D_MODEL = 1024
D_FF = 2816
NORM_EPS = 1e-6
FFN_RES_WEIGHT = 0.5
GM_WIDTH = 512
GM_GROUPS = 4
GM_CHUNK = 128
RET_HEADS = 4
RET_DK = 64
RET_DV = 128
RET_CHUNK = 128
NSA_HEADS = 8
NSA_KV = 2
NSA_REP = NSA_HEADS // NSA_KV
NSA_DH = 64
CMP_LEN = 32
CMP_STRIDE = 16
CMP_HIDDEN = 128
SLC_BLOCK = 64
N_SELECT = 16
WINDOW = 512
Q_BLOCK = 128
N_MIXERS = 3
MIX_WIDTH = 512
IN_SPLITS = (512, 512, 256, 256, 512, 512, 512, 768, 24)
D_IN = sum(IN_SPLITS)
D_IN_PAD = 4096
BIG = 1e9
NEG = -1e30
PICKED = -(2.0 ** 127)

OFF_GM_U, OFF_GM_V, OFF_RQ, OFF_RK, OFF_RV, OFF_RG, OFF_NQ, OFF_NKV, OFF_NG = (
    0, 512, 1024, 1280, 1536, 2048, 2560, 3072, 3840)

QCOLS = NSA_REP * Q_BLOCK
SEL_TILE = 512
SEL_TILE_BLOCKS = SEL_TILE // SLC_BLOCK
SUB = 128
SUB_BLOCKS = SUB // SLC_BLOCK
FAR_GROUP = 4
KAUG = 128
COL_SEL = NSA_DH
COL_POS = NSA_DH + SEL_TILE_BLOCKS
VAUG = 80
WIN_TILES = (WINDOW + Q_BLOCK) // SUB
FAR_SLOT = 2 * WIN_TILES
CMP_TILE = 128
CMP_TILE_TOKENS = CMP_TILE * CMP_STRIDE
CMP_TILE_SLC = CMP_TILE_TOKENS // SLC_BLOCK
CMP_VARIANT_STEP = 1
N_NSA_BRANCH = 3
KVW = NSA_KV * NSA_DH
PA_WIDTH = OFF_NQ
WN_CV = PA_WIDTH
WN_GL = WN_CV + 2 * KVW
WN_KS = WN_GL + NSA_KV * KAUG
WN_WIDTH = WN_KS + 2 * NSA_KV * KAUG
WT_ROWS = NSA_HEADS * NSA_DH + 2 * KVW
VMEM_LIMIT = 56 * 1024 * 1024
CAST_BLOCK_BYTES = 4 * 1024 * 1024


def _cparams(sem, vmem=VMEM_LIMIT):
    return pltpu.CompilerParams(dimension_semantics=sem, vmem_limit_bytes=vmem)


def _cast_kernel(w_ref, o_ref):
    o_ref[...] = w_ref[...].astype(o_ref.dtype)


def _to_bf16(w):
    nl, r, c = w.shape
    tr = 16
    while r % (2 * tr) == 0 and 2 * tr * c * 4 <= CAST_BLOCK_BYTES:
        tr *= 2
    return pl.pallas_call(
        _cast_kernel,
        out_shape=jax.ShapeDtypeStruct(w.shape, BF16),
        grid=(nl, r // tr),
        in_specs=[pl.BlockSpec((None, tr, c), lambda l, i: (l, i, 0))],
        out_specs=pl.BlockSpec((None, tr, c), lambda l, i: (l, i, 0)),
        compiler_params=_cparams(("arbitrary", "arbitrary")),
        name="cast_bf16",
    )(w)


def _cast_along(nsteps, w):
    nl, r, c = w.shape
    per_layer = nsteps // nl
    tr = r // per_layer
    assert nl * per_layer == nsteps and tr * per_layer == r and tr % 16 == 0, (w.shape, nsteps)
    return pl.BlockSpec((None, tr, c), lambda i: (i // per_layer, i % per_layer, 0))


def _rms(x, g):
    return x * lax.rsqrt(jnp.mean(x * x, axis=-1, keepdims=True) + NORM_EPS) * g


def _dot(a, b):
    return jnp.dot(a, b, preferred_element_type=F32)


def _ffn_kernel(x_ref, g_ref, w1_ref, w2_ref, fin_ref, *rest, final_norm, tf, ncast):
    cast_in, o_ref, cast_out, act_ref = rest[:ncast], rest[ncast], rest[ncast + 1:2 * ncast + 1], rest[-1]
    for src, dst in zip(cast_in, cast_out):
        dst[...] = src[...].astype(dst.dtype)
    x = x_ref[...]
    h = _rms(x, g_ref[...]).astype(BF16)
    for f in range(D_FF // tf):
        a = _dot(h, w1_ref[:, f * tf:(f + 1) * tf])
        b = _dot(h, w1_ref[:, D_FF + f * tf:D_FF + (f + 1) * tf])
        act_ref[:, f * tf:(f + 1) * tf] = (jax.nn.silu(a) * b).astype(BF16)
    y = x + FFN_RES_WEIGHT * _dot(act_ref[...], w2_ref[...])
    if final_norm:
        y = _rms(y, fin_ref[...])
    o_ref[...] = y


def _ffn(x, g, w1, w2, layer, fin, final_norm, cast=(), tm=512, tf=256):
    s = x.shape[0]
    nsteps = s // tm
    fixed = lambda i: (0, 0)
    resident = pl.Buffered(1)
    cast_specs = [_cast_along(nsteps, w) for w in cast]
    outs = pl.pallas_call(
        functools.partial(_ffn_kernel, final_norm=final_norm, tf=tf, ncast=len(cast)),
        out_shape=[jax.ShapeDtypeStruct((s, D_MODEL), F32)] + [jax.ShapeDtypeStruct(w.shape, BF16) for w in cast],
        grid=(nsteps,),
        in_specs=[
            pl.BlockSpec((tm, D_MODEL), lambda i: (i, 0)),
            pl.BlockSpec((1, D_MODEL), fixed),
            pl.BlockSpec((None, D_MODEL, 2 * D_FF), lambda i: (layer, 0, 0), pipeline_mode=resident),
            pl.BlockSpec((None, D_FF, D_MODEL), lambda i: (layer, 0, 0), pipeline_mode=resident),
            pl.BlockSpec((1, D_MODEL), fixed),
        ] + cast_specs,
        out_specs=[pl.BlockSpec((tm, D_MODEL), lambda i: (i, 0))] + cast_specs,
        scratch_shapes=[pltpu.VMEM((tm, D_FF), BF16)],
        compiler_params=_cparams(("arbitrary",)),
        name="ffn",
    )(x, g, w1, w2, fin, *cast)
    return outs[0] if not cast else tuple(outs)


def _proj_kernel(x_ref, g_ref, wn_ref, wt_ref, kcs_ref, kcw_ref, vones_ref,
                 pa_ref, pcv_ref, pgl_ref, ksa_ref, kwa_ref, qt_ref, vsta_ref, vwta_ref):
    tm = x_ref.shape[0]
    h = _rms(x_ref[...], g_ref[...]).astype(BF16)
    step = 512
    for c in range(0, PA_WIDTH, step):
        pa_ref[:, c:c + step] = _dot(h, wn_ref[:, c:c + step])
    pcv_ref[...] = _dot(h, wn_ref[:, WN_CV:WN_CV + 2 * KVW])
    pgl_ref[...] = _dot(h, wn_ref[:, WN_GL:WN_GL + 2 * KAUG])
    kk = _dot(h, wn_ref[:, WN_KS:WN_KS + 4 * KAUG])
    for g in range(NSA_KV):
        ksa_ref[g] = (kk[:, g * KAUG:(g + 1) * KAUG] + kcs_ref[...]).astype(BF16)
        kwa_ref[g] = (kk[:, (NSA_KV + g) * KAUG:(NSA_KV + g + 1) * KAUG] + kcw_ref[...]).astype(BF16)
    pt = lax.dot_general(wt_ref[...], h, (((1,), (1,)), ((), ())), preferred_element_type=F32)
    nq = NSA_HEADS * NSA_DH
    for g in range(NSA_KV):
        for b in range(tm // Q_BLOCK):
            for r in range(NSA_REP):
                hd = g * NSA_REP + r
                qt_ref[g, b, :, r * Q_BLOCK:(r + 1) * Q_BLOCK] = (
                    pt[hd * NSA_DH:(hd + 1) * NSA_DH, b * Q_BLOCK:(b + 1) * Q_BLOCK].astype(BF16))
            for v_ref, first_row in ((vsta_ref, nq), (vwta_ref, nq + KVW)):
                rows = slice(first_row + g * NSA_DH, first_row + (g + 1) * NSA_DH)
                v_ref[g, b, 0:NSA_DH, :] = pt[rows, b * SUB:(b + 1) * SUB].astype(BF16)
                v_ref[g, b, NSA_DH:VAUG, :] = vones_ref[...]


def _proj(x, g, wn, wt, layer, kcs, kcw, vones, tm=512):
    s = x.shape[0]
    fixed = lambda i: (0, 0)
    resident = pl.Buffered(1)
    return pl.pallas_call(
        _proj_kernel,
        out_shape=[
            jax.ShapeDtypeStruct((s, PA_WIDTH), F32),
            jax.ShapeDtypeStruct((s, 2 * KVW), F32),
            jax.ShapeDtypeStruct((s, 2 * KAUG), F32),
            jax.ShapeDtypeStruct((NSA_KV, s, KAUG), BF16),
            jax.ShapeDtypeStruct((NSA_KV, s, KAUG), BF16),
            jax.ShapeDtypeStruct((NSA_KV, s // Q_BLOCK, NSA_DH, QCOLS), BF16),
            jax.ShapeDtypeStruct((NSA_KV, s // SUB, VAUG, SUB), BF16),
            jax.ShapeDtypeStruct((NSA_KV, s // SUB, VAUG, SUB), BF16),
        ],
        grid=(s // tm,),
        in_specs=[
            pl.BlockSpec((tm, D_MODEL), lambda i: (i, 0)),
            pl.BlockSpec((1, D_MODEL), fixed),
            pl.BlockSpec((None, D_MODEL, WN_WIDTH), lambda i: (layer, 0, 0), pipeline_mode=resident),
            pl.BlockSpec((None, WT_ROWS, D_MODEL), lambda i: (layer, 0, 0), pipeline_mode=resident),
            pl.BlockSpec((tm, KAUG), fixed),
            pl.BlockSpec((tm, KAUG), fixed),
            pl.BlockSpec((VAUG - NSA_DH, SUB), fixed),
        ],
        out_specs=[
            pl.BlockSpec((tm, PA_WIDTH), lambda i: (i, 0)),
            pl.BlockSpec((tm, 2 * KVW), lambda i: (i, 0)),
            pl.BlockSpec((tm, 2 * KAUG), lambda i: (i, 0)),
            pl.BlockSpec((NSA_KV, tm, KAUG), lambda i: (0, i, 0)),
            pl.BlockSpec((NSA_KV, tm, KAUG), lambda i: (0, i, 0)),
            pl.BlockSpec((NSA_KV, tm // Q_BLOCK, NSA_DH, QCOLS), lambda i: (0, i, 0, 0)),
            pl.BlockSpec((NSA_KV, tm // SUB, VAUG, SUB), lambda i: (0, i, 0, 0)),
            pl.BlockSpec((NSA_KV, tm // SUB, VAUG, SUB), lambda i: (0, i, 0, 0)),
        ],
        compiler_params=_cparams(("arbitrary",)),
        name="proj",
    )(x, g, wn, wt, kcs, kcw, vones)


def _proj_weights(w_in):
    w = w_in.astype(BF16)
    z = lambda n: jnp.zeros(w.shape[:-1] + (n,), BF16)
    kv = lambda i, g: w[..., OFF_NKV + i * KVW + g * NSA_DH:OFF_NKV + i * KVW + (g + 1) * NSA_DH]
    ngl = N_NSA_BRANCH * NSA_REP
    cols = [w[..., 0:PA_WIDTH], w[..., OFF_NKV:OFF_NKV + 2 * KVW]]
    for g in range(NSA_KV):
        cols += [w[..., OFF_NG + g * ngl:OFF_NG + (g + 1) * ngl], z(KAUG - ngl)]
    for i in (2, 4):
        for g in range(NSA_KV):
            cols += [kv(i, g), z(KAUG - NSA_DH)]
    wn = jnp.concatenate(cols, axis=-1)
    wt = jnp.concatenate([w[..., OFF_NQ:OFF_NQ + NSA_HEADS * NSA_DH], kv(3, 0), kv(3, 1), kv(5, 0), kv(5, 1)], axis=-1)
    return wn, jnp.swapaxes(wt, -1, -2)


def _proj_constants(tm=512):
    pos = jnp.arange(tm)
    zero = jnp.zeros((tm, NSA_DH), F32)

    def pos_cols(p):
        return jnp.stack([p // 16 * 16, p % 16], axis=1).astype(F32)

    onehot = (pos[:, None] // SLC_BLOCK % SEL_TILE_BLOCKS == jnp.arange(SEL_TILE_BLOCKS)[None, :]).astype(F32)
    tail = jnp.zeros((tm, KAUG - COL_POS - 2), F32)
    kcs = jnp.concatenate([zero, onehot, pos_cols(pos % SUB), tail], axis=1)
    kcw = jnp.concatenate([zero, jnp.zeros_like(onehot), pos_cols(pos % SUB), tail], axis=1)
    r = jnp.arange(VAUG - NSA_DH)[:, None]
    vones = jnp.broadcast_to(jnp.where(r == 0, 1.0, 0.0), (VAUG - NSA_DH, SUB)).astype(BF16)
    return kcs, kcw, vones


def _gmlp_kernel(u_ref, v_ref, lg_ref, lb_ref, ws_ref, bst_ref, o_ref, *, chunks):
    cg = GM_WIDTH // GM_GROUPS
    u = jax.nn.gelu(u_ref[...])
    v = jax.nn.gelu(v_ref[...])
    mu = jnp.mean(v, axis=-1, keepdims=True)
    vc = v - mu
    var = jnp.mean(vc * vc, axis=-1, keepdims=True)
    vn = (vc * lax.rsqrt(var + NORM_EPS) * lg_ref[...] + lb_ref[...]).astype(BF16)
    row = lax.broadcasted_iota(jnp.int32, (GM_CHUNK, GM_CHUNK), 0)
    col = lax.broadcasted_iota(jnp.int32, (GM_CHUNK, GM_CHUNK), 1)
    causal = row >= col
    bst = bst_ref[...]
    for gi in range(GM_GROUPS):
        w = jnp.where(causal, ws_ref[gi], 0.0).astype(BF16)
        bias = bst[:, gi:gi + 1]
        for c in range(chunks):
            rs = slice(c * GM_CHUNK, (c + 1) * GM_CHUNK)
            cs = slice(gi * cg, (gi + 1) * cg)
            sv = _dot(w, vn[rs, cs]) + bias
            o_ref[rs, cs] = (u[rs, cs] * sv).astype(o_ref.dtype)


def _gmlp(proj, ln_g, ln_b, ws, bst, tm=512):
    s = proj.shape[0]
    return pl.pallas_call(
        functools.partial(_gmlp_kernel, chunks=tm // GM_CHUNK),
        out_shape=jax.ShapeDtypeStruct((s, GM_WIDTH), BF16),
        grid=(s // tm,),
        in_specs=[
            pl.BlockSpec((tm, GM_WIDTH), lambda i: (i, OFF_GM_U // GM_WIDTH)),
            pl.BlockSpec((tm, GM_WIDTH), lambda i: (i, OFF_GM_V // GM_WIDTH)),
            pl.BlockSpec((1, GM_WIDTH), lambda i: (0, 0)),
            pl.BlockSpec((1, GM_WIDTH), lambda i: (0, 0)),
            pl.BlockSpec((GM_GROUPS, GM_CHUNK, GM_CHUNK), lambda i: (0, 0, 0)),
            pl.BlockSpec((GM_CHUNK, GM_GROUPS), lambda i: (0, 0)),
        ],
        out_specs=pl.BlockSpec((tm, GM_WIDTH), lambda i: (i, 0)),
        compiler_params=_cparams(("arbitrary",)),
        name="gmlp",
    )(proj, proj, ln_g, ln_b, ws, bst)


def _ret_kernel(q_ref, k_ref, v_ref, g_ref, gg_ref, gb_ref, o_ref, st_ref, *, chunks):
    @pl.when(pl.program_id(0) == 0)
    def _():
        st_ref[...] = jnp.zeros_like(st_ref)

    c = RET_CHUNK
    ri = lax.broadcasted_iota(jnp.int32, (c, c), 0)
    ci = lax.broadcasted_iota(jnp.int32, (c, c), 1)
    diff = (ri - ci).astype(F32)
    pos_k = lax.broadcasted_iota(jnp.int32, (c, RET_DK), 0).astype(F32)
    pos_v = lax.broadcasted_iota(jnp.int32, (c, RET_DV), 0).astype(F32)
    for h in range(RET_HEADS):
        log_gamma = math.log(1.0 - 2.0 ** (-5.0 - h))
        intra_decay = jnp.where(diff >= 0, jnp.exp(log_gamma * jnp.maximum(diff, 0.0)), 0.0)
        k_decay = jnp.exp(log_gamma * (c - 1.0 - pos_k))
        q_decay = jnp.exp(log_gamma * (pos_v + 1.0))
        chunk_decay = math.exp(log_gamma * c)
        gam = gg_ref[:, h * RET_DV:(h + 1) * RET_DV]
        bet = gb_ref[:, h * RET_DV:(h + 1) * RET_DV]
        for n in range(chunks):
            rs = slice(n * c, (n + 1) * c)
            q = q_ref[rs, h * RET_DK:(h + 1) * RET_DK].astype(BF16)
            kf = k_ref[rs, h * RET_DK:(h + 1) * RET_DK] * (RET_DK ** -0.5)
            k = kf.astype(BF16)
            v = v_ref[rs, h * RET_DV:(h + 1) * RET_DV].astype(BF16)
            scores = lax.dot_general(q, k, (((1,), (1,)), ((), ())), preferred_element_type=F32) * intra_decay
            intra = _dot(scores.astype(BF16), v)
            state = st_ref[h]
            cross = _dot(q, state.astype(BF16)) * q_decay
            kd = (kf * k_decay).astype(BF16)
            kv = lax.dot_general(kd, v, (((0,), (0,)), ((), ())), preferred_element_type=F32)
            st_ref[h] = state * chunk_decay + kv
            y = intra + cross
            mu = jnp.mean(y, axis=-1, keepdims=True)
            yc = y - mu
            var = jnp.mean(yc * yc, axis=-1, keepdims=True)
            yn = yc * lax.rsqrt(var + NORM_EPS) * gam + bet
            gate = g_ref[rs, h * RET_DV:(h + 1) * RET_DV]
            o_ref[rs, h * RET_DV:(h + 1) * RET_DV] = (jax.nn.silu(gate) * yn).astype(o_ref.dtype)


def _retention(proj, gn_g, gn_b, tm=512):
    s = proj.shape[0]
    hk = RET_HEADS * RET_DK
    hv = RET_HEADS * RET_DV
    return pl.pallas_call(
        functools.partial(_ret_kernel, chunks=tm // RET_CHUNK),
        out_shape=jax.ShapeDtypeStruct((s, hv), BF16),
        grid=(s // tm,),
        in_specs=[
            pl.BlockSpec((tm, hk), lambda i: (i, OFF_RQ // hk)),
            pl.BlockSpec((tm, hk), lambda i: (i, OFF_RK // hk)),
            pl.BlockSpec((tm, hv), lambda i: (i, OFF_RV // hv)),
            pl.BlockSpec((tm, hv), lambda i: (i, OFF_RG // hv)),
            pl.BlockSpec((1, hv), lambda i: (0, 0)),
            pl.BlockSpec((1, hv), lambda i: (0, 0)),
        ],
        out_specs=pl.BlockSpec((tm, hv), lambda i: (i, 0)),
        scratch_shapes=[pltpu.VMEM((RET_HEADS, RET_DK, RET_DV), F32)],
        compiler_params=_cparams(("arbitrary",)),
        name="retention",
    )(proj, proj, proj, proj, gn_g, gn_b)


def _compress_kernel(t_ref, pos_ref, w1_ref, w2_ref, o_ref, xlo_ref, xhi_ref, *, values):
    n = t_ref.shape[0] // CMP_STRIDE
    for i in range(CMP_STRIDE):
        x = t_ref[pl.ds(i, n, stride=CMP_STRIDE), :]
        xlo_ref[:, i * KVW:(i + 1) * KVW] = (x + pos_ref[0:1, i * KVW:(i + 1) * KVW]).astype(BF16)
        xhi_ref[:, i * KVW:(i + 1) * KVW] = (x + pos_ref[1:2, i * KVW:(i + 1) * KVW]).astype(BF16)
    lo = _dot(xlo_ref[...], w1_ref[0])
    hi = _dot(xhi_ref[...], w1_ref[1])
    hi_next = jnp.concatenate([hi[1:], hi[:1]], axis=0)
    rowi = lax.broadcasted_iota(jnp.int32, lo.shape, 0)
    hid = jnp.where(rowi < n - 1, jax.nn.gelu(lo + hi_next), 0.0).astype(BF16)
    for g in range(NSA_KV):
        if values:
            vt = lax.dot_general(w2_ref[g], hid, (((1,), (1,)), ((), ())), preferred_element_type=F32)
            r = lax.broadcasted_iota(jnp.int32, vt.shape, 0)
            vt = jnp.where(r == NSA_DH, 1.0, vt).astype(BF16)
            for c in range(n // CMP_TILE):
                o_ref[g, c] = vt[:, c * CMP_TILE:(c + 1) * CMP_TILE]
        else:
            k = _dot(hid, w2_ref[g])
            col = lax.broadcasted_iota(jnp.int32, k.shape, 1)
            blk = (lax.broadcasted_iota(jnp.int32, k.shape, 0) & (CMP_TILE - 1)) * CMP_STRIDE
            k = jnp.where(col == NSA_DH, blk.astype(F32), k).astype(BF16)
            for c in range(n // CMP_TILE):
                o_ref[g, c] = k[c * CMP_TILE:(c + 1) * CMP_TILE, :]


def _compress(pcv, posrows, w1, w2, values):
    s = pcv.shape[0]
    n = s // CMP_STRIDE
    nct = n // CMP_TILE
    out_tile = (VAUG, CMP_TILE) if values else (CMP_TILE, KAUG)
    width = CMP_STRIDE * KVW
    return pl.pallas_call(
        functools.partial(_compress_kernel, values=values),
        out_shape=jax.ShapeDtypeStruct((NSA_KV, nct) + out_tile, BF16),
        grid=(1,),
        in_specs=[
            pl.BlockSpec((s, KVW), lambda i: (0, 1 if values else 0)),
            pl.BlockSpec((2, width), lambda i: (0, 0)),
            pl.BlockSpec((2, width, NSA_KV * CMP_HIDDEN), lambda i: (0, 0, 0)),
            pl.BlockSpec((NSA_KV,) + w2.shape[1:], lambda i: (0, 0, 0)),
        ],
        out_specs=pl.BlockSpec((NSA_KV, nct) + out_tile, lambda i: (0, 0, 0, 0)),
        scratch_shapes=[pltpu.VMEM((n, width), BF16), pltpu.VMEM((n, width), BF16)],
        compiler_params=_cparams(("arbitrary",)),
        name="nsa_compress_v" if values else "nsa_compress_k",
    )(pcv, posrows, w1, w2)


def _compress_weights(pos, w1, w2, values):
    posrows = jnp.tile(pos.reshape(2, CMP_STRIDE, 1, NSA_DH), (1, 1, NSA_KV, 1)).reshape(2, CMP_STRIDE * KVW)
    w = w1.astype(BF16).reshape(2, CMP_STRIDE, NSA_DH, CMP_HIDDEN)
    eye = jnp.eye(NSA_KV, dtype=BF16)
    w1b = jnp.einsum('hidc,gk->higdkc', w, eye).reshape(2, CMP_STRIDE * KVW, NSA_KV * CMP_HIDDEN)
    w2g = jnp.einsum('cd,gk->gkcd', w2.astype(BF16), eye).reshape(NSA_KV, NSA_KV * CMP_HIDDEN, NSA_DH)
    if values:
        w2g = jnp.pad(w2g.transpose(0, 2, 1), ((0, 0), (0, VAUG - NSA_DH), (0, 0)))
    else:
        w2g = jnp.pad(w2g, ((0, 0), (0, 0), (0, KAUG - NSA_DH)))
    return posrows, w1b, w2g


def _nsa_cmp_kernel(qt_ref, kca_ref, vo_ref, slope_ref, oc_ref, selb_ref, any_ref, dmask_ref, jf_ref):
    qb = pl.program_id(0)
    nct = kca_ref.shape[1]
    nslc = selb_ref.shape[2]
    nsub = nslc // SUB_BLOCKS

    @pl.when(qb == 0)
    def _():
        cl = lax.broadcasted_iota(jnp.int32, (CMP_TILE, QCOLS), 0)
        q = lax.broadcasted_iota(jnp.int32, (CMP_TILE, QCOLS), 1) & (Q_BLOCK - 1)
        dmask_ref[...] = (cl * CMP_STRIDE - q).astype(F32)
        jf_ref[...] = lax.broadcasted_iota(jnp.int32, jf_ref.shape, 0).astype(F32)

    t0 = qb * Q_BLOCK
    r16 = lax.broadcasted_iota(jnp.int32, (16, QCOLS), 0)
    td = t0 // CMP_TILE_TOKENS

    def colmax(s):
        return jnp.max(s, axis=0, keepdims=True)

    def tile_scores(nt, g):
        slope = slope_ref[g]
        qa = jnp.concatenate([qt_ref[g, 0] * (NSA_DH ** -0.5), jnp.where(r16 == 0, slope, 0.0).astype(BF16),
                              jnp.zeros((KAUG - NSA_DH - 16, QCOLS), BF16)], axis=0)
        offs = [slope * (t0 - ti * CMP_TILE_TOKENS).astype(F32) for ti in range(nt)]
        zs, m = [], None
        for ti in range(nt):
            s = _dot(kca_ref[g, ti], qa)
            if ti >= nt - CMP_VARIANT_STEP - 1:
                lim = (t0 - ti * CMP_TILE_TOKENS - (CMP_LEN - 1)).astype(F32)
                s = jnp.where(dmask_ref[...] <= lim, s, NEG)
            zs.append(s)
            cand = colmax(s) - offs[ti]
            m = cand if m is None else jnp.maximum(m, cand)
        return zs, m, offs

    def importance(nt, g, zs, m, offs):
        acc, pieces = None, []
        for ti in range(nt):
            p = jnp.exp(zs[ti] - (m + offs[ti])).astype(BF16)
            t = _dot(vo_ref[g, ti], p)
            acc = t[0:VAUG] if acc is None else acc + t[0:VAUG]
            pieces.append(t[VAUG:VAUG + 2 * CMP_TILE_SLC])
        inv = jnp.where(m > 0.5 * NEG, 1.0 / acc[NSA_DH:NSA_DH + 1], 0.0)
        oc_ref[g, 0] = acc[0:NSA_DH] * inv
        blocks = []
        for b in range(nt):
            part = pieces[b][0:CMP_TILE_SLC]
            if b >= 1:
                part = part + pieces[b - 1][CMP_TILE_SLC:2 * CMP_TILE_SLC]
            blocks.append(part)
        imp4 = jnp.concatenate(blocks, axis=0) * inv
        imp = imp4[:, 0:Q_BLOCK]
        for r in range(1, NSA_REP):
            imp = imp + imp4[:, r * Q_BLOCK:(r + 1) * Q_BLOCK]
        return imp

    def both_groups(nt):
        nr = nt * CMP_TILE_SLC
        width = NSA_KV * Q_BLOCK
        scored = [tile_scores(nt, g) for g in range(NSA_KV)]
        imp = jnp.concatenate([importance(nt, g, *scored[g]) for g in range(NSA_KV)], axis=1)
        j = lax.broadcasted_iota(jnp.int32, (nr, width), 0)
        qq = lax.broadcasted_iota(jnp.int32, (nr, width), 1) & (Q_BLOCK - 1)
        cur = 2 * qb + jnp.where(qq >= SLC_BLOCK, 1, 0)
        forced = (j == 0) | (j == cur) | (j == cur - 1)
        imp = jnp.where(j > cur, -BIG, imp)
        imp = jnp.where(forced, PICKED, imp)
        jf = jf_ref[0:nr, :]
        for _ in range(max(min(N_SELECT, nslc) - 3, 0)):
            best = jnp.max(imp, axis=0, keepdims=True)
            first = jnp.min(jnp.where(imp == best, jf, float(nslc)), axis=0, keepdims=True)
            imp = jnp.where(jf == first, PICKED, imp)
        sel = imp == PICKED
        selb = jnp.where(sel, 0.0, NEG)
        ti = lax.broadcasted_iota(jnp.int32, (nsub, nr), 0) * SUB_BLOCKS
        tj = lax.broadcasted_iota(jnp.int32, (nsub, nr), 1)
        member = jnp.where((tj >= ti) & (tj < ti + SUB_BLOCKS), 1.0, 0.0).astype(BF16)
        count = _dot(member, jnp.where(sel, 1.0, 0.0).astype(BF16)).astype(BF16)
        for g in range(NSA_KV):
            lanes = slice(g * Q_BLOCK, (g + 1) * Q_BLOCK)
            selb_ref[g, 0, 0:nr, :] = selb[:, lanes]
            if nr < nslc:
                selb_ref[g, 0, nr:nslc, :] = jnp.full((nslc - nr, Q_BLOCK), NEG, F32)
            any_ref[g, 0] = lax.dot_general(jnp.ones((8, Q_BLOCK), BF16), count[:, lanes],
                                            (((1,), (1,)), ((), ())), preferred_element_type=F32)

    for k in range(-(-nct // CMP_VARIANT_STEP)):
        pl.when(td // CMP_VARIANT_STEP == k)(functools.partial(both_groups, min((k + 1) * CMP_VARIANT_STEP, nct)))


def _nsa_cmp(qt, kca, vcta, ovl, slopes, nslc):
    kv, nqb = qt.shape[0], qt.shape[1]
    nct = kca.shape[1]
    nsub = nslc // SUB_BLOCKS
    vo = jnp.concatenate([vcta, jnp.broadcast_to(ovl, (kv, nct) + ovl.shape)], axis=2)
    vo_rows = vo.shape[2]
    return pl.pallas_call(
        _nsa_cmp_kernel,
        out_shape=[
            jax.ShapeDtypeStruct((kv, nqb, NSA_DH, QCOLS), F32),
            jax.ShapeDtypeStruct((kv, nqb, nslc, Q_BLOCK), F32),
            jax.ShapeDtypeStruct((kv, nqb, 8, nsub), F32),
        ],
        grid=(nqb,),
        in_specs=[
            pl.BlockSpec((kv, 1, NSA_DH, QCOLS), lambda b: (0, b, 0, 0)),
            pl.BlockSpec((kv, nct, CMP_TILE, KAUG), lambda b: (0, 0, 0, 0)),
            pl.BlockSpec((kv, nct, vo_rows, CMP_TILE), lambda b: (0, 0, 0, 0)),
            pl.BlockSpec((kv, 1, QCOLS), lambda b: (0, 0, 0)),
        ],
        out_specs=[
            pl.BlockSpec((kv, 1, NSA_DH, QCOLS), lambda b: (0, b, 0, 0)),
            pl.BlockSpec((kv, 1, nslc, Q_BLOCK), lambda b: (0, b, 0, 0)),
            pl.BlockSpec((kv, 1, 8, nsub), lambda b: (0, b, 0, 0)),
        ],
        scratch_shapes=[pltpu.VMEM((CMP_TILE, QCOLS), F32), pltpu.VMEM((nslc, kv * Q_BLOCK), F32)],
        compiler_params=_cparams(("arbitrary",)),
        name="nsa_cmp_topk",
    )(qt, kca, vo, slopes)


def _nsa_attn_kernel(cnt_ref, lst_ref, qt_ref, ksa_ref, vsta_ref, kwa_ref, vwta_ref, selb_ref, oc_ref, gl_ref,
                     slope_ref, o_ref, addlo_ref, addhi_ref, z_ref, m_ref, acc_ref, car_ref, part_ref, gate_ref):
    g = pl.program_id(0)
    qb = pl.program_id(1)
    nqb = pl.num_programs(1)
    nsub = ksa_ref.shape[1]
    slope = slope_ref[0]

    @pl.when(qb == 0)
    def _():
        cl = lax.broadcasted_iota(jnp.int32, (SUB, QCOLS), 0)
        q = lax.broadcasted_iota(jnp.int32, (SUB, QCOLS), 1) & (Q_BLOCK - 1)
        addlo_ref[...] = jnp.where(cl <= q, 0.0, NEG)
        addhi_ref[...] = jnp.where(cl > q, 0.0, NEG)

    t0 = qb * Q_BLOCK
    q = qt_ref[0, 0] * (NSA_DH ** -0.5)
    r8 =lax.broadcasted_iota(jnp.int32, (SEL_TILE_BLOCKS, QCOLS), 0)
    slope_rows = jnp.where(r8 < 2, slope, 0.0)
    zero_rows = jnp.zeros((KAUG - COL_SEL - 16, QCOLS), BF16)
    q_win = jnp.concatenate([q, jnp.concatenate([jnp.zeros_like(slope_rows), slope_rows], axis=0).astype(BF16),
                             zero_rows], axis=0)

    def q_sel(a):
        grp = a // (SEL_TILE_BLOCKS // SUB_BLOCKS)
        sb = selb_ref[0, 0, pl.ds(pl.multiple_of(grp * SEL_TILE_BLOCKS, SEL_TILE_BLOCKS), SEL_TILE_BLOCKS), :]
        sb = jnp.concatenate([sb] * NSA_REP, axis=1)
        return jnp.concatenate([q, jnp.concatenate([sb, slope_rows], axis=0).astype(BF16), zero_rows], axis=0)

    def colmax(s):
        return jnp.max(s, axis=0, keepdims=True)

    def weigh(v_ref, subs, slots, shifts):
        pv = None
        for a, slot, shift in zip(subs, slots, shifts):
            p = jnp.exp(z_ref[slot] - shift).astype(BF16)
            t = _dot(v_ref[0, a], p)
            pv = t if pv is None else pv + t
        return pv

    first = qb - WINDOW // Q_BLOCK
    near = [jnp.maximum(first + i, 0) for i in range(WIN_TILES)]
    gone = [jnp.where(first + i >= 0, 0.0, -NEG) for i in range(WIN_TILES)]
    offs_s, offs_w, mg, mw = [], [], None, None
    for i in range(WIN_TILES):
        w = _dot(kwa_ref[0, near[i]], q_win)
        if i == 0:
            w = w + addhi_ref[...]
        if i == WIN_TILES - 1:
            w = w + addlo_ref[...]
        z_ref[WIN_TILES + i] = w
        off_w = gone[i] - slope * float(i * SUB)
        cw = colmax(w) - off_w
        mw = cw if mw is None else jnp.maximum(mw, cw)
        offs_w.append(off_w)
    for i in range(WIN_TILES):
        s = _dot(ksa_ref[0, near[i]], q_sel(near[i]))
        if i == WIN_TILES - 1:
            s = s + addlo_ref[...]
        z_ref[i] = s
        off_s = slope * float((WIN_TILES - 1 - i) * SUB) + gone[i]
        cs = colmax(s) - off_s
        mg = cs if mg is None else jnp.maximum(mg, cs)
        offs_s.append(off_s)
    pvw = weigh(vwta_ref, near, range(WIN_TILES, 2 * WIN_TILES), [mw + o for o in offs_w])
    o_win = pvw[0:NSA_DH] / pvw[NSA_DH:NSA_DH + 1]
    gl_t = gl_ref[...].T
    gates = [jax.nn.sigmoid(jnp.concatenate(
        [gl_t[r * N_NSA_BRANCH + br:r * N_NSA_BRANCH + br + 1, :] for r in range(NSA_REP)], axis=1))
        for br in range(N_NSA_BRANCH)]
    part_ref[...] = gates[0] * oc_ref[0, 0] + gates[2] * o_win
    gate_ref[...] = gates[1]

    base = (g * nqb + qb) * nsub
    count = cnt_ref[g * nqb + qb]

    def far_score(gi, half, u):
        e = gi * FAR_GROUP + u
        a = lst_ref[base + e]
        s = _dot(ksa_ref[0, a], q_sel(a))
        z_ref[FAR_SLOT + half * FAR_GROUP + u] = s
        off = slope * (t0 - a * SUB).astype(F32) + jnp.where(e < count, 0.0, -NEG)
        return colmax(s) - off, off

    def far_scores(gi, half):
        offs, mg = [], None
        for u in range(FAR_GROUP):
            cand, off = far_score(gi, half, u)
            mg = cand if mg is None else jnp.maximum(mg, cand)
            offs.append(off)
        return mg, tuple(offs)

    def keep(half, mg, offs):
        car_ref[half, 0:1, :] = mg
        for u in range(FAR_GROUP):
            car_ref[half, u + 1:u + 2, :] = offs[u]

    def far_stage(gi, half):
        mg = car_ref[half, 0:1, :]
        offs = [car_ref[half, u + 1:u + 2, :] for u in range(FAR_GROUP)]
        m_old = m_ref[...]
        m_new = jnp.maximum(m_old, mg)
        pv, nmg, noffs = None, None, []
        for u in range(FAR_GROUP):
            cand, off = far_score(gi + 1, 1 - half, u)
            nmg = cand if nmg is None else jnp.maximum(nmg, cand)
            noffs.append(off)
            a = lst_ref[base + gi * FAR_GROUP + u]
            t = weigh(vsta_ref, [a], [FAR_SLOT + half * FAR_GROUP + u], [m_new + offs[u]])
            pv = t if pv is None else pv + t
        acc_ref[...] = jnp.exp(m_old - m_new) * acc_ref[...] + pv
        m_ref[...] = m_new
        keep(1 - half, nmg, noffs)

    ngroups = (count + FAR_GROUP - 1) // FAR_GROUP

    def far_pair(pi, carry):
        far_stage(2 * pi, 0)

        @pl.when(2 * pi + 1 < ngroups)
        def _():
            far_stage(2 * pi + 1, 1)

        return carry

    keep(0, *far_scores(0, 0))
    m_ref[...] = mg
    acc_ref[...] = weigh(vsta_ref, near, range(WIN_TILES), [mg + o for o in offs_s])
    lax.fori_loop(0, (ngroups + 1) // 2, far_pair, 0)
    acc = acc_ref[...]
    o_sel = acc[0:NSA_DH] / acc[NSA_DH:NSA_DH + 1]

    out = part_ref[...] + gate_ref[...] * o_sel
    out_t = out.T
    heads = [out_t[r * Q_BLOCK:(r + 1) * Q_BLOCK, :] for r in range(NSA_REP)]
    o_ref[...] = jnp.concatenate(heads, axis=1).astype(o_ref.dtype)


def _nsa_attn(far_count, far_list, qt, ksa, vsta, kwa, vwta, selb, oc, gl, slopes):
    kv, nqb = qt.shape[0], qt.shape[1]
    nsub = ksa.shape[1]
    nslc = selb.shape[2]
    blk = lambda g, b, c, l: (g, b, 0, 0)
    grp = lambda g, b, c, l: (g, 0, 0, 0)
    grid_spec = pltpu.PrefetchScalarGridSpec(
        num_scalar_prefetch=2,
        grid=(kv, nqb),
        in_specs=[
            pl.BlockSpec((1, 1, NSA_DH, QCOLS), blk),
            pl.BlockSpec((1, nsub, SUB, KAUG), grp),
            pl.BlockSpec((1, nsub, VAUG, SUB), grp),
            pl.BlockSpec((1, nsub, SUB, KAUG), grp),
            pl.BlockSpec((1, nsub, VAUG, SUB), grp),
            pl.BlockSpec((1, 1, nslc, Q_BLOCK), blk),
            pl.BlockSpec((1, 1, NSA_DH, QCOLS), blk),
            pl.BlockSpec((Q_BLOCK, KAUG), lambda g, b, c, l: (b, g)),
            pl.BlockSpec((1, 1, QCOLS), lambda g, b, c, l: (g, 0, 0)),
        ],
        out_specs=pl.BlockSpec((Q_BLOCK, NSA_REP * NSA_DH), lambda g, b, c, l: (b, g)),
        scratch_shapes=[
            pltpu.VMEM((SUB, QCOLS), F32),
            pltpu.VMEM((SUB, QCOLS), F32),
            pltpu.VMEM((FAR_SLOT + 2 * FAR_GROUP, SUB, QCOLS), F32),
            pltpu.VMEM((1, QCOLS), F32),
            pltpu.VMEM((VAUG, QCOLS), F32),
            pltpu.VMEM((2, 8, QCOLS), F32),
            pltpu.VMEM((NSA_DH, QCOLS), F32),
            pltpu.VMEM((1, QCOLS), F32),
        ],
    )
    return pl.pallas_call(
        _nsa_attn_kernel,
        out_shape=jax.ShapeDtypeStruct((nsub * SUB, NSA_HEADS * NSA_DH), BF16),
        grid_spec=grid_spec,
        compiler_params=_cparams(("arbitrary", "arbitrary")),
        name="nsa_attn",
    )(far_count, far_list, qt, ksa, vsta, kwa, vwta, selb, oc, gl, slopes)


def _far_lists(picks, nqb):
    nsub = picks.shape[-1]
    a = jnp.arange(nsub)[None, None, :]
    far = a < (jnp.arange(nqb)[None, :, None] - WINDOW // Q_BLOCK)
    active = (picks > 0) & far
    count = jnp.sum(active, axis=-1).astype(jnp.int32)
    order = jnp.argsort(jnp.where(active, 0, 1), axis=-1, stable=True).astype(jnp.int32)
    lst = jnp.where(a < count[..., None], order, 0).reshape(-1)
    return count.reshape(-1), jnp.concatenate([lst, jnp.zeros((2 * FAR_GROUP,), jnp.int32)])


def _merge_kernel(x_ref, g_ref, ya_ref, yb_ref, yc_ref, wg_ref, bg_ref, wb_ref, wo_ref, o_ref):
    x = x_ref[...]
    h = _rms(x, g_ref[...]).astype(BF16)
    mix = None
    for mi, y_ref in enumerate((ya_ref, yb_ref, yc_ref)):
        cs = slice(mi * D_MODEL, (mi + 1) * D_MODEL)
        gate = jax.nn.sigmoid(_dot(h, wg_ref[:, cs]) + bg_ref[:, cs])
        term = gate * _dot(y_ref[...], wb_ref[mi])
        mix = term if mix is None else mix + term
    o_ref[...] = x + _dot(mix.astype(BF16), wo_ref[...])


def _merge(x, g, ya, yb, yc, wg, bg, wb, wo, layer, tm=512):
    s = x.shape[0]
    row = lambda i: (i, 0)
    fixed2 = lambda i: (0, 0)
    return pl.pallas_call(
        _merge_kernel,
        out_shape=jax.ShapeDtypeStruct((s, D_MODEL), F32),
        grid=(s // tm,),
        in_specs=[
            pl.BlockSpec((tm, D_MODEL), row),
            pl.BlockSpec((1, D_MODEL), fixed2),
            pl.BlockSpec((tm, MIX_WIDTH), row),
            pl.BlockSpec((tm, MIX_WIDTH), row),
            pl.BlockSpec((tm, MIX_WIDTH), row),
            pl.BlockSpec((None, D_MODEL, N_MIXERS * D_MODEL), lambda i: (layer, 0, 0), pipeline_mode=pl.Buffered(1)),
            pl.BlockSpec((1, N_MIXERS * D_MODEL), fixed2),
            pl.BlockSpec((None, N_MIXERS, MIX_WIDTH, D_MODEL), lambda i: (layer, 0, 0, 0), pipeline_mode=pl.Buffered(1)),
            pl.BlockSpec((None, D_MODEL, D_MODEL), lambda i: (layer, 0, 0), pipeline_mode=pl.Buffered(1)),
        ],
        out_specs=pl.BlockSpec((tm, D_MODEL), row),
        compiler_params=_cparams(("arbitrary",)),
        name="merge",
    )(x, g, ya, yb, yc, wg, bg, wb, wo)


def _alibi_slope_cols():
    h = jnp.arange(1, NSA_HEADS + 1, dtype=F32)
    slopes = (2.0 ** (-8.0 * h / NSA_HEADS)).reshape(NSA_KV, NSA_REP)
    return jnp.repeat(slopes, Q_BLOCK, axis=1).reshape(NSA_KV, 1, QCOLS)


def _overlap_local():
    cl = jnp.arange(CMP_TILE)[None, :]
    jl = jnp.arange(2 * CMP_TILE_SLC)[:, None]
    ov = (cl * CMP_STRIDE < (jl + 1) * SLC_BLOCK) & (cl * CMP_STRIDE + CMP_LEN > jl * SLC_BLOCK)
    return ov.astype(BF16)


def _nsa(pcv, pgl, ksa, kwa, qt, vsta, vwta, cmp_pos, cmp_w1, cmp_w2):
    s = pcv.shape[0]
    nqb = s // Q_BLOCK
    slopes = _alibi_slope_cols()
    kca = _compress(pcv, *_compress_weights(cmp_pos[0], cmp_w1[0], cmp_w2[0], False), False)
    vcta = _compress(pcv, *_compress_weights(cmp_pos[1], cmp_w1[1], cmp_w2[1], True), True)
    oc, selb, picks = _nsa_cmp(qt, kca, vcta, _overlap_local(), slopes, s // SLC_BLOCK)
    far_count, far_list = _far_lists(picks[:, :, 0, :], nqb)
    ksa = ksa.reshape(NSA_KV, s // SUB, SUB, KAUG)
    kwa = kwa.reshape(NSA_KV, s // SUB, SUB, KAUG)
    return _nsa_attn(far_count, far_list, qt, ksa, vsta, kwa, vwta, selb, oc, pgl, slopes)


def kernel(x, ffn1_norm, ffn1_w1, ffn1_w2, mix_norm, w_in, gm_ln_g, gm_ln_b, gm_ws, gm_bs, ret_gn_g, ret_gn_b,
           cmp_pos, cmp_w1, cmp_w2, w_branch_out, w_merge_gate, b_merge_gate, w_o, ffn2_norm, ffn2_w1, ffn2_w2,
           final_norm):
    bsz, s, _ = x.shape
    depth = ffn1_w1.shape[0]
    row = lambda v: v.reshape(1, -1)
    fin = row(final_norm)
    f1w1, f1w2 = _to_bf16(ffn1_w1), _to_bf16(ffn1_w2)
    later = (ffn2_w1, ffn2_w2, w_in, w_merge_gate, w_branch_out.reshape(depth, N_MIXERS * MIX_WIDTH, D_MODEL), w_o)
    converted = None
    kcs, kcw, vones = _proj_constants()
    outs = []
    for b in range(bsz):
        xb = x.reshape(s, D_MODEL) if bsz == 1 else x[b]
        for l in range(depth):
            if converted is None:
                xb, *converted = _ffn(xb, row(ffn1_norm[l]), f1w1, f1w2, l, fin, False, cast=later)
                f2w1, f2w2, w_in_b, wg_b, wb_b, wo_b = converted
                wb_b = wb_b.reshape(w_branch_out.shape)
                wn, wt = _proj_weights(w_in_b)
            else:
                xb = _ffn(xb, row(ffn1_norm[l]), f1w1, f1w2, l, fin, False)
            pa, pcv, pgl, ksa, kwa, qt, vsta, vwta = _proj(xb, row(mix_norm[l]), wn, wt, l, kcs, kcw, vones)
            y_a = _gmlp(pa, row(gm_ln_g[l]), row(gm_ln_b[l]), gm_ws[l], gm_bs[l].T)
            y_b = _retention(pa, row(ret_gn_g[l]), row(ret_gn_b[l]))
            y_c = _nsa(pcv, pgl, ksa, kwa, qt, vsta, vwta, cmp_pos[l], cmp_w1[l], cmp_w2[l])
            xb = _merge(xb, row(mix_norm[l]), y_a, y_b, y_c, wg_b, row(b_merge_gate[l]), wb_b, wo_b, l)
            xb = _ffn(xb, row(ffn2_norm[l]), f2w1, f2w2, l, fin, l == depth - 1)
        outs.append(xb)
    return outs[0].reshape(1, s, D_MODEL) if bsz == 1 else jnp.stack(outs)
```

```python
import functools
import math

import jax
import jax.numpy as jnp
from jax import lax
from jax.experimental import pallas as pl
from jax.experimental.pallas import tpu as pltpu

F32 = jnp.float32
BF16 = jnp.bfloat16

D_MODEL = 1024
D_FF = 2816
NORM_EPS = 1e-6
FFN_RES_WEIGHT = 0.5
GM_WIDTH = 512
GM_GROUPS = 4
GM_CHUNK = 128
RET_HEADS = 4
RET_DK = 64
RET_DV = 128
RET_CHUNK = 128
NSA_HEADS = 8
NSA_KV = 2
NSA_REP = NSA_HEADS // NSA_KV
NSA_DH = 64
CMP_LEN = 32
CMP_STRIDE = 16
CMP_HIDDEN = 128
SLC_BLOCK = 64
N_SELECT = 16
WINDOW = 512
Q_BLOCK = 128
N_MIXERS = 3
MIX_WIDTH = 512
BIG = 1e9
NEG = -1e30
PICKED = -(2.0 ** 127)

OFF_GM_U, OFF_GM_V, OFF_RQ, OFF_RK, OFF_RV, OFF_RG, OFF_NQ, OFF_NKV, OFF_NG = (
    0, 512, 1024, 1280, 1536, 2048, 2560, 3072, 3840)

QCOLS = NSA_REP * Q_BLOCK
SEL_TILE = 512
SEL_TILE_BLOCKS = SEL_TILE // SLC_BLOCK
SUB = 128
SUB_BLOCKS = SUB // SLC_BLOCK
FAR_GROUP = 4
KAUG = 128
COL_SEL = NSA_DH
COL_POS = NSA_DH + SEL_TILE_BLOCKS
VAUG = 80
WIN_TILES = (WINDOW + Q_BLOCK) // SUB
FAR_SLOT = 2 * WIN_TILES
CMP_TILE = 128
CMP_TILE_TOKENS = CMP_TILE * CMP_STRIDE
CMP_TILE_SLC = CMP_TILE_TOKENS // SLC_BLOCK
CMP_VARIANT_STEP = 1
N_NSA_BRANCH = 3
KVW = NSA_KV * NSA_DH
PA_WIDTH = OFF_NQ
WN_CV = PA_WIDTH
WN_GL = WN_CV + 2 * KVW
WN_KS = WN_GL + NSA_KV * KAUG
WN_WIDTH = WN_KS + 2 * NSA_KV * KAUG
WT_ROWS = NSA_HEADS * NSA_DH + 2 * KVW
VMEM_LIMIT = 56 * 1024 * 1024
CAST_BLOCK_BYTES = 4 * 1024 * 1024


def _cparams(sem, vmem=VMEM_LIMIT):
    return pltpu.CompilerParams(dimension_semantics=sem, vmem_limit_bytes=vmem)


def _cast_kernel(w_ref, o_ref):
    o_ref[...] = w_ref[...].astype(o_ref.dtype)


def _to_bf16(w):
    nl, r, c = w.shape
    tr = 16
    while r % (2 * tr) == 0 and 2 * tr * c * 4 <= CAST_BLOCK_BYTES:
        tr *= 2
    return pl.pallas_call(
        _cast_kernel,
        out_shape=jax.ShapeDtypeStruct(w.shape, BF16),
        grid=(nl, r // tr),
        in_specs=[pl.BlockSpec((None, tr, c), lambda l, i: (l, i, 0))],
        out_specs=pl.BlockSpec((None, tr, c), lambda l, i: (l, i, 0)),
        compiler_params=_cparams(("arbitrary", "arbitrary")),
        name="cast_bf16",
    )(w)


def _cast_along(nsteps, w):
    nl, r, c = w.shape
    per_layer = nsteps // nl
    tr = -(-r // (16 * per_layer)) * 16
    assert nl * per_layer == nsteps and tr * (per_layer - 1) < r, (w.shape, nsteps)
    return pl.BlockSpec((None, tr, c), lambda i: (i // per_layer, i % per_layer, 0))


def _rms(x, g):
    return x * lax.rsqrt(jnp.mean(x * x, axis=-1, keepdims=True) + NORM_EPS) * g


def _dot(a, b):
    return jnp.dot(a, b, preferred_element_type=F32)


def _ffn_kernel(x_ref, g_ref, w1_ref, w2_ref, fin_ref, *rest, final_norm, tf, ncast):
    cast_in, o_ref, cast_out, act_ref = rest[:ncast], rest[ncast], rest[ncast + 1:2 * ncast + 1], rest[-1]
    for src, dst in zip(cast_in, cast_out):
        dst[...] = src[...].astype(dst.dtype)
    x = x_ref[...]
    h = _rms(x, g_ref[...]).astype(BF16)
    for f in range(D_FF // tf):
        a = _dot(h, w1_ref[:, f * tf:(f + 1) * tf])
        b = _dot(h, w1_ref[:, D_FF + f * tf:D_FF + (f + 1) * tf])
        act_ref[:, f * tf:(f + 1) * tf] = (jax.nn.silu(a) * b).astype(BF16)
    y = x + FFN_RES_WEIGHT * _dot(act_ref[...], w2_ref[...])
    if final_norm:
        y = _rms(y, fin_ref[...])
    o_ref[...] = y


def _ffn(x, g, w1, w2, layer, fin, final_norm, cast=(), tm=512, tf=256):
    s = x.shape[0]
    nsteps = s // tm
    fixed = lambda i: (0, 0)
    resident = pl.Buffered(1)
    cast_specs = [_cast_along(nsteps, w) for w in cast]
    outs = pl.pallas_call(
        functools.partial(_ffn_kernel, final_norm=final_norm, tf=tf, ncast=len(cast)),
        out_shape=[jax.ShapeDtypeStruct((s, D_MODEL), F32)] + [jax.ShapeDtypeStruct(w.shape, BF16) for w in cast],
        grid=(nsteps,),
        in_specs=[
            pl.BlockSpec((tm, D_MODEL), lambda i: (i, 0)),
            pl.BlockSpec((1, D_MODEL), fixed),
            pl.BlockSpec((None, D_MODEL, 2 * D_FF), lambda i: (layer, 0, 0), pipeline_mode=resident),
            pl.BlockSpec((None, D_FF, D_MODEL), lambda i: (layer, 0, 0), pipeline_mode=resident),
            pl.BlockSpec((1, D_MODEL), fixed),
        ] + cast_specs,
        out_specs=[pl.BlockSpec((tm, D_MODEL), lambda i: (i, 0))] + cast_specs,
        scratch_shapes=[pltpu.VMEM((tm, D_FF), BF16)],
        compiler_params=_cparams(("arbitrary",)),
        name="ffn",
    )(x, g, w1, w2, fin, *cast)
    return outs[0] if not cast else tuple(outs)


def _proj_kernel(x_ref, g_ref, wn_ref, wt_ref, kcs_ref, kcw_ref, vones_ref,
                 pa_ref, pcv_ref, pgl_ref, ksa_ref, kwa_ref, qt_ref, vsta_ref, vwta_ref):
    tm = x_ref.shape[0]
    h = _rms(x_ref[...], g_ref[...]).astype(BF16)
    step = 512
    for c in range(0, PA_WIDTH, step):
        pa_ref[:, c:c + step] = _dot(h, wn_ref[:, c:c + step])
    pcv_ref[...] = _dot(h, wn_ref[:, WN_CV:WN_CV + 2 * KVW])
    pgl_ref[...] = _dot(h, wn_ref[:, WN_GL:WN_GL + 2 * KAUG])
    kk = _dot(h, wn_ref[:, WN_KS:WN_KS + 4 * KAUG])
    for g in range(NSA_KV):
        ksa_ref[g] = (kk[:, g * KAUG:(g + 1) * KAUG] + kcs_ref[...]).astype(BF16)
        kwa_ref[g] = (kk[:, (NSA_KV + g) * KAUG:(NSA_KV + g + 1) * KAUG] + kcw_ref[...]).astype(BF16)
    pt = lax.dot_general(wt_ref[...], h, (((1,), (1,)), ((), ())), preferred_element_type=F32)
    nq = NSA_HEADS * NSA_DH
    for g in range(NSA_KV):
        for b in range(tm // Q_BLOCK):
            for r in range(NSA_REP):
                hd = g * NSA_REP + r
                qt_ref[g, b, :, r * Q_BLOCK:(r + 1) * Q_BLOCK] = (
                    pt[hd * NSA_DH:(hd + 1) * NSA_DH, b * Q_BLOCK:(b + 1) * Q_BLOCK].astype(BF16))
            for v_ref, first_row in ((vsta_ref, nq), (vwta_ref, nq + KVW)):
                rows = slice(first_row + g * NSA_DH, first_row + (g + 1) * NSA_DH)
                v_ref[g, b, 0:NSA_DH, :] = pt[rows, b * SUB:(b + 1) * SUB].astype(BF16)
                v_ref[g, b, NSA_DH:VAUG, :] = vones_ref[...]


def _proj(x, g, wn, wt, layer, kcs, kcw, vones, tm=512):
    s = x.shape[0]
    fixed = lambda i: (0, 0)
    resident = pl.Buffered(1)
    return pl.pallas_call(
        _proj_kernel,
        out_shape=[
            jax.ShapeDtypeStruct((s, PA_WIDTH), F32),
            jax.ShapeDtypeStruct((s, 2 * KVW), F32),
            jax.ShapeDtypeStruct((s, 2 * KAUG), F32),
            jax.ShapeDtypeStruct((NSA_KV, s, KAUG), BF16),
            jax.ShapeDtypeStruct((NSA_KV, s, KAUG), BF16),
            jax.ShapeDtypeStruct((NSA_KV, s // Q_BLOCK, NSA_DH, QCOLS), BF16),
            jax.ShapeDtypeStruct((NSA_KV, s // SUB, VAUG, SUB), BF16),
            jax.ShapeDtypeStruct((NSA_KV, s // SUB, VAUG, SUB), BF16),
        ],
        grid=(s // tm,),
        in_specs=[
            pl.BlockSpec((tm, D_MODEL), lambda i: (i, 0)),
            pl.BlockSpec((1, D_MODEL), fixed),
            pl.BlockSpec((None, D_MODEL, WN_WIDTH), lambda i: (layer, 0, 0), pipeline_mode=resident),
            pl.BlockSpec((None, WT_ROWS, D_MODEL), lambda i: (layer, 0, 0), pipeline_mode=resident),
            pl.BlockSpec((tm, KAUG), fixed),
            pl.BlockSpec((tm, KAUG), fixed),
            pl.BlockSpec((VAUG - NSA_DH, SUB), fixed),
        ],
        out_specs=[
            pl.BlockSpec((tm, PA_WIDTH), lambda i: (i, 0)),
            pl.BlockSpec((tm, 2 * KVW), lambda i: (i, 0)),
            pl.BlockSpec((tm, 2 * KAUG), lambda i: (i, 0)),
            pl.BlockSpec((NSA_KV, tm, KAUG), lambda i: (0, i, 0)),
            pl.BlockSpec((NSA_KV, tm, KAUG), lambda i: (0, i, 0)),
            pl.BlockSpec((NSA_KV, tm // Q_BLOCK, NSA_DH, QCOLS), lambda i: (0, i, 0, 0)),
            pl.BlockSpec((NSA_KV, tm // SUB, VAUG, SUB), lambda i: (0, i, 0, 0)),
            pl.BlockSpec((NSA_KV, tm // SUB, VAUG, SUB), lambda i: (0, i, 0, 0)),
        ],
        compiler_params=_cparams(("arbitrary",)),
        name="proj",
    )(x, g, wn, wt, kcs, kcw, vones)


def _proj_weights(w_in_t):
    w = w_in_t
    z = lambda n: jnp.zeros((w.shape[0], n, D_MODEL), BF16)
    kv = lambda i, g: w[:, OFF_NKV + i * KVW + g * NSA_DH:OFF_NKV + i * KVW + (g + 1) * NSA_DH]
    ngl = N_NSA_BRANCH * NSA_REP
    rows = [w[:, 0:PA_WIDTH], w[:, OFF_NKV:OFF_NKV + 2 * KVW]]
    for g in range(NSA_KV):
        rows += [w[:, OFF_NG + g * ngl:OFF_NG + (g + 1) * ngl], z(KAUG - ngl)]
    for i in (2, 4):
        for g in range(NSA_KV):
            rows += [kv(i, g), z(KAUG - NSA_DH)]
    wn = jnp.swapaxes(jnp.concatenate(rows, axis=1), 1, 2)
    wt = jnp.concatenate([w[:, OFF_NQ:OFF_NQ + NSA_HEADS * NSA_DH], kv(3, 0), kv(3, 1), kv(5, 0), kv(5, 1)], axis=1)
    return wn, wt


def _proj_constants(tm=512):
    pos = jnp.arange(tm)
    zero = jnp.zeros((tm, NSA_DH), F32)

    def pos_cols(p):
        return jnp.stack([p // 16 * 16, p % 16], axis=1).astype(F32)

    onehot = (pos[:, None] // SLC_BLOCK % SEL_TILE_BLOCKS == jnp.arange(SEL_TILE_BLOCKS)[None, :]).astype(F32)
    tail = jnp.zeros((tm, KAUG - COL_POS - 2), F32)
    kcs = jnp.concatenate([zero, onehot, pos_cols(pos % SUB), tail], axis=1)
    kcw = jnp.concatenate([zero, jnp.zeros_like(onehot), pos_cols(pos % SUB), tail], axis=1)
    r = jnp.arange(VAUG - NSA_DH)[:, None]
    vones = jnp.broadcast_to(jnp.where(r == 0, 1.0, 0.0), (VAUG - NSA_DH, SUB)).astype(BF16)
    return kcs, kcw, vones


def _gmlp_kernel(u_ref, v_ref, lg_ref, lb_ref, ws_ref, bst_ref, o_ref, *, chunks):
    cg = GM_WIDTH // GM_GROUPS
    u = jax.nn.gelu(u_ref[...])
    v = jax.nn.gelu(v_ref[...])
    mu = jnp.mean(v, axis=-1, keepdims=True)
    vc = v - mu
    var = jnp.mean(vc * vc, axis=-1, keepdims=True)
    vn = (vc * lax.rsqrt(var + NORM_EPS) * lg_ref[...] + lb_ref[...]).astype(BF16)
    row = lax.broadcasted_iota(jnp.int32, (GM_CHUNK, GM_CHUNK), 0)
    col = lax.broadcasted_iota(jnp.int32, (GM_CHUNK, GM_CHUNK), 1)
    causal = row >= col
    bst = bst_ref[...]
    for gi in range(GM_GROUPS):
        w = jnp.where(causal, ws_ref[gi], 0.0).astype(BF16)
        bias = bst[:, gi:gi + 1]
        for c in range(chunks):
            rs = slice(c * GM_CHUNK, (c + 1) * GM_CHUNK)
            cs = slice(gi * cg, (gi + 1) * cg)
            sv = _dot(w, vn[rs, cs]) + bias
            o_ref[rs, cs] = (u[rs, cs] * sv).astype(o_ref.dtype)


def _gmlp(proj, ln_g, ln_b, ws, bst, tm=512):
    s = proj.shape[0]
    return pl.pallas_call(
        functools.partial(_gmlp_kernel, chunks=tm // GM_CHUNK),
        out_shape=jax.ShapeDtypeStruct((s, GM_WIDTH), BF16),
        grid=(s // tm,),
        in_specs=[
            pl.BlockSpec((tm, GM_WIDTH), lambda i: (i, OFF_GM_U // GM_WIDTH)),
            pl.BlockSpec((tm, GM_WIDTH), lambda i: (i, OFF_GM_V // GM_WIDTH)),
            pl.BlockSpec((1, GM_WIDTH), lambda i: (0, 0)),
            pl.BlockSpec((1, GM_WIDTH), lambda i: (0, 0)),
            pl.BlockSpec((GM_GROUPS, GM_CHUNK, GM_CHUNK), lambda i: (0, 0, 0)),
            pl.BlockSpec((GM_CHUNK, GM_GROUPS), lambda i: (0, 0)),
        ],
        out_specs=pl.BlockSpec((tm, GM_WIDTH), lambda i: (i, 0)),
        compiler_params=_cparams(("arbitrary",)),
        name="gmlp",
    )(proj, proj, ln_g, ln_b, ws, bst)


def _ret_kernel(q_ref, k_ref, v_ref, g_ref, gg_ref, gb_ref, o_ref, st_ref, *, chunks):
    @pl.when(pl.program_id(0) == 0)
    def _():
        st_ref[...] = jnp.zeros_like(st_ref)

    c = RET_CHUNK
    ri = lax.broadcasted_iota(jnp.int32, (c, c), 0)
    ci = lax.broadcasted_iota(jnp.int32, (c, c), 1)
    diff = (ri - ci).astype(F32)
    pos_k = lax.broadcasted_iota(jnp.int32, (c, RET_DK), 0).astype(F32)
    pos_v = lax.broadcasted_iota(jnp.int32, (c, RET_DV), 0).astype(F32)
    for h in range(RET_HEADS):
        log_gamma = math.log(1.0 - 2.0 ** (-5.0 - h))
        intra_decay = jnp.where(diff >= 0, jnp.exp(log_gamma * jnp.maximum(diff, 0.0)), 0.0)
        k_decay = jnp.exp(log_gamma * (c - 1.0 - pos_k))
        q_decay = jnp.exp(log_gamma * (pos_v + 1.0))
        chunk_decay = math.exp(log_gamma * c)
        gam = gg_ref[:, h * RET_DV:(h + 1) * RET_DV]
        bet = gb_ref[:, h * RET_DV:(h + 1) * RET_DV]
        for n in range(chunks):
            rs = slice(n * c, (n + 1) * c)
            q = q_ref[rs, h * RET_DK:(h + 1) * RET_DK].astype(BF16)
            kf = k_ref[rs, h * RET_DK:(h + 1) * RET_DK] * (RET_DK ** -0.5)
            k = kf.astype(BF16)
            v = v_ref[rs, h * RET_DV:(h + 1) * RET_DV].astype(BF16)
            scores = lax.dot_general(q, k, (((1,), (1,)), ((), ())), preferred_element_type=F32) * intra_decay
            intra = _dot(scores.astype(BF16), v)
            state = st_ref[h]
            cross = _dot(q, state.astype(BF16)) * q_decay
            kd = (kf * k_decay).astype(BF16)
            kv = lax.dot_general(kd, v, (((0,), (0,)), ((), ())), preferred_element_type=F32)
            st_ref[h] = state * chunk_decay + kv
            y = intra + cross
            mu = jnp.mean(y, axis=-1, keepdims=True)
            yc = y - mu
            var = jnp.mean(yc * yc, axis=-1, keepdims=True)
            yn = yc * lax.rsqrt(var + NORM_EPS) * gam + bet
            gate = g_ref[rs, h * RET_DV:(h + 1) * RET_DV]
            o_ref[rs, h * RET_DV:(h + 1) * RET_DV] = (jax.nn.silu(gate) * yn).astype(o_ref.dtype)


def _retention(proj, gn_g, gn_b, tm=512):
    s = proj.shape[0]
    hk = RET_HEADS * RET_DK
    hv = RET_HEADS * RET_DV
    return pl.pallas_call(
        functools.partial(_ret_kernel, chunks=tm // RET_CHUNK),
        out_shape=jax.ShapeDtypeStruct((s, hv), BF16),
        grid=(s // tm,),
        in_specs=[
            pl.BlockSpec((tm, hk), lambda i: (i, OFF_RQ // hk)),
            pl.BlockSpec((tm, hk), lambda i: (i, OFF_RK // hk)),
            pl.BlockSpec((tm, hv), lambda i: (i, OFF_RV // hv)),
            pl.BlockSpec((tm, hv), lambda i: (i, OFF_RG // hv)),
            pl.BlockSpec((1, hv), lambda i: (0, 0)),
            pl.BlockSpec((1, hv), lambda i: (0, 0)),
        ],
        out_specs=pl.BlockSpec((tm, hv), lambda i: (i, 0)),
        scratch_shapes=[pltpu.VMEM((RET_HEADS, RET_DK, RET_DV), F32)],
        compiler_params=_cparams(("arbitrary",)),
        name="retention",
    )(proj, proj, proj, proj, gn_g, gn_b)


def _compress_kernel(t_ref, pos_ref, w1_ref, w2_ref, o_ref, xlo_ref, xhi_ref, *, values):
    n = t_ref.shape[0] // CMP_STRIDE
    for i in range(CMP_STRIDE):
        x = t_ref[pl.ds(i, n, stride=CMP_STRIDE), :]
        xlo_ref[:, i * KVW:(i + 1) * KVW] = (x + pos_ref[0:1, i * KVW:(i + 1) * KVW]).astype(BF16)
        xhi_ref[:, i * KVW:(i + 1) * KVW] = (x + pos_ref[1:2, i * KVW:(i + 1) * KVW]).astype(BF16)
    lo = _dot(xlo_ref[...], w1_ref[0])
    hi = _dot(xhi_ref[...], w1_ref[1])
    hi_next = jnp.concatenate([hi[1:], hi[:1]], axis=0)
    rowi = lax.broadcasted_iota(jnp.int32, lo.shape, 0)
    hid = jnp.where(rowi < n - 1, jax.nn.gelu(lo + hi_next), 0.0).astype(BF16)
    for g in range(NSA_KV):
        if values:
            vt = lax.dot_general(w2_ref[g], hid, (((1,), (1,)), ((), ())), preferred_element_type=F32)
            r = lax.broadcasted_iota(jnp.int32, vt.shape, 0)
            vt = jnp.where(r == NSA_DH, 1.0, vt).astype(BF16)
            for c in range(n // CMP_TILE):
                o_ref[g, c] = vt[:, c * CMP_TILE:(c + 1) * CMP_TILE]
        else:
            k = _dot(hid, w2_ref[g])
            col = lax.broadcasted_iota(jnp.int32, k.shape, 1)
            blk = (lax.broadcasted_iota(jnp.int32, k.shape, 0) & (CMP_TILE - 1)) * CMP_STRIDE
            k = jnp.where(col == NSA_DH, blk.astype(F32), k).astype(BF16)
            for c in range(n // CMP_TILE):
                o_ref[g, c] = k[c * CMP_TILE:(c + 1) * CMP_TILE, :]


def _compress(pcv, posrows, w1, w2, layer, values):
    s = pcv.shape[0]
    n = s // CMP_STRIDE
    nct = n // CMP_TILE
    out_tile = (VAUG, CMP_TILE) if values else (CMP_TILE, KAUG)
    width = CMP_STRIDE * KVW
    return pl.pallas_call(
        functools.partial(_compress_kernel, values=values),
        out_shape=jax.ShapeDtypeStruct((NSA_KV, nct) + out_tile, BF16),
        grid=(1,),
        in_specs=[
            pl.BlockSpec((s, KVW), lambda i: (0, 1 if values else 0)),
            pl.BlockSpec((None, 2, width), lambda i: (layer, 0, 0)),
            pl.BlockSpec((None, 2, width, NSA_KV * CMP_HIDDEN), lambda i: (layer, 0, 0, 0)),
            pl.BlockSpec((None, NSA_KV) + w2.shape[2:], lambda i: (layer, 0, 0, 0)),
        ],
        out_specs=pl.BlockSpec((NSA_KV, nct) + out_tile, lambda i: (0, 0, 0, 0)),
        scratch_shapes=[pltpu.VMEM((n, width), BF16), pltpu.VMEM((n, width), BF16)],
        compiler_params=_cparams(("arbitrary",)),
        name="nsa_compress_v" if values else "nsa_compress_k",
    )(pcv, posrows, w1, w2)


def _compress_weights(pos, w1, w2, values):
    nl = pos.shape[0]
    posrows = jnp.tile(pos.reshape(nl, 2, CMP_STRIDE, 1, NSA_DH), (1, 1, 1, NSA_KV, 1)).reshape(nl, 2, CMP_STRIDE * KVW)
    w = w1.astype(BF16).reshape(nl, 2, CMP_STRIDE, NSA_DH, CMP_HIDDEN)
    eye = jnp.eye(NSA_KV, dtype=BF16)
    w1b = jnp.einsum('lhidc,gk->lhigdkc', w, eye).reshape(nl, 2, CMP_STRIDE * KVW, NSA_KV * CMP_HIDDEN)
    w2g = jnp.einsum('lcd,gk->lgkcd', w2.astype(BF16), eye).reshape(nl, NSA_KV, NSA_KV * CMP_HIDDEN, NSA_DH)
    if values:
        w2g = jnp.pad(w2g.transpose(0, 1, 3, 2), ((0, 0), (0, 0), (0, VAUG - NSA_DH), (0, 0)))
    else:
        w2g = jnp.pad(w2g, ((0, 0), (0, 0), (0, 0), (0, KAUG - NSA_DH)))
    return posrows, w1b, w2g


def _nsa_cmp_kernel(qt_ref, kca_ref, vo_ref, slope_ref, oc_ref, selb_ref, any_ref, dmask_ref, jf_ref):
    qb = pl.program_id(0)
    nct = kca_ref.shape[1]
    nslc = selb_ref.shape[2]
    nsub = nslc // SUB_BLOCKS

    @pl.when(qb == 0)
    def _():
        cl = lax.broadcasted_iota(jnp.int32, (CMP_TILE, QCOLS), 0)
        q = lax.broadcasted_iota(jnp.int32, (CMP_TILE, QCOLS), 1) & (Q_BLOCK - 1)
        dmask_ref[...] = (cl * CMP_STRIDE - q).astype(F32)
        jf_ref[...] = lax.broadcasted_iota(jnp.int32, jf_ref.shape, 0).astype(F32)

    t0 = qb * Q_BLOCK
    r16 = lax.broadcasted_iota(jnp.int32, (16, QCOLS), 0)
    td = t0 // CMP_TILE_TOKENS

    def colmax(s):
        return jnp.max(s, axis=0, keepdims=True)

    def tile_scores(nt, g):
        slope = slope_ref[g]
        qa = jnp.concatenate([qt_ref[g, 0] * (NSA_DH ** -0.5), jnp.where(r16 == 0, slope, 0.0).astype(BF16),
                              jnp.zeros((KAUG - NSA_DH - 16, QCOLS), BF16)], axis=0)
        offs = [slope * (t0 - ti * CMP_TILE_TOKENS).astype(F32) for ti in range(nt)]
        zs, m = [], None
        for ti in range(nt):
            s = _dot(kca_ref[g, ti], qa)
            if ti >= nt - CMP_VARIANT_STEP - 1:
                lim = (t0 - ti * CMP_TILE_TOKENS - (CMP_LEN - 1)).astype(F32)
                s = jnp.where(dmask_ref[...] <= lim, s, NEG)
            zs.append(s)
            cand = colmax(s) - offs[ti]
            m = cand if m is None else jnp.maximum(m, cand)
        return zs, m, offs

    def importance(nt, g, zs, m, offs):
        acc, pieces = None, []
        for ti in range(nt):
            p = jnp.exp(zs[ti] - (m + offs[ti])).astype(BF16)
            t = _dot(vo_ref[g, ti], p)
            acc = t[0:VAUG] if acc is None else acc + t[0:VAUG]
            pieces.append(t[VAUG:VAUG + 2 * CMP_TILE_SLC])
        inv = jnp.where(m > 0.5 * NEG, 1.0 / acc[NSA_DH:NSA_DH + 1], 0.0)
        oc_ref[g, 0] = acc[0:NSA_DH] * inv
        blocks = []
        for b in range(nt):
            part = pieces[b][0:CMP_TILE_SLC]
            if b >= 1:
                part = part + pieces[b - 1][CMP_TILE_SLC:2 * CMP_TILE_SLC]
            blocks.append(part)
        imp4 = jnp.concatenate(blocks, axis=0) * inv
        imp = imp4[:, 0:Q_BLOCK]
        for r in range(1, NSA_REP):
            imp = imp + imp4[:, r * Q_BLOCK:(r + 1) * Q_BLOCK]
        return imp

    def both_groups(nt):
        nr = nt * CMP_TILE_SLC
        width = NSA_KV * Q_BLOCK
        scored = [tile_scores(nt, g) for g in range(NSA_KV)]
        imp = jnp.concatenate([importance(nt, g, *scored[g]) for g in range(NSA_KV)], axis=1)
        j = lax.broadcasted_iota(jnp.int32, (nr, width), 0)
        qq = lax.broadcasted_iota(jnp.int32, (nr, width), 1) & (Q_BLOCK - 1)
        cur = 2 * qb + jnp.where(qq >= SLC_BLOCK, 1, 0)
        forced = (j == 0) | (j == cur) | (j == cur - 1)
        imp = jnp.where(j > cur, -BIG, imp)
        imp = jnp.where(forced, PICKED, imp)
        jf = jf_ref[0:nr, :]
        for _ in range(max(min(N_SELECT, nslc) - 3, 0)):
            best = jnp.max(imp, axis=0, keepdims=True)
            first = jnp.min(jnp.where(imp == best, jf, float(nslc)), axis=0, keepdims=True)
            imp = jnp.where(jf == first, PICKED, imp)
        sel = imp == PICKED
        selb = jnp.where(sel, 0.0, NEG)
        ti = lax.broadcasted_iota(jnp.int32, (nsub, nr), 0) * SUB_BLOCKS
        tj = lax.broadcasted_iota(jnp.int32, (nsub, nr), 1)
        member = jnp.where((tj >= ti) & (tj < ti + SUB_BLOCKS), 1.0, 0.0).astype(BF16)
        count = _dot(member, jnp.where(sel, 1.0, 0.0).astype(BF16)).astype(BF16)
        for g in range(NSA_KV):
            lanes = slice(g * Q_BLOCK, (g + 1) * Q_BLOCK)
            selb_ref[g, 0, 0:nr, :] = selb[:, lanes]
            if nr < nslc:
                selb_ref[g, 0, nr:nslc, :] = jnp.full((nslc - nr, Q_BLOCK), NEG, F32)
            any_ref[g, 0] = lax.dot_general(jnp.ones((8, Q_BLOCK), BF16), count[:, lanes],
                                            (((1,), (1,)), ((), ())), preferred_element_type=F32)

    for k in range(-(-nct // CMP_VARIANT_STEP)):
        pl.when(td // CMP_VARIANT_STEP == k)(functools.partial(both_groups, min((k + 1) * CMP_VARIANT_STEP, nct)))


def _nsa_cmp(qt, kca, vcta, ovl, slopes, nslc):
    kv, nqb = qt.shape[0], qt.shape[1]
    nct = kca.shape[1]
    nsub = nslc // SUB_BLOCKS
    vo = jnp.concatenate([vcta, jnp.broadcast_to(ovl, (kv, nct) + ovl.shape)], axis=2)
    vo_rows = vo.shape[2]
    return pl.pallas_call(
        _nsa_cmp_kernel,
        out_shape=[
            jax.ShapeDtypeStruct((kv, nqb, NSA_DH, QCOLS), F32),
            jax.ShapeDtypeStruct((kv, nqb, nslc, Q_BLOCK), F32),
            jax.ShapeDtypeStruct((kv, nqb, 8, nsub), F32),
        ],
        grid=(nqb,),
        in_specs=[
            pl.BlockSpec((kv, 1, NSA_DH, QCOLS), lambda b: (0, b, 0, 0)),
            pl.BlockSpec((kv, nct, CMP_TILE, KAUG), lambda b: (0, 0, 0, 0)),
            pl.BlockSpec((kv, nct, vo_rows, CMP_TILE), lambda b: (0, 0, 0, 0)),
            pl.BlockSpec((kv, 1, QCOLS), lambda b: (0, 0, 0)),
        ],
        out_specs=[
            pl.BlockSpec((kv, 1, NSA_DH, QCOLS), lambda b: (0, b, 0, 0)),
            pl.BlockSpec((kv, 1, nslc, Q_BLOCK), lambda b: (0, b, 0, 0)),
            pl.BlockSpec((kv, 1, 8, nsub), lambda b: (0, b, 0, 0)),
        ],
        scratch_shapes=[pltpu.VMEM((CMP_TILE, QCOLS), F32), pltpu.VMEM((nslc, kv * Q_BLOCK), F32)],
        compiler_params=_cparams(("arbitrary",)),
        name="nsa_cmp_topk",
    )(qt, kca, vo, slopes)


def _nsa_attn_kernel(cnt_ref, lst_ref, qt_ref, ksa_ref, vsta_ref, kwa_ref, vwta_ref, selb_ref, oc_ref, gl_ref,
                     slope_ref, o_ref, addlo_ref, addhi_ref, z_ref, m_ref, acc_ref, car_ref, part_ref, gate_ref):
    g = pl.program_id(0)
    qb = pl.program_id(1)
    nqb = pl.num_programs(1)
    nsub = ksa_ref.shape[1]
    slope = slope_ref[0]

    @pl.when(qb == 0)
    def _():
        cl = lax.broadcasted_iota(jnp.int32, (SUB, QCOLS), 0)
        q = lax.broadcasted_iota(jnp.int32, (SUB, QCOLS), 1) & (Q_BLOCK - 1)
        addlo_ref[...] = jnp.where(cl <= q, 0.0, NEG)
        addhi_ref[...] = jnp.where(cl > q, 0.0, NEG)

    t0 = qb * Q_BLOCK
    q = qt_ref[0, 0] * (NSA_DH ** -0.5)
    r8 =lax.broadcasted_iota(jnp.int32, (SEL_TILE_BLOCKS, QCOLS), 0)
    slope_rows = jnp.where(r8 < 2, slope, 0.0)
    zero_rows = jnp.zeros((KAUG - COL_SEL - 16, QCOLS), BF16)
    q_win = jnp.concatenate([q, jnp.concatenate([jnp.zeros_like(slope_rows), slope_rows], axis=0).astype(BF16),
                             zero_rows], axis=0)

    def q_sel(a):
        grp = a // (SEL_TILE_BLOCKS // SUB_BLOCKS)
        sb = selb_ref[0, 0, pl.ds(pl.multiple_of(grp * SEL_TILE_BLOCKS, SEL_TILE_BLOCKS), SEL_TILE_BLOCKS), :]
        sb = jnp.concatenate([sb] * NSA_REP, axis=1)
        return jnp.concatenate([q, jnp.concatenate([sb, slope_rows], axis=0).astype(BF16), zero_rows], axis=0)

    def colmax(s):
        return jnp.max(s, axis=0, keepdims=True)

    def weigh(v_ref, subs, slots, shifts):
        pv = None
        for a, slot, shift in zip(subs, slots, shifts):
            p = jnp.exp(z_ref[slot] - shift).astype(BF16)
            t = _dot(v_ref[0, a], p)
            pv = t if pv is None else pv + t
        return pv

    first = qb - WINDOW // Q_BLOCK
    near = [jnp.maximum(first + i, 0) for i in range(WIN_TILES)]
    gone = [jnp.where(first + i >= 0, 0.0, -NEG) for i in range(WIN_TILES)]
    offs_s, offs_w, mg, mw = [], [], None, None
    for i in range(WIN_TILES):
        w = _dot(kwa_ref[0, near[i]], q_win)
        if i == 0:
            w = w + addhi_ref[...]
        if i == WIN_TILES - 1:
            w = w + addlo_ref[...]
        z_ref[WIN_TILES + i] = w
        off_w = gone[i] - slope * float(i * SUB)
        cw = colmax(w) - off_w
        mw = cw if mw is None else jnp.maximum(mw, cw)
        offs_w.append(off_w)
    for i in range(WIN_TILES):
        s = _dot(ksa_ref[0, near[i]], q_sel(near[i]))
        if i == WIN_TILES - 1:
            s = s + addlo_ref[...]
        z_ref[i] = s
        off_s = slope * float((WIN_TILES - 1 - i) * SUB) + gone[i]
        cs = colmax(s) - off_s
        mg = cs if mg is None else jnp.maximum(mg, cs)
        offs_s.append(off_s)
    pvw = weigh(vwta_ref, near, range(WIN_TILES, 2 * WIN_TILES), [mw + o for o in offs_w])
    o_win = pvw[0:NSA_DH] / pvw[NSA_DH:NSA_DH + 1]
    gl_t = gl_ref[...].T
    gates = [jax.nn.sigmoid(jnp.concatenate(
        [gl_t[r * N_NSA_BRANCH + br:r * N_NSA_BRANCH + br + 1, :] for r in range(NSA_REP)], axis=1))
        for br in range(N_NSA_BRANCH)]
    part_ref[...] = gates[0] * oc_ref[0, 0] + gates[2] * o_win
    gate_ref[...] = gates[1]

    base = (g * nqb + qb) * nsub
    count = cnt_ref[g * nqb + qb]

    def far_score(gi, half, u):
        e = gi * FAR_GROUP + u
        a = lst_ref[base + e]
        s = _dot(ksa_ref[0, a], q_sel(a))
        z_ref[FAR_SLOT + half * FAR_GROUP + u] = s
        off = slope * (t0 - a * SUB).astype(F32) + jnp.where(e < count, 0.0, -NEG)
        return colmax(s) - off, off

    def far_scores(gi, half):
        offs, mg = [], None
        for u in range(FAR_GROUP):
            cand, off = far_score(gi, half, u)
            mg = cand if mg is None else jnp.maximum(mg, cand)
            offs.append(off)
        return mg, tuple(offs)

    def keep(half, mg, offs):
        car_ref[half, 0:1, :] = mg
        for u in range(FAR_GROUP):
            car_ref[half, u + 1:u + 2, :] = offs[u]

    def far_stage(gi, half):
        mg = car_ref[half, 0:1, :]
        offs = [car_ref[half, u + 1:u + 2, :] for u in range(FAR_GROUP)]
        m_old = m_ref[...]
        m_new = jnp.maximum(m_old, mg)
        pv, nmg, noffs = None, None, []
        for u in range(FAR_GROUP):
            cand, off = far_score(gi + 1, 1 - half, u)
            nmg = cand if nmg is None else jnp.maximum(nmg, cand)
            noffs.append(off)
            a = lst_ref[base + gi * FAR_GROUP + u]
            t = weigh(vsta_ref, [a], [FAR_SLOT + half * FAR_GROUP + u], [m_new + offs[u]])
            pv = t if pv is None else pv + t
        acc_ref[...] = jnp.exp(m_old - m_new) * acc_ref[...] + pv
        m_ref[...] = m_new
        keep(1 - half, nmg, noffs)

    ngroups = (count + FAR_GROUP - 1) // FAR_GROUP

    def far_pair(pi, carry):
        far_stage(2 * pi, 0)

        @pl.when(2 * pi + 1 < ngroups)
        def _():
            far_stage(2 * pi + 1, 1)

        return carry

    keep(0, *far_scores(0, 0))
    m_ref[...] = mg
    acc_ref[...] = weigh(vsta_ref, near, range(WIN_TILES), [mg + o for o in offs_s])
    lax.fori_loop(0, (ngroups + 1) // 2, far_pair, 0)
    acc = acc_ref[...]
    o_sel = acc[0:NSA_DH] / acc[NSA_DH:NSA_DH + 1]

    out = part_ref[...] + gate_ref[...] * o_sel
    out_t = out.T
    heads = [out_t[r * Q_BLOCK:(r + 1) * Q_BLOCK, :] for r in range(NSA_REP)]
    o_ref[...] = jnp.concatenate(heads, axis=1).astype(o_ref.dtype)


def _nsa_attn(far_count, far_list, qt, ksa, vsta, kwa, vwta, selb, oc, gl, slopes):
    kv, nqb = qt.shape[0], qt.shape[1]
    nsub = ksa.shape[1]
    nslc = selb.shape[2]
    blk = lambda g, b, c, l: (g, b, 0, 0)
    grp = lambda g, b, c, l: (g, 0, 0, 0)
    grid_spec = pltpu.PrefetchScalarGridSpec(
        num_scalar_prefetch=2,
        grid=(kv, nqb),
        in_specs=[
            pl.BlockSpec((1, 1, NSA_DH, QCOLS), blk),
            pl.BlockSpec((1, nsub, SUB, KAUG), grp),
            pl.BlockSpec((1, nsub, VAUG, SUB), grp),
            pl.BlockSpec((1, nsub, SUB, KAUG), grp),
            pl.BlockSpec((1, nsub, VAUG, SUB), grp),
            pl.BlockSpec((1, 1, nslc, Q_BLOCK), blk),
            pl.BlockSpec((1, 1, NSA_DH, QCOLS), blk),
            pl.BlockSpec((Q_BLOCK, KAUG), lambda g, b, c, l: (b, g)),
            pl.BlockSpec((1, 1, QCOLS), lambda g, b, c, l: (g, 0, 0)),
        ],
        out_specs=pl.BlockSpec((Q_BLOCK, NSA_REP * NSA_DH), lambda g, b, c, l: (b, g)),
        scratch_shapes=[
            pltpu.VMEM((SUB, QCOLS), F32),
            pltpu.VMEM((SUB, QCOLS), F32),
            pltpu.VMEM((FAR_SLOT + 2 * FAR_GROUP, SUB, QCOLS), F32),
            pltpu.VMEM((1, QCOLS), F32),
            pltpu.VMEM((VAUG, QCOLS), F32),
            pltpu.VMEM((2, 8, QCOLS), F32),
            pltpu.VMEM((NSA_DH, QCOLS), F32),
            pltpu.VMEM((1, QCOLS), F32),
        ],
    )
    return pl.pallas_call(
        _nsa_attn_kernel,
        out_shape=jax.ShapeDtypeStruct((nsub * SUB, NSA_HEADS * NSA_DH), BF16),
        grid_spec=grid_spec,
        compiler_params=_cparams(("arbitrary", "arbitrary")),
        name="nsa_attn",
    )(far_count, far_list, qt, ksa, vsta, kwa, vwta, selb, oc, gl, slopes)


def _far_lists(picks, nqb):
    nsub = picks.shape[-1]
    a = jnp.arange(nsub)[None, None, :]
    far = a < (jnp.arange(nqb)[None, :, None] - WINDOW // Q_BLOCK)
    active = (picks > 0) & far
    count = jnp.sum(active, axis=-1).astype(jnp.int32)
    order = jnp.argsort(jnp.where(active, 0, 1), axis=-1, stable=True).astype(jnp.int32)
    lst = jnp.where(a < count[..., None], order, 0).reshape(-1)
    return count.reshape(-1), jnp.concatenate([lst, jnp.zeros((2 * FAR_GROUP,), jnp.int32)])


def _merge_kernel(x_ref, g_ref, ya_ref, yb_ref, yc_ref, wg_ref, bg_ref, wb_ref, wo_ref, o_ref):
    x = x_ref[...]
    h = _rms(x, g_ref[...]).astype(BF16)
    mix = None
    for mi, y_ref in enumerate((ya_ref, yb_ref, yc_ref)):
        cs = slice(mi * D_MODEL, (mi + 1) * D_MODEL)
        gate = jax.nn.sigmoid(_dot(h, wg_ref[:, cs]) + bg_ref[:, cs])
        term = gate * _dot(y_ref[...], wb_ref[mi])
        mix = term if mix is None else mix + term
    o_ref[...] = x + _dot(mix.astype(BF16), wo_ref[...])


def _merge(x, g, ya, yb, yc, wg, bg, wb, wo, layer, tm=512):
    s = x.shape[0]
    row = lambda i: (i, 0)
    fixed2 = lambda i: (0, 0)
    return pl.pallas_call(
        _merge_kernel,
        out_shape=jax.ShapeDtypeStruct((s, D_MODEL), F32),
        grid=(s // tm,),
        in_specs=[
            pl.BlockSpec((tm, D_MODEL), row),
            pl.BlockSpec((1, D_MODEL), fixed2),
            pl.BlockSpec((tm, MIX_WIDTH), row),
            pl.BlockSpec((tm, MIX_WIDTH), row),
            pl.BlockSpec((tm, MIX_WIDTH), row),
            pl.BlockSpec((None, D_MODEL, N_MIXERS * D_MODEL), lambda i: (layer, 0, 0), pipeline_mode=pl.Buffered(1)),
            pl.BlockSpec((1, N_MIXERS * D_MODEL), fixed2),
            pl.BlockSpec((None, N_MIXERS, MIX_WIDTH, D_MODEL), lambda i: (layer, 0, 0, 0), pipeline_mode=pl.Buffered(1)),
            pl.BlockSpec((None, D_MODEL, D_MODEL), lambda i: (layer, 0, 0), pipeline_mode=pl.Buffered(1)),
        ],
        out_specs=pl.BlockSpec((tm, D_MODEL), row),
        compiler_params=_cparams(("arbitrary",)),
        name="merge",
    )(x, g, ya, yb, yc, wg, bg, wb, wo)


def _alibi_slope_cols():
    h = jnp.arange(1, NSA_HEADS + 1, dtype=F32)
    slopes = (2.0 ** (-8.0 * h / NSA_HEADS)).reshape(NSA_KV, NSA_REP)
    return jnp.repeat(slopes, Q_BLOCK, axis=1).reshape(NSA_KV, 1, QCOLS)


def _overlap_local():
    cl = jnp.arange(CMP_TILE)[None, :]
    jl = jnp.arange(2 * CMP_TILE_SLC)[:, None]
    ov = (cl * CMP_STRIDE < (jl + 1) * SLC_BLOCK) & (cl * CMP_STRIDE + CMP_LEN > jl * SLC_BLOCK)
    return ov.astype(BF16)


def _nsa(pcv, pgl, ksa, kwa, qt, vsta, vwta, cmp_k, cmp_v, layer):
    s = pcv.shape[0]
    nqb = s // Q_BLOCK
    slopes = _alibi_slope_cols()
    kca = _compress(pcv, *cmp_k, layer, False)
    vcta = _compress(pcv, *cmp_v, layer, True)
    oc, selb, picks = _nsa_cmp(qt, kca, vcta, _overlap_local(), slopes, s // SLC_BLOCK)
    far_count, far_list = _far_lists(picks[:, :, 0, :], nqb)
    ksa = ksa.reshape(NSA_KV, s // SUB, SUB, KAUG)
    kwa = kwa.reshape(NSA_KV, s // SUB, SUB, KAUG)
    return _nsa_attn(far_count, far_list, qt, ksa, vsta, kwa, vwta, selb, oc, pgl, slopes)


def kernel(x, ffn1_norm, ffn1_w1, ffn1_w2, mix_norm, w_in, gm_ln_g, gm_ln_b, gm_ws, gm_bs, ret_gn_g, ret_gn_b,
           cmp_pos, cmp_w1, cmp_w2, w_branch_out, w_merge_gate, b_merge_gate, w_o, ffn2_norm, ffn2_w1, ffn2_w2,
           final_norm):
    bsz, s, d = x.shape
    depth = ffn1_w1.shape[0]
    assert d == D_MODEL and s % CMP_TILE_TOKENS == 0 and (s // 512) % depth == 0, x.shape
    row = lambda v: v.reshape(1, -1)
    fin = row(final_norm)
    f1w1, f1w2 = _to_bf16(ffn1_w1), _to_bf16(ffn1_w2)
    later = (ffn2_w1, ffn2_w2, jnp.swapaxes(w_in, 1, 2), w_merge_gate,
             w_branch_out.reshape(depth, N_MIXERS * MIX_WIDTH, D_MODEL), w_o)
    converted = None
    kcs, kcw, vones = _proj_constants()
    cmp_k = _compress_weights(cmp_pos[:, 0], cmp_w1[:, 0], cmp_w2[:, 0], False)
    cmp_v = _compress_weights(cmp_pos[:, 1], cmp_w1[:, 1], cmp_w2[:, 1], True)
    outs = []
    for b in range(bsz):
        xb = x.reshape(s, D_MODEL) if bsz == 1 else x[b]
        for l in range(depth):
            if converted is None:
                xb, *converted = _ffn(xb, row(ffn1_norm[l]), f1w1, f1w2, l, fin, False, cast=later)
                f2w1, f2w2, w_in_b, wg_b, wb_b, wo_b = converted
                wb_b = wb_b.reshape(w_branch_out.shape)
                wn, wt = _proj_weights(w_in_b)
            else:
                xb = _ffn(xb, row(ffn1_norm[l]), f1w1, f1w2, l, fin, False)
            pa, pcv, pgl, ksa, kwa, qt, vsta, vwta = _proj(xb, row(mix_norm[l]), wn, wt, l, kcs, kcw, vones)
            y_a = _gmlp(pa, row(gm_ln_g[l]), row(gm_ln_b[l]), gm_ws[l], gm_bs[l].T)
            y_b = _retention(pa, row(ret_gn_g[l]), row(ret_gn_b[l]))
            y_c = _nsa(pcv, pgl, ksa, kwa, qt, vsta, vwta, cmp_k, cmp_v, l)
            xb = _merge(xb, row(mix_norm[l]), y_a, y_b, y_c, wg_b, row(b_merge_gate[l]), wb_b, wo_b, l)
            xb = _ffn(xb, row(ffn2_norm[l]), f2w1, f2w2, l, fin, l == depth - 1)
        outs.append(xb)
    return outs[0].reshape(1, s, D_MODEL) if bsz == 1 else jnp.stack(outs)
```

```python
import functools
import math

import jax
import jax.numpy as jnp
from jax import lax
from jax.experimental import pallas as pl
from jax.experimental.pallas import tpu as pltpu

F32 = jnp.float32
BF16 = jnp.bfloat16

D_MODEL = 1024
D_FF = 2816
NORM_EPS = 1e-6
FFN_RES_WEIGHT = 0.5
GM_WIDTH = 512
GM_GROUPS = 4
GM_CHUNK = 128
RET_HEADS = 4
RET_DK = 64
RET_DV = 128
RET_CHUNK = 128
NSA_HEADS = 8
NSA_KV = 2
NSA_REP = NSA_HEADS // NSA_KV
NSA_DH = 64
CMP_LEN = 32
CMP_STRIDE = 16
CMP_HIDDEN = 128
SLC_BLOCK = 64
N_SELECT = 16
WINDOW = 512
Q_BLOCK = 128
N_MIXERS = 3
MIX_WIDTH = 512
BIG = 1e9
NEG = -1e30
PICKED = -(2.0 ** 127)

OFF_GM_U, OFF_GM_V, OFF_RQ, OFF_RK, OFF_RV, OFF_RG, OFF_NQ, OFF_NKV, OFF_NG = (
    0, 512, 1024, 1280, 1536, 2048, 2560, 3072, 3840)

QCOLS = NSA_REP * Q_BLOCK
SEL_TILE = 512
SEL_TILE_BLOCKS = SEL_TILE // SLC_BLOCK
SUB = 128
SUB_BLOCKS = SUB // SLC_BLOCK
FAR_GROUP = 4
KAUG = 128
COL_SEL = NSA_DH
COL_POS = NSA_DH + SEL_TILE_BLOCKS
VAUG = 80
WIN_TILES = (WINDOW + Q_BLOCK) // SUB
FAR_SLOT = 2 * WIN_TILES
CMP_TILE = 128
CMP_TILE_TOKENS = CMP_TILE * CMP_STRIDE
CMP_TILE_SLC = CMP_TILE_TOKENS // SLC_BLOCK
CMP_VARIANT_STEP = 1
N_NSA_BRANCH = 3
KVW = NSA_KV * NSA_DH
PA_GM_U, PA_GM_V, PA_RV, PA_RG, PA_RQ = 0, 512, 1024, 1536, 2048
PA_WIDTH = PA_RQ + RET_HEADS * RET_DK
WN_CV = PA_WIDTH
WN_GL = WN_CV + 2 * KVW
WN_KS = WN_GL + NSA_KV * KAUG
WN_WIDTH = WN_KS + 2 * NSA_KV * KAUG
WT_RK = NSA_HEADS * NSA_DH + 2 * KVW
WT_ROWS = WT_RK + RET_HEADS * RET_DK
VMEM_LIMIT = 56 * 1024 * 1024
CAST_BLOCK_BYTES = 4 * 1024 * 1024


def _cparams(sem, vmem=VMEM_LIMIT):
    return pltpu.CompilerParams(dimension_semantics=sem, vmem_limit_bytes=vmem)


def _cast_kernel(w_ref, o_ref):
    o_ref[...] = w_ref[...].astype(o_ref.dtype)


def _to_bf16(w, layers):
    _, r, c = w.shape
    nl = layers
    tr = 16
    while r % (2 * tr) == 0 and 2 * tr * c * 4 <= CAST_BLOCK_BYTES:
        tr *= 2
    return pl.pallas_call(
        _cast_kernel,
        out_shape=jax.ShapeDtypeStruct((nl, r, c), BF16),
        grid=(nl, r // tr),
        in_specs=[pl.BlockSpec((None, tr, c), lambda l, i: (l, i, 0))],
        out_specs=pl.BlockSpec((None, tr, c), lambda l, i: (l, i, 0)),
        compiler_params=_cparams(("arbitrary", "arbitrary")),
        name="cast_bf16",
    )(w)


def _cast_along(nsteps, w):
    nl, r, c = w.shape
    per_layer = nsteps // nl
    tr = -(-r // (16 * per_layer)) * 16
    assert nl * per_layer == nsteps and tr * (per_layer - 1) < r, (w.shape, nsteps)
    return pl.BlockSpec((None, tr, c), lambda i: (i // per_layer, i % per_layer, 0))


def _rms(x, g):
    return x * lax.rsqrt(jnp.mean(x * x, axis=-1, keepdims=True) + NORM_EPS) * g


def _dot(a, b):
    return jnp.dot(a, b, preferred_element_type=F32)


def _ffn_kernel(x_ref, g_ref, w1_ref, w2_ref, fin_ref, *rest, final_norm, tf, ncast):
    cast_in, o_ref, cast_out, act_ref = rest[:ncast], rest[ncast], rest[ncast + 1:2 * ncast + 1], rest[-1]
    for src, dst in zip(cast_in, cast_out):
        dst[...] = src[...].astype(dst.dtype)
    x = x_ref[...]
    h = _rms(x, g_ref[...]).astype(BF16)
    for f in range(D_FF // tf):
        a = _dot(h, w1_ref[:, f * tf:(f + 1) * tf])
        b = _dot(h, w1_ref[:, D_FF + f * tf:D_FF + (f + 1) * tf])
        act_ref[:, f * tf:(f + 1) * tf] = (jax.nn.silu(a) * b).astype(BF16)
    y = x + FFN_RES_WEIGHT * _dot(act_ref[...], w2_ref[...])
    if final_norm:
        y = _rms(y, fin_ref[...])
    o_ref[...] = y


def _ffn(x, g, w1, w2, layer, fin, final_norm, cast=(), tm=512, tf=256):
    s = x.shape[0]
    nsteps = s // tm
    fixed = lambda i: (0, 0)
    resident = pl.Buffered(1)
    cast_specs = [_cast_along(nsteps, w) for w in cast]
    outs = pl.pallas_call(
        functools.partial(_ffn_kernel, final_norm=final_norm, tf=tf, ncast=len(cast)),
        out_shape=[jax.ShapeDtypeStruct((s, D_MODEL), F32)] + [jax.ShapeDtypeStruct(w.shape, BF16) for w in cast],
        grid=(nsteps,),
        in_specs=[
            pl.BlockSpec((tm, D_MODEL), lambda i: (i, 0)),
            pl.BlockSpec((1, D_MODEL), fixed),
            pl.BlockSpec((None, D_MODEL, 2 * D_FF), lambda i: (layer, 0, 0), pipeline_mode=resident),
            pl.BlockSpec((None, D_FF, D_MODEL), lambda i: (layer, 0, 0), pipeline_mode=resident),
            pl.BlockSpec((1, D_MODEL), fixed),
        ] + cast_specs,
        out_specs=[pl.BlockSpec((tm, D_MODEL), lambda i: (i, 0))] + cast_specs,
        scratch_shapes=[pltpu.VMEM((tm, D_FF), BF16)],
        compiler_params=_cparams(("arbitrary",)),
        name="ffn",
    )(x, g, w1, w2, fin, *cast)
    return outs[0] if not cast else tuple(outs)


def _proj_kernel(x_ref, g_ref, wn_ref, wt_ref, kcs_ref, kcw_ref, vones_ref,
                 pa_ref, pcv_ref, pgl_ref, ksa_ref, kwa_ref, qt_ref, vsta_ref, vwta_ref, rkt_ref):
    tm = x_ref.shape[0]
    h = _rms(x_ref[...], g_ref[...]).astype(BF16)
    step = 256
    for c in range(0, PA_WIDTH, step):
        pa_ref[:, c:c + step] = _dot(h, wn_ref[:, c:c + step])
    pcv_ref[...] = _dot(h, wn_ref[:, WN_CV:WN_CV + 2 * KVW])
    pgl_ref[...] = _dot(h, wn_ref[:, WN_GL:WN_GL + 2 * KAUG])
    kk = _dot(h, wn_ref[:, WN_KS:WN_KS + 4 * KAUG])
    for g in range(NSA_KV):
        ksa_ref[g] = (kk[:, g * KAUG:(g + 1) * KAUG] + kcs_ref[...]).astype(BF16)
        kwa_ref[g] = (kk[:, (NSA_KV + g) * KAUG:(NSA_KV + g + 1) * KAUG] + kcw_ref[...]).astype(BF16)
    pt = lax.dot_general(wt_ref[...], h, (((1,), (1,)), ((), ())), preferred_element_type=F32)
    nq = NSA_HEADS * NSA_DH
    for b in range(tm // RET_CHUNK):
        rkt_ref[b] = pt[WT_RK:WT_ROWS, b * RET_CHUNK:(b + 1) * RET_CHUNK]
    for g in range(NSA_KV):
        for b in range(tm // Q_BLOCK):
            for r in range(NSA_REP):
                hd = g * NSA_REP + r
                qt_ref[g, b, :, r * Q_BLOCK:(r + 1) * Q_BLOCK] = (
                    pt[hd * NSA_DH:(hd + 1) * NSA_DH, b * Q_BLOCK:(b + 1) * Q_BLOCK].astype(BF16))
            for v_ref, first_row in ((vsta_ref, nq), (vwta_ref, nq + KVW)):
                rows = slice(first_row + g * NSA_DH, first_row + (g + 1) * NSA_DH)
                v_ref[g, b, 0:NSA_DH, :] = pt[rows, b * SUB:(b + 1) * SUB].astype(BF16)
                v_ref[g, b, NSA_DH:VAUG, :] = vones_ref[...]


def _proj(x, g, wn, wt, layer, kcs, kcw, vones, tm=512):
    s = x.shape[0]
    fixed = lambda i: (0, 0)
    resident = pl.Buffered(1)
    return pl.pallas_call(
        _proj_kernel,
        out_shape=[
            jax.ShapeDtypeStruct((s, PA_WIDTH), F32),
            jax.ShapeDtypeStruct((s, 2 * KVW), F32),
            jax.ShapeDtypeStruct((s, 2 * KAUG), F32),
            jax.ShapeDtypeStruct((NSA_KV, s, KAUG), BF16),
            jax.ShapeDtypeStruct((NSA_KV, s, KAUG), BF16),
            jax.ShapeDtypeStruct((NSA_KV, s // Q_BLOCK, NSA_DH, QCOLS), BF16),
            jax.ShapeDtypeStruct((NSA_KV, s // SUB, VAUG, SUB), BF16),
            jax.ShapeDtypeStruct((NSA_KV, s // SUB, VAUG, SUB), BF16),
            jax.ShapeDtypeStruct((s // RET_CHUNK, RET_HEADS * RET_DK, RET_CHUNK), F32),
        ],
        grid=(s // tm,),
        in_specs=[
            pl.BlockSpec((tm, D_MODEL), lambda i: (i, 0)),
            pl.BlockSpec((1, D_MODEL), fixed),
            pl.BlockSpec((None, D_MODEL, WN_WIDTH), lambda i: (layer, 0, 0), pipeline_mode=resident),
            pl.BlockSpec((None, WT_ROWS, D_MODEL), lambda i: (layer, 0, 0), pipeline_mode=resident),
            pl.BlockSpec((tm, KAUG), fixed),
            pl.BlockSpec((tm, KAUG), fixed),
            pl.BlockSpec((VAUG - NSA_DH, SUB), fixed),
        ],
        out_specs=[
            pl.BlockSpec((tm, PA_WIDTH), lambda i: (i, 0)),
            pl.BlockSpec((tm, 2 * KVW), lambda i: (i, 0)),
            pl.BlockSpec((tm, 2 * KAUG), lambda i: (i, 0)),
            pl.BlockSpec((NSA_KV, tm, KAUG), lambda i: (0, i, 0)),
            pl.BlockSpec((NSA_KV, tm, KAUG), lambda i: (0, i, 0)),
            pl.BlockSpec((NSA_KV, tm // Q_BLOCK, NSA_DH, QCOLS), lambda i: (0, i, 0, 0)),
            pl.BlockSpec((NSA_KV, tm // SUB, VAUG, SUB), lambda i: (0, i, 0, 0)),
            pl.BlockSpec((NSA_KV, tm // SUB, VAUG, SUB), lambda i: (0, i, 0, 0)),
            pl.BlockSpec((tm // RET_CHUNK, RET_HEADS * RET_DK, RET_CHUNK), lambda i: (i, 0, 0)),
        ],
        compiler_params=_cparams(("arbitrary",)),
        name="proj",
    )(x, g, wn, wt, kcs, kcw, vones)


def _proj_weights(w_in_t):
    w = w_in_t
    z = lambda n: jnp.zeros((w.shape[0], n, D_MODEL), BF16)
    kv = lambda i, g: w[:, OFF_NKV + i * KVW + g * NSA_DH:OFF_NKV + i * KVW + (g + 1) * NSA_DH]
    ngl = N_NSA_BRANCH * NSA_REP
    rows = [w[:, OFF_GM_U:OFF_RQ], w[:, OFF_RV:OFF_NQ], w[:, OFF_RQ:OFF_RK],
            w[:, OFF_NKV:OFF_NKV + 2 * KVW]]
    for g in range(NSA_KV):
        rows += [w[:, OFF_NG + g * ngl:OFF_NG + (g + 1) * ngl], z(KAUG - ngl)]
    for i in (2, 4):
        for g in range(NSA_KV):
            rows += [kv(i, g), z(KAUG - NSA_DH)]
    wn = jnp.swapaxes(jnp.concatenate(rows, axis=1), 1, 2)
    wt = jnp.concatenate([w[:, OFF_NQ:OFF_NQ + NSA_HEADS * NSA_DH], kv(3, 0), kv(3, 1), kv(5, 0), kv(5, 1),
                          w[:, OFF_RK:OFF_RV]], axis=1)
    return wn, wt


def _proj_constants(tm=512):
    pos = jnp.arange(tm)
    zero = jnp.zeros((tm, NSA_DH), F32)

    def pos_cols(p):
        return jnp.stack([p // 16 * 16, p % 16], axis=1).astype(F32)

    onehot = (pos[:, None] // SLC_BLOCK % SEL_TILE_BLOCKS == jnp.arange(SEL_TILE_BLOCKS)[None, :]).astype(F32)
    tail = jnp.zeros((tm, KAUG - COL_POS - 2), F32)
    kcs = jnp.concatenate([zero, onehot, pos_cols(pos % SUB), tail], axis=1)
    kcw = jnp.concatenate([zero, jnp.zeros_like(onehot), pos_cols(pos % SUB), tail], axis=1)
    r = jnp.arange(VAUG - NSA_DH)[:, None]
    vones = jnp.broadcast_to(jnp.where(r == 0, 1.0, 0.0), (VAUG - NSA_DH, SUB)).astype(BF16)
    return kcs, kcw, vones


def _gmlp_kernel(u_ref, v_ref, lg_ref, lb_ref, ws_ref, bst_ref, o_ref, *, chunks):
    cg = GM_WIDTH // GM_GROUPS
    u = jax.nn.gelu(u_ref[...])
    v = jax.nn.gelu(v_ref[...])
    mu = jnp.mean(v, axis=-1, keepdims=True)
    vc = v - mu
    var = jnp.mean(vc * vc, axis=-1, keepdims=True)
    vn = (vc * lax.rsqrt(var + NORM_EPS) * lg_ref[...] + lb_ref[...]).astype(BF16)
    row = lax.broadcasted_iota(jnp.int32, (GM_CHUNK, GM_CHUNK), 0)
    col = lax.broadcasted_iota(jnp.int32, (GM_CHUNK, GM_CHUNK), 1)
    causal = row >= col
    bst = bst_ref[...]
    for gi in range(GM_GROUPS):
        w = jnp.where(causal, ws_ref[gi], 0.0).astype(BF16)
        bias = bst[:, gi:gi + 1]
        for c in range(chunks):
            rs = slice(c * GM_CHUNK, (c + 1) * GM_CHUNK)
            cs = slice(gi * cg, (gi + 1) * cg)
            sv = _dot(w, vn[rs, cs]) + bias
            o_ref[rs, cs] = (u[rs, cs] * sv).astype(o_ref.dtype)


def _gmlp(proj, ln_g, ln_b, ws, bst, tm=512):
    s = proj.shape[0]
    return pl.pallas_call(
        functools.partial(_gmlp_kernel, chunks=tm // GM_CHUNK),
        out_shape=jax.ShapeDtypeStruct((s, GM_WIDTH), BF16),
        grid=(s // tm,),
        in_specs=[
            pl.BlockSpec((tm, GM_WIDTH), lambda i: (i, PA_GM_U // GM_WIDTH)),
            pl.BlockSpec((tm, GM_WIDTH), lambda i: (i, PA_GM_V // GM_WIDTH)),
            pl.BlockSpec((1, GM_WIDTH), lambda i: (0, 0)),
            pl.BlockSpec((1, GM_WIDTH), lambda i: (0, 0)),
            pl.BlockSpec((GM_GROUPS, GM_CHUNK, GM_CHUNK), lambda i: (0, 0, 0)),
            pl.BlockSpec((GM_CHUNK, GM_GROUPS), lambda i: (0, 0)),
        ],
        out_specs=pl.BlockSpec((tm, GM_WIDTH), lambda i: (i, 0)),
        compiler_params=_cparams(("arbitrary",)),
        name="gmlp",
    )(proj, proj, ln_g, ln_b, ws, bst)


def _ret_kernel(q_ref, kt_ref, v_ref, g_ref, gg_ref, gb_ref, o_ref, st_ref, *, chunks):
    @pl.when(pl.program_id(0) == 0)
    def _():
        st_ref[...] = jnp.zeros_like(st_ref)

    c = RET_CHUNK
    ri = lax.broadcasted_iota(jnp.int32, (c, c), 0)
    ci = lax.broadcasted_iota(jnp.int32, (c, c), 1)
    diff = (ri - ci).astype(F32)
    pos_k = lax.broadcasted_iota(jnp.int32, (RET_DK, c), 1).astype(F32)
    pos_v = lax.broadcasted_iota(jnp.int32, (c, RET_DV), 0).astype(F32)
    for h in range(RET_HEADS):
        log_gamma = math.log(1.0 - 2.0 ** (-5.0 - h))
        intra_decay = jnp.where(diff >= 0, jnp.exp(log_gamma * jnp.maximum(diff, 0.0)), 0.0)
        k_decay = jnp.exp(log_gamma * (c - 1.0 - pos_k))
        q_decay = jnp.exp(log_gamma * (pos_v + 1.0))
        chunk_decay = math.exp(log_gamma * c)
        gam = gg_ref[:, h * RET_DV:(h + 1) * RET_DV]
        bet = gb_ref[:, h * RET_DV:(h + 1) * RET_DV]
        for n in range(chunks):
            rs = slice(n * c, (n + 1) * c)
            q = q_ref[rs, h * RET_DK:(h + 1) * RET_DK].astype(BF16)
            kt = kt_ref[n, h * RET_DK:(h + 1) * RET_DK, :] * (RET_DK ** -0.5)
            v = v_ref[rs, h * RET_DV:(h + 1) * RET_DV].astype(BF16)
            scores = _dot(q, kt.astype(BF16)) * intra_decay
            intra = _dot(scores.astype(BF16), v)
            state = st_ref[h]
            cross = _dot(q, state.astype(BF16)) * q_decay
            kv = _dot((kt * k_decay).astype(BF16), v)
            st_ref[h] = state * chunk_decay + kv
            y = intra + cross
            mu = jnp.mean(y, axis=-1, keepdims=True)
            yc = y - mu
            var = jnp.mean(yc * yc, axis=-1, keepdims=True)
            yn = yc * lax.rsqrt(var + NORM_EPS) * gam + bet
            gate = g_ref[rs, h * RET_DV:(h + 1) * RET_DV]
            o_ref[rs, h * RET_DV:(h + 1) * RET_DV] = (jax.nn.silu(gate) * yn).astype(o_ref.dtype)


def _retention(proj, rkt, gn_g, gn_b, tm=512):
    s = proj.shape[0]
    hk = RET_HEADS * RET_DK
    hv = RET_HEADS * RET_DV
    return pl.pallas_call(
        functools.partial(_ret_kernel, chunks=tm // RET_CHUNK),
        out_shape=jax.ShapeDtypeStruct((s, hv), BF16),
        grid=(s // tm,),
        in_specs=[
            pl.BlockSpec((tm, hk), lambda i: (i, PA_RQ // hk)),
            pl.BlockSpec((tm // RET_CHUNK, hk, RET_CHUNK), lambda i: (i, 0, 0)),
            pl.BlockSpec((tm, hv), lambda i: (i, PA_RV // hv)),
            pl.BlockSpec((tm, hv), lambda i: (i, PA_RG // hv)),
            pl.BlockSpec((1, hv), lambda i: (0, 0)),
            pl.BlockSpec((1, hv), lambda i: (0, 0)),
        ],
        out_specs=pl.BlockSpec((tm, hv), lambda i: (i, 0)),
        scratch_shapes=[pltpu.VMEM((RET_HEADS, RET_DK, RET_DV), F32)],
        compiler_params=_cparams(("arbitrary",)),
        name="retention",
    )(proj, rkt, proj, proj, gn_g, gn_b)


def _compress_kernel(t_ref, pos_ref, w1_ref, w2_ref, o_ref, xlo_ref, xhi_ref, *, values):
    n = t_ref.shape[0] // CMP_STRIDE
    for i in range(CMP_STRIDE):
        x = t_ref[pl.ds(i, n, stride=CMP_STRIDE), :]
        xlo_ref[:, i * KVW:(i + 1) * KVW] = (x + pos_ref[0:1, i * KVW:(i + 1) * KVW]).astype(BF16)
        xhi_ref[:, i * KVW:(i + 1) * KVW] = (x + pos_ref[1:2, i * KVW:(i + 1) * KVW]).astype(BF16)
    lo = _dot(xlo_ref[...], w1_ref[0])
    hi = _dot(xhi_ref[...], w1_ref[1])
    hi_next = jnp.concatenate([hi[1:], hi[:1]], axis=0)
    rowi = lax.broadcasted_iota(jnp.int32, lo.shape, 0)
    hid = jnp.where(rowi < n - 1, jax.nn.gelu(lo + hi_next), 0.0).astype(BF16)
    for g in range(NSA_KV):
        if values:
            vt = lax.dot_general(w2_ref[g], hid, (((1,), (1,)), ((), ())), preferred_element_type=F32)
            r = lax.broadcasted_iota(jnp.int32, vt.shape, 0)
            vt = jnp.where(r == NSA_DH, 1.0, vt).astype(BF16)
            for c in range(n // CMP_TILE):
                o_ref[g, c] = vt[:, c * CMP_TILE:(c + 1) * CMP_TILE]
        else:
            k = _dot(hid, w2_ref[g])
            col = lax.broadcasted_iota(jnp.int32, k.shape, 1)
            blk = (lax.broadcasted_iota(jnp.int32, k.shape, 0) & (CMP_TILE - 1)) * CMP_STRIDE
            k = jnp.where(col == NSA_DH, blk.astype(F32), k).astype(BF16)
            for c in range(n // CMP_TILE):
                o_ref[g, c] = k[c * CMP_TILE:(c + 1) * CMP_TILE, :]


def _compress(pcv, posrows, w1, w2, layer, values):
    s = pcv.shape[0]
    n = s // CMP_STRIDE
    nct = n // CMP_TILE
    out_tile = (VAUG, CMP_TILE) if values else (CMP_TILE, KAUG)
    width = CMP_STRIDE * KVW
    return pl.pallas_call(
        functools.partial(_compress_kernel, values=values),
        out_shape=jax.ShapeDtypeStruct((NSA_KV, nct) + out_tile, BF16),
        grid=(1,),
        in_specs=[
            pl.BlockSpec((s, KVW), lambda i: (0, 1 if values else 0)),
            pl.BlockSpec((None, 2, width), lambda i: (layer, 0, 0)),
            pl.BlockSpec((None, 2, width, NSA_KV * CMP_HIDDEN), lambda i: (layer, 0, 0, 0)),
            pl.BlockSpec((None, NSA_KV) + w2.shape[2:], lambda i: (layer, 0, 0, 0)),
        ],
        out_specs=pl.BlockSpec((NSA_KV, nct) + out_tile, lambda i: (0, 0, 0, 0)),
        scratch_shapes=[pltpu.VMEM((n, width), BF16), pltpu.VMEM((n, width), BF16)],
        compiler_params=_cparams(("arbitrary",)),
        name="nsa_compress_v" if values else "nsa_compress_k",
    )(pcv, posrows, w1, w2)


def _compress_weights(pos, w1, w2, values):
    nl = pos.shape[0]
    posrows = jnp.tile(pos.reshape(nl, 2, CMP_STRIDE, 1, NSA_DH), (1, 1, 1, NSA_KV, 1)).reshape(nl, 2, CMP_STRIDE * KVW)
    w = w1.astype(BF16).reshape(nl, 2, CMP_STRIDE, NSA_DH, CMP_HIDDEN)
    eye = jnp.eye(NSA_KV, dtype=BF16)
    w1b = jnp.einsum('lhidc,gk->lhigdkc', w, eye).reshape(nl, 2, CMP_STRIDE * KVW, NSA_KV * CMP_HIDDEN)
    w2g = jnp.einsum('lcd,gk->lgkcd', w2.astype(BF16), eye).reshape(nl, NSA_KV, NSA_KV * CMP_HIDDEN, NSA_DH)
    if values:
        w2g = jnp.pad(w2g.transpose(0, 1, 3, 2), ((0, 0), (0, 0), (0, VAUG - NSA_DH), (0, 0)))
    else:
        w2g = jnp.pad(w2g, ((0, 0), (0, 0), (0, 0), (0, KAUG - NSA_DH)))
    return posrows, w1b, w2g


def _nsa_cmp_kernel(qt_ref, kca_ref, vo_ref, slope_ref, oc_ref, selb_ref, any_ref, dmask_ref, jf_ref):
    qb = pl.program_id(0)
    nct = kca_ref.shape[1]
    nslc = selb_ref.shape[2]
    nsub = nslc // SUB_BLOCKS

    @pl.when(qb == 0)
    def _():
        cl = lax.broadcasted_iota(jnp.int32, (CMP_TILE, QCOLS), 0)
        q = lax.broadcasted_iota(jnp.int32, (CMP_TILE, QCOLS), 1) & (Q_BLOCK - 1)
        dmask_ref[...] = (cl * CMP_STRIDE - q).astype(F32)
        jf_ref[...] = lax.broadcasted_iota(jnp.int32, jf_ref.shape, 0).astype(F32)

    t0 = qb * Q_BLOCK
    r16 = lax.broadcasted_iota(jnp.int32, (16, QCOLS), 0)
    td = t0 // CMP_TILE_TOKENS

    def colmax(s):
        return jnp.max(s, axis=0, keepdims=True)

    def tile_scores(nt, g):
        slope = slope_ref[g]
        qa = jnp.concatenate([qt_ref[g, 0] * (NSA_DH ** -0.5), jnp.where(r16 == 0, slope, 0.0).astype(BF16),
                              jnp.zeros((KAUG - NSA_DH - 16, QCOLS), BF16)], axis=0)
        offs = [slope * (t0 - ti * CMP_TILE_TOKENS).astype(F32) for ti in range(nt)]
        zs, m = [], None
        for ti in range(nt):
            s = _dot(kca_ref[g, ti], qa)
            if ti >= nt - CMP_VARIANT_STEP - 1:
                lim = (t0 - ti * CMP_TILE_TOKENS - (CMP_LEN - 1)).astype(F32)
                s = jnp.where(dmask_ref[...] <= lim, s, NEG)
            zs.append(s)
            cand = colmax(s) - offs[ti]
            m = cand if m is None else jnp.maximum(m, cand)
        return zs, m, offs

    def importance(nt, g, zs, m, offs):
        acc, pieces = None, []
        for ti in range(nt):
            p = jnp.exp(zs[ti] - (m + offs[ti])).astype(BF16)
            t = _dot(vo_ref[g, ti], p)
            acc = t[0:VAUG] if acc is None else acc + t[0:VAUG]
            pieces.append(t[VAUG:VAUG + 2 * CMP_TILE_SLC])
        inv = jnp.where(m > 0.5 * NEG, 1.0 / acc[NSA_DH:NSA_DH + 1], 0.0)
        oc_ref[g, 0] = acc[0:NSA_DH] * inv
        blocks = []
        for b in range(nt):
            part = pieces[b][0:CMP_TILE_SLC]
            if b >= 1:
                part = part + pieces[b - 1][CMP_TILE_SLC:2 * CMP_TILE_SLC]
            blocks.append(part)
        imp4 = jnp.concatenate(blocks, axis=0) * inv
        imp = imp4[:, 0:Q_BLOCK]
        for r in range(1, NSA_REP):
            imp = imp + imp4[:, r * Q_BLOCK:(r + 1) * Q_BLOCK]
        return imp

    def both_groups(nt):
        nr = nt * CMP_TILE_SLC
        width = NSA_KV * Q_BLOCK
        scored = [tile_scores(nt, g) for g in range(NSA_KV)]
        imp = jnp.concatenate([importance(nt, g, *scored[g]) for g in range(NSA_KV)], axis=1)
        j = lax.broadcasted_iota(jnp.int32, (nr, width), 0)
        qq = lax.broadcasted_iota(jnp.int32, (nr, width), 1) & (Q_BLOCK - 1)
        cur = 2 * qb + jnp.where(qq >= SLC_BLOCK, 1, 0)
        forced = (j == 0) | (j == cur) | (j == cur - 1)
        imp = jnp.where(j > cur, -BIG, imp)
        imp = jnp.where(forced, PICKED, imp)
        jf = jf_ref[0:nr, :]
        for _ in range(max(min(N_SELECT, nslc) - 3, 0)):
            best = jnp.max(imp, axis=0, keepdims=True)
            first = jnp.min(jnp.where(imp == best, jf, float(nslc)), axis=0, keepdims=True)
            imp = jnp.where(jf == first, PICKED, imp)
        sel = imp == PICKED
        selb = jnp.where(sel, 0.0, NEG)
        ti = lax.broadcasted_iota(jnp.int32, (nsub, nr), 0) * SUB_BLOCKS
        tj = lax.broadcasted_iota(jnp.int32, (nsub, nr), 1)
        member = jnp.where((tj >= ti) & (tj < ti + SUB_BLOCKS), 1.0, 0.0).astype(BF16)
        count = _dot(member, jnp.where(sel, 1.0, 0.0).astype(BF16)).astype(BF16)
        for g in range(NSA_KV):
            lanes = slice(g * Q_BLOCK, (g + 1) * Q_BLOCK)
            selb_ref[g, 0, 0:nr, :] = selb[:, lanes]
            if nr < nslc:
                selb_ref[g, 0, nr:nslc, :] = jnp.full((nslc - nr, Q_BLOCK), NEG, F32)
            any_ref[g, 0] = lax.dot_general(jnp.ones((8, Q_BLOCK), BF16), count[:, lanes],
                                            (((1,), (1,)), ((), ())), preferred_element_type=F32)

    for k in range(-(-nct // CMP_VARIANT_STEP)):
        pl.when(td // CMP_VARIANT_STEP == k)(functools.partial(both_groups, min((k + 1) * CMP_VARIANT_STEP, nct)))


def _nsa_cmp(qt, kca, vcta, ovl, slopes, nslc):
    kv, nqb = qt.shape[0], qt.shape[1]
    nct = kca.shape[1]
    nsub = nslc // SUB_BLOCKS
    vo = jnp.concatenate([vcta, jnp.broadcast_to(ovl, (kv, nct) + ovl.shape)], axis=2)
    vo_rows = vo.shape[2]
    return pl.pallas_call(
        _nsa_cmp_kernel,
        out_shape=[
            jax.ShapeDtypeStruct((kv, nqb, NSA_DH, QCOLS), F32),
            jax.ShapeDtypeStruct((kv, nqb, nslc, Q_BLOCK), F32),
            jax.ShapeDtypeStruct((kv, nqb, 8, nsub), F32),
        ],
        grid=(nqb,),
        in_specs=[
            pl.BlockSpec((kv, 1, NSA_DH, QCOLS), lambda b: (0, b, 0, 0)),
            pl.BlockSpec((kv, nct, CMP_TILE, KAUG), lambda b: (0, 0, 0, 0)),
            pl.BlockSpec((kv, nct, vo_rows, CMP_TILE), lambda b: (0, 0, 0, 0)),
            pl.BlockSpec((kv, 1, QCOLS), lambda b: (0, 0, 0)),
        ],
        out_specs=[
            pl.BlockSpec((kv, 1, NSA_DH, QCOLS), lambda b: (0, b, 0, 0)),
            pl.BlockSpec((kv, 1, nslc, Q_BLOCK), lambda b: (0, b, 0, 0)),
            pl.BlockSpec((kv, 1, 8, nsub), lambda b: (0, b, 0, 0)),
        ],
        scratch_shapes=[pltpu.VMEM((CMP_TILE, QCOLS), F32), pltpu.VMEM((nslc, kv * Q_BLOCK), F32)],
        compiler_params=_cparams(("arbitrary",)),
        name="nsa_cmp_topk",
    )(qt, kca, vo, slopes)


def _nsa_attn_kernel(cnt_ref, lst_ref, qt_ref, ksa_ref, vsta_ref, kwa_ref, vwta_ref, selb_ref, oc_ref, gl_ref,
                     slope_ref, o_ref, addlo_ref, addhi_ref, z_ref, m_ref, acc_ref, car_ref, part_ref, gate_ref):
    g = pl.program_id(0)
    qb = pl.program_id(1)
    nqb = pl.num_programs(1)
    nsub = ksa_ref.shape[1]
    slope = slope_ref[0]

    @pl.when(qb == 0)
    def _():
        cl = lax.broadcasted_iota(jnp.int32, (SUB, QCOLS), 0)
        q = lax.broadcasted_iota(jnp.int32, (SUB, QCOLS), 1) & (Q_BLOCK - 1)
        addlo_ref[...] = jnp.where(cl <= q, 0.0, NEG)
        addhi_ref[...] = jnp.where(cl > q, 0.0, NEG)

    t0 = qb * Q_BLOCK
    q = qt_ref[0, 0] * (NSA_DH ** -0.5)
    r8 =lax.broadcasted_iota(jnp.int32, (SEL_TILE_BLOCKS, QCOLS), 0)
    slope_rows = jnp.where(r8 < 2, slope, 0.0)
    zero_rows = jnp.zeros((KAUG - COL_SEL - 16, QCOLS), BF16)
    q_win = jnp.concatenate([q, jnp.concatenate([jnp.zeros_like(slope_rows), slope_rows], axis=0).astype(BF16),
                             zero_rows], axis=0)

    def q_sel(a):
        grp = a // (SEL_TILE_BLOCKS // SUB_BLOCKS)
        sb = selb_ref[0, 0, pl.ds(pl.multiple_of(grp * SEL_TILE_BLOCKS, SEL_TILE_BLOCKS), SEL_TILE_BLOCKS), :]
        sb = jnp.concatenate([sb] * NSA_REP, axis=1)
        return jnp.concatenate([q, jnp.concatenate([sb, slope_rows], axis=0).astype(BF16), zero_rows], axis=0)

    def colmax(s):
        return jnp.max(s, axis=0, keepdims=True)

    def weigh(v_ref, subs, slots, shifts):
        pv = None
        for a, slot, shift in zip(subs, slots, shifts):
            p = jnp.exp(z_ref[slot] - shift).astype(BF16)
            t = _dot(v_ref[0, a], p)
            pv = t if pv is None else pv + t
        return pv

    first = qb - WINDOW // Q_BLOCK
    near = [jnp.maximum(first + i, 0) for i in range(WIN_TILES)]
    gone = [jnp.where(first + i >= 0, 0.0, -NEG) for i in range(WIN_TILES)]
    offs_s, offs_w, mg, mw = [], [], None, None
    for i in range(WIN_TILES):
        w = _dot(kwa_ref[0, near[i]], q_win)
        if i == 0:
            w = w + addhi_ref[...]
        if i == WIN_TILES - 1:
            w = w + addlo_ref[...]
        z_ref[WIN_TILES + i] = w
        off_w = gone[i] - slope * float(i * SUB)
        cw = colmax(w) - off_w
        mw = cw if mw is None else jnp.maximum(mw, cw)
        offs_w.append(off_w)
    for i in range(WIN_TILES):
        s = _dot(ksa_ref[0, near[i]], q_sel(near[i]))
        if i == WIN_TILES - 1:
            s = s + addlo_ref[...]
        z_ref[i] = s
        off_s = slope * float((WIN_TILES - 1 - i) * SUB) + gone[i]
        cs = colmax(s) - off_s
        mg = cs if mg is None else jnp.maximum(mg, cs)
        offs_s.append(off_s)
    pvw = weigh(vwta_ref, near, range(WIN_TILES, 2 * WIN_TILES), [mw + o for o in offs_w])
    o_win = pvw[0:NSA_DH] / pvw[NSA_DH:NSA_DH + 1]
    gl_t = gl_ref[...].T
    gates = [jax.nn.sigmoid(jnp.concatenate(
        [gl_t[r * N_NSA_BRANCH + br:r * N_NSA_BRANCH + br + 1, :] for r in range(NSA_REP)], axis=1))
        for br in range(N_NSA_BRANCH)]
    part_ref[...] = gates[0] * oc_ref[0, 0] + gates[2] * o_win
    gate_ref[...] = gates[1]

    base = (g * nqb + qb) * nsub
    count = cnt_ref[g * nqb + qb]

    def far_score(gi, half, u):
        e = gi * FAR_GROUP + u
        a = lst_ref[base + e]
        s = _dot(ksa_ref[0, a], q_sel(a))
        z_ref[FAR_SLOT + half * FAR_GROUP + u] = s
        off = slope * (t0 - a * SUB).astype(F32) + jnp.where(e < count, 0.0, -NEG)
        return colmax(s) - off, off

    def far_scores(gi, half):
        offs, mg = [], None
        for u in range(FAR_GROUP):
            cand, off = far_score(gi, half, u)
            mg = cand if mg is None else jnp.maximum(mg, cand)
            offs.append(off)
        return mg, tuple(offs)

    def keep(half, mg, offs):
        car_ref[half, 0:1, :] = mg
        for u in range(FAR_GROUP):
            car_ref[half, u + 1:u + 2, :] = offs[u]

    def far_stage(gi, half):
        mg = car_ref[half, 0:1, :]
        offs = [car_ref[half, u + 1:u + 2, :] for u in range(FAR_GROUP)]
        m_old = m_ref[...]
        m_new = jnp.maximum(m_old, mg)
        pv, nmg, noffs = None, None, []
        for u in range(FAR_GROUP):
            cand, off = far_score(gi + 1, 1 - half, u)
            nmg = cand if nmg is None else jnp.maximum(nmg, cand)
            noffs.append(off)
            a = lst_ref[base + gi * FAR_GROUP + u]
            t = weigh(vsta_ref, [a], [FAR_SLOT + half * FAR_GROUP + u], [m_new + offs[u]])
            pv = t if pv is None else pv + t
        acc_ref[...] = jnp.exp(m_old - m_new) * acc_ref[...] + pv
        m_ref[...] = m_new
        keep(1 - half, nmg, noffs)

    ngroups = (count + FAR_GROUP - 1) // FAR_GROUP

    def far_pair(pi, carry):
        far_stage(2 * pi, 0)

        @pl.when(2 * pi + 1 < ngroups)
        def _():
            far_stage(2 * pi + 1, 1)

        return carry

    keep(0, *far_scores(0, 0))
    m_ref[...] = mg
    acc_ref[...] = weigh(vsta_ref, near, range(WIN_TILES), [mg + o for o in offs_s])
    lax.fori_loop(0, (ngroups + 1) // 2, far_pair, 0)
    acc = acc_ref[...]
    o_sel = acc[0:NSA_DH] / acc[NSA_DH:NSA_DH + 1]

    out = part_ref[...] + gate_ref[...] * o_sel
    out_t = out.T
    heads = [out_t[r * Q_BLOCK:(r + 1) * Q_BLOCK, :] for r in range(NSA_REP)]
    o_ref[...] = jnp.concatenate(heads, axis=1).astype(o_ref.dtype)


def _nsa_attn(far_count, far_list, qt, ksa, vsta, kwa, vwta, selb, oc, gl, slopes):
    kv, nqb = qt.shape[0], qt.shape[1]
    nsub = ksa.shape[1]
    nslc = selb.shape[2]
    blk = lambda g, b, c, l: (g, b, 0, 0)
    grp = lambda g, b, c, l: (g, 0, 0, 0)
    grid_spec = pltpu.PrefetchScalarGridSpec(
        num_scalar_prefetch=2,
        grid=(kv, nqb),
        in_specs=[
            pl.BlockSpec((1, 1, NSA_DH, QCOLS), blk),
            pl.BlockSpec((1, nsub, SUB, KAUG), grp),
            pl.BlockSpec((1, nsub, VAUG, SUB), grp),
            pl.BlockSpec((1, nsub, SUB, KAUG), grp),
            pl.BlockSpec((1, nsub, VAUG, SUB), grp),
            pl.BlockSpec((1, 1, nslc, Q_BLOCK), blk),
            pl.BlockSpec((1, 1, NSA_DH, QCOLS), blk),
            pl.BlockSpec((Q_BLOCK, KAUG), lambda g, b, c, l: (b, g)),
            pl.BlockSpec((1, 1, QCOLS), lambda g, b, c, l: (g, 0, 0)),
        ],
        out_specs=pl.BlockSpec((Q_BLOCK, NSA_REP * NSA_DH), lambda g, b, c, l: (b, g)),
        scratch_shapes=[
            pltpu.VMEM((SUB, QCOLS), F32),
            pltpu.VMEM((SUB, QCOLS), F32),
            pltpu.VMEM((FAR_SLOT + 2 * FAR_GROUP, SUB, QCOLS), F32),
            pltpu.VMEM((1, QCOLS), F32),
            pltpu.VMEM((VAUG, QCOLS), F32),
            pltpu.VMEM((2, 8, QCOLS), F32),
            pltpu.VMEM((NSA_DH, QCOLS), F32),
            pltpu.VMEM((1, QCOLS), F32),
        ],
    )
    return pl.pallas_call(
        _nsa_attn_kernel,
        out_shape=jax.ShapeDtypeStruct((nsub * SUB, NSA_HEADS * NSA_DH), BF16),
        grid_spec=grid_spec,
        compiler_params=_cparams(("arbitrary", "arbitrary")),
        name="nsa_attn",
    )(far_count, far_list, qt, ksa, vsta, kwa, vwta, selb, oc, gl, slopes)


def _far_lists(picks, nqb):
    nsub = picks.shape[-1]
    a = jnp.arange(nsub)[None, None, :]
    far = a < (jnp.arange(nqb)[None, :, None] - WINDOW // Q_BLOCK)
    active = (picks > 0) & far
    count = jnp.sum(active, axis=-1).astype(jnp.int32)
    order = jnp.argsort(jnp.where(active, 0, 1), axis=-1, stable=True).astype(jnp.int32)
    lst = jnp.where(a < count[..., None], order, 0).reshape(-1)
    return count.reshape(-1), jnp.concatenate([lst, jnp.zeros((2 * FAR_GROUP,), jnp.int32)])


def _merge_kernel(x_ref, g_ref, ya_ref, yb_ref, yc_ref, wg_ref, bg_ref, wb_ref, wo_ref, o_ref):
    x = x_ref[...]
    h = _rms(x, g_ref[...]).astype(BF16)
    mix = None
    for mi, y_ref in enumerate((ya_ref, yb_ref, yc_ref)):
        cs = slice(mi * D_MODEL, (mi + 1) * D_MODEL)
        gate = jax.nn.sigmoid(_dot(h, wg_ref[:, cs]) + bg_ref[:, cs])
        term = gate * _dot(y_ref[...], wb_ref[mi])
        mix = term if mix is None else mix + term
    o_ref[...] = x + _dot(mix.astype(BF16), wo_ref[...])


def _merge(x, g, ya, yb, yc, wg, bg, wb, wo, layer, tm=512):
    s = x.shape[0]
    row = lambda i: (i, 0)
    fixed2 = lambda i: (0, 0)
    return pl.pallas_call(
        _merge_kernel,
        out_shape=jax.ShapeDtypeStruct((s, D_MODEL), F32),
        grid=(s // tm,),
        in_specs=[
            pl.BlockSpec((tm, D_MODEL), row),
            pl.BlockSpec((1, D_MODEL), fixed2),
            pl.BlockSpec((tm, MIX_WIDTH), row),
            pl.BlockSpec((tm, MIX_WIDTH), row),
            pl.BlockSpec((tm, MIX_WIDTH), row),
            pl.BlockSpec((None, D_MODEL, N_MIXERS * D_MODEL), lambda i: (layer, 0, 0), pipeline_mode=pl.Buffered(1)),
            pl.BlockSpec((1, N_MIXERS * D_MODEL), fixed2),
            pl.BlockSpec((None, N_MIXERS, MIX_WIDTH, D_MODEL), lambda i: (layer, 0, 0, 0), pipeline_mode=pl.Buffered(1)),
            pl.BlockSpec((None, D_MODEL, D_MODEL), lambda i: (layer, 0, 0), pipeline_mode=pl.Buffered(1)),
        ],
        out_specs=pl.BlockSpec((tm, D_MODEL), row),
        compiler_params=_cparams(("arbitrary",)),
        name="merge",
    )(x, g, ya, yb, yc, wg, bg, wb, wo)


def _alibi_slope_cols():
    h = jnp.arange(1, NSA_HEADS + 1, dtype=F32)
    slopes = (2.0 ** (-8.0 * h / NSA_HEADS)).reshape(NSA_KV, NSA_REP)
    return jnp.repeat(slopes, Q_BLOCK, axis=1).reshape(NSA_KV, 1, QCOLS)


def _overlap_local():
    cl = jnp.arange(CMP_TILE)[None, :]
    jl = jnp.arange(2 * CMP_TILE_SLC)[:, None]
    ov = (cl * CMP_STRIDE < (jl + 1) * SLC_BLOCK) & (cl * CMP_STRIDE + CMP_LEN > jl * SLC_BLOCK)
    return ov.astype(BF16)


def _nsa(pcv, pgl, ksa, kwa, qt, vsta, vwta, cmp_k, cmp_v, layer):
    s = pcv.shape[0]
    nqb = s // Q_BLOCK
    slopes = _alibi_slope_cols()
    kca = _compress(pcv, *cmp_k, layer, False)
    vcta = _compress(pcv, *cmp_v, layer, True)
    oc, selb, picks = _nsa_cmp(qt, kca, vcta, _overlap_local(), slopes, s // SLC_BLOCK)
    far_count, far_list = _far_lists(picks[:, :, 0, :], nqb)
    ksa = ksa.reshape(NSA_KV, s // SUB, SUB, KAUG)
    kwa = kwa.reshape(NSA_KV, s // SUB, SUB, KAUG)
    return _nsa_attn(far_count, far_list, qt, ksa, vsta, kwa, vwta, selb, oc, pgl, slopes)


def kernel(x, ffn1_norm, ffn1_w1, ffn1_w2, mix_norm, w_in, gm_ln_g, gm_ln_b, gm_ws, gm_bs, ret_gn_g, ret_gn_b,
           cmp_pos, cmp_w1, cmp_w2, w_branch_out, w_merge_gate, b_merge_gate, w_o, ffn2_norm, ffn2_w1, ffn2_w2,
           final_norm):
    bsz, s, d = x.shape
    depth = ffn1_w1.shape[0]
    assert d == D_MODEL and s % CMP_TILE_TOKENS == 0 and (s // 512) % depth == 0, x.shape
    row = lambda v: v.reshape(1, -1)
    fin = row(final_norm)
    first_w1, first_w2 = _to_bf16(ffn1_w1, 1), _to_bf16(ffn1_w2, 1)
    later = (ffn1_w1, ffn1_w2, ffn2_w1, ffn2_w2, jnp.swapaxes(w_in, 1, 2), w_merge_gate,
             w_branch_out.reshape(depth, N_MIXERS * MIX_WIDTH, D_MODEL), w_o)
    converted = None
    kcs, kcw, vones = _proj_constants()
    cmp_k = _compress_weights(cmp_pos[:, 0], cmp_w1[:, 0], cmp_w2[:, 0], False)
    cmp_v = _compress_weights(cmp_pos[:, 1], cmp_w1[:, 1], cmp_w2[:, 1], True)
    outs = []
    for b in range(bsz):
        xb = x.reshape(s, D_MODEL) if bsz == 1 else x[b]
        for l in range(depth):
            if converted is None:
                xb, *converted = _ffn(xb, row(ffn1_norm[l]), first_w1, first_w2, 0, fin, False, cast=later)
                f1w1, f1w2, f2w1, f2w2, w_in_b, wg_b, wb_b, wo_b = converted
                wb_b = wb_b.reshape(w_branch_out.shape)
                wn, wt = _proj_weights(w_in_b)
            else:
                xb = _ffn(xb, row(ffn1_norm[l]), f1w1, f1w2, l, fin, False)
            pa, pcv, pgl, ksa, kwa, qt, vsta, vwta, rkt = _proj(xb, row(mix_norm[l]), wn, wt, l, kcs, kcw, vones)
            y_a = _gmlp(pa, row(gm_ln_g[l]), row(gm_ln_b[l]), gm_ws[l], gm_bs[l].T)
            y_b = _retention(pa, rkt, row(ret_gn_g[l]), row(ret_gn_b[l]))
            y_c = _nsa(pcv, pgl, ksa, kwa, qt, vsta, vwta, cmp_k, cmp_v, l)
            xb = _merge(xb, row(mix_norm[l]), y_a, y_b, y_c, wg_b, row(b_merge_gate[l]), wb_b, wo_b, l)
            xb = _ffn(xb, row(ffn2_norm[l]), f2w1, f2w2, l, fin, l == depth - 1)
        outs.append(xb)
    return outs[0].reshape(1, s, D_MODEL) if bsz == 1 else jnp.stack(outs)
```

```python
import functools
import math

import jax
import jax.numpy as jnp
from jax import lax
from jax.experimental import pallas as pl
from jax.experimental.pallas import tpu as pltpu

F32 = jnp.float32
BF16 = jnp.bfloat16

D_MODEL = 1024
D_FF = 2816
NORM_EPS = 1e-6
FFN_RES_WEIGHT = 0.5
GM_WIDTH = 512
GM_GROUPS = 4
GM_CHUNK = 128
RET_HEADS = 4
RET_DK = 64
RET_DV = 128
RET_CHUNK = 128
NSA_HEADS = 8
NSA_KV = 2
NSA_REP = NSA_HEADS // NSA_KV
NSA_DH = 64
CMP_LEN = 32
CMP_STRIDE = 16
CMP_HIDDEN = 128
SLC_BLOCK = 64
N_SELECT = 16
WINDOW = 512
Q_BLOCK = 128
N_MIXERS = 3
MIX_WIDTH = 512
BIG = 1e9
NEG = -1e30
PICKED = -(2.0 ** 127)

OFF_GM_U, OFF_GM_V, OFF_RQ, OFF_RK, OFF_RV, OFF_RG, OFF_NQ, OFF_NKV, OFF_NG = (
    0, 512, 1024, 1280, 1536, 2048, 2560, 3072, 3840)

QCOLS = NSA_REP * Q_BLOCK
SEL_TILE = 512
SEL_TILE_BLOCKS = SEL_TILE // SLC_BLOCK
SUB = 128
SUB_BLOCKS = SUB // SLC_BLOCK
FAR_GROUP = 4
KAUG = 128
COL_SEL = NSA_DH
COL_POS = NSA_DH + SEL_TILE_BLOCKS
VAUG = 80
WIN_TILES = (WINDOW + Q_BLOCK) // SUB
FAR_SLOT = 2 * WIN_TILES
CMP_TILE = 128
CMP_TILE_TOKENS = CMP_TILE * CMP_STRIDE
CMP_TILE_SLC = CMP_TILE_TOKENS // SLC_BLOCK
CMP_VARIANT_STEP = 1
N_NSA_BRANCH = 3
KVW = NSA_KV * NSA_DH
PA_RV, PA_RG, PA_RQ = 0, 512, 1024
PA_WIDTH = PA_RQ + RET_HEADS * RET_DK
WN_GM_U, WN_GM_V, WN_PA = 0, GM_WIDTH, 2 * GM_WIDTH
WN_CV = WN_PA + PA_WIDTH
WN_GL = WN_CV + 2 * KVW
WN_KS = WN_GL + NSA_KV * KAUG
WN_WIDTH = WN_KS + 2 * NSA_KV * KAUG
WT_RK = NSA_HEADS * NSA_DH + 2 * KVW
WT_ROWS = WT_RK + RET_HEADS * RET_DK
VMEM_LIMIT = 56 * 1024 * 1024
CAST_BLOCK_BYTES = 4 * 1024 * 1024


def _cparams(sem, vmem=VMEM_LIMIT):
    return pltpu.CompilerParams(dimension_semantics=sem, vmem_limit_bytes=vmem)


def _cast_kernel(w_ref, o_ref):
    o_ref[...] = w_ref[...].astype(o_ref.dtype)


def _to_bf16(w, layers):
    _, r, c = w.shape
    nl = layers
    tr = 16
    while r % (2 * tr) == 0 and 2 * tr * c * 4 <= CAST_BLOCK_BYTES:
        tr *= 2
    return pl.pallas_call(
        _cast_kernel,
        out_shape=jax.ShapeDtypeStruct((nl, r, c), BF16),
        grid=(nl, r // tr),
        in_specs=[pl.BlockSpec((None, tr, c), lambda l, i: (l, i, 0))],
        out_specs=pl.BlockSpec((None, tr, c), lambda l, i: (l, i, 0)),
        compiler_params=_cparams(("arbitrary", "arbitrary")),
        name="cast_bf16",
    )(w)


def _cast_along(nsteps, w):
    nl, r, c = w.shape
    per_layer = nsteps // nl
    tr = -(-r // (16 * per_layer)) * 16
    assert nl * per_layer == nsteps and tr * (per_layer - 1) < r, (w.shape, nsteps)
    return pl.BlockSpec((None, tr, c), lambda i: (i // per_layer, i % per_layer, 0))


def _rms(x, g):
    return x * lax.rsqrt(jnp.mean(x * x, axis=-1, keepdims=True) + NORM_EPS) * g


def _dot(a, b):
    return jnp.dot(a, b, preferred_element_type=F32)


def _ffn_kernel(x_ref, g_ref, w1_ref, w2_ref, fin_ref, *rest, final_norm, tf, ncast):
    cast_in, o_ref, cast_out, act_ref = rest[:ncast], rest[ncast], rest[ncast + 1:2 * ncast + 1], rest[-1]
    for src, dst in zip(cast_in, cast_out):
        dst[...] = src[...].astype(dst.dtype)
    x = x_ref[...]
    h = _rms(x, g_ref[...]).astype(BF16)
    for f in range(D_FF // tf):
        a = _dot(h, w1_ref[:, f * tf:(f + 1) * tf])
        b = _dot(h, w1_ref[:, D_FF + f * tf:D_FF + (f + 1) * tf])
        act_ref[:, f * tf:(f + 1) * tf] = (jax.nn.silu(a) * b).astype(BF16)
    y = x + FFN_RES_WEIGHT * _dot(act_ref[...], w2_ref[...])
    if final_norm:
        y = _rms(y, fin_ref[...])
    o_ref[...] = y


def _ffn(x, g, w1, w2, layer, fin, final_norm, cast=(), tm=512, tf=256):
    s = x.shape[0]
    nsteps = s // tm
    fixed = lambda i: (0, 0)
    resident = pl.Buffered(1)
    cast_specs = [_cast_along(nsteps, w) for w in cast]
    outs = pl.pallas_call(
        functools.partial(_ffn_kernel, final_norm=final_norm, tf=tf, ncast=len(cast)),
        out_shape=[jax.ShapeDtypeStruct((s, D_MODEL), F32)] + [jax.ShapeDtypeStruct(w.shape, BF16) for w in cast],
        grid=(nsteps,),
        in_specs=[
            pl.BlockSpec((tm, D_MODEL), lambda i: (i, 0)),
            pl.BlockSpec((1, D_MODEL), fixed),
            pl.BlockSpec((None, D_MODEL, 2 * D_FF), lambda i: (layer, 0, 0), pipeline_mode=resident),
            pl.BlockSpec((None, D_FF, D_MODEL), lambda i: (layer, 0, 0), pipeline_mode=resident),
            pl.BlockSpec((1, D_MODEL), fixed),
        ] + cast_specs,
        out_specs=[pl.BlockSpec((tm, D_MODEL), lambda i: (i, 0))] + cast_specs,
        scratch_shapes=[pltpu.VMEM((tm, D_FF), BF16)],
        compiler_params=_cparams(("arbitrary",)),
        name="ffn",
    )(x, g, w1, w2, fin, *cast)
    return outs[0] if not cast else tuple(outs)


def _gmlp_norm(v, lg_ref, lb_ref):
    v = jax.nn.gelu(v)
    mu = jnp.mean(v, axis=-1, keepdims=True)
    vc = v - mu
    var = jnp.mean(vc * vc, axis=-1, keepdims=True)
    return (vc * lax.rsqrt(var + NORM_EPS) * lg_ref[...] + lb_ref[...]).astype(BF16)


def _gmlp_gate(u, vn, ws_ref, bst_ref):
    cg = GM_WIDTH // GM_GROUPS
    row = lax.broadcasted_iota(jnp.int32, (GM_CHUNK, GM_CHUNK), 0)
    col = lax.broadcasted_iota(jnp.int32, (GM_CHUNK, GM_CHUNK), 1)
    causal = row >= col
    bst = bst_ref[...]
    groups = []
    for gi in range(GM_GROUPS):
        w = jnp.where(causal, ws_ref[gi], 0.0).astype(BF16)
        bias = bst[:, gi:gi + 1]
        cs = slice(gi * cg, (gi + 1) * cg)
        chunks = [u[c:c + GM_CHUNK, cs] * (_dot(w, vn[c:c + GM_CHUNK, cs]) + bias)
                  for c in range(0, u.shape[0], GM_CHUNK)]
        groups.append(jnp.concatenate(chunks, axis=0))
    return jnp.concatenate(groups, axis=1)


def _proj_kernel(x_ref, g_ref, wn_ref, wt_ref, kcs_ref, kcw_ref, vones_ref, lg_ref, lb_ref, ws_ref, bst_ref,
                 gg_ref, gb_ref, ya_ref, yb_ref, pcv_ref, pgl_ref, ksa_ref, kwa_ref, qt_ref, vsta_ref, vwta_ref,
                 st_ref):
    @pl.when(pl.program_id(0) == 0)
    def _():
        st_ref[...] = jnp.zeros_like(st_ref)

    tm = x_ref.shape[0]
    h = _rms(x_ref[...], g_ref[...]).astype(BF16)
    u = _dot(h, wn_ref[:, WN_GM_U:WN_GM_U + GM_WIDTH])
    v = _dot(h, wn_ref[:, WN_GM_V:WN_GM_V + GM_WIDTH])
    hv = RET_HEADS * RET_DV
    ret_v = _dot(h, wn_ref[:, WN_PA + PA_RV:WN_PA + PA_RV + hv])
    ret_g = _dot(h, wn_ref[:, WN_PA + PA_RG:WN_PA + PA_RG + hv])
    ret_q = _dot(h, wn_ref[:, WN_PA + PA_RQ:WN_PA + PA_WIDTH])
    u = jax.nn.gelu(u)
    pcv_ref[...] = _dot(h, wn_ref[:, WN_CV:WN_CV + 2 * KVW])
    pgl_ref[...] = _dot(h, wn_ref[:, WN_GL:WN_GL + 2 * KAUG])
    kk = _dot(h, wn_ref[:, WN_KS:WN_KS + 4 * KAUG])
    vn = _gmlp_norm(v, lg_ref, lb_ref)
    for g in range(NSA_KV):
        ksa_ref[g] = (kk[:, g * KAUG:(g + 1) * KAUG] + kcs_ref[...]).astype(BF16)
        kwa_ref[g] = (kk[:, (NSA_KV + g) * KAUG:(NSA_KV + g + 1) * KAUG] + kcw_ref[...]).astype(BF16)
    pt = lax.dot_general(wt_ref[...], h, (((1,), (1,)), ((), ())), preferred_element_type=F32)
    ya_ref[...] = _gmlp_gate(u, vn, ws_ref, bst_ref).astype(ya_ref.dtype)
    yb_ref[...] = _retention_mixer(ret_q, pt[WT_RK:WT_ROWS, :], ret_v, ret_g, gg_ref, gb_ref,
                                   st_ref).astype(yb_ref.dtype)
    nq = NSA_HEADS * NSA_DH
    for g in range(NSA_KV):
        for b in range(tm // Q_BLOCK):
            for r in range(NSA_REP):
                hd = g * NSA_REP + r
                qt_ref[g, b, :, r * Q_BLOCK:(r + 1) * Q_BLOCK] = (
                    pt[hd * NSA_DH:(hd + 1) * NSA_DH, b * Q_BLOCK:(b + 1) * Q_BLOCK].astype(BF16))
            for v_ref, first_row in ((vsta_ref, nq), (vwta_ref, nq + KVW)):
                rows = slice(first_row + g * NSA_DH, first_row + (g + 1) * NSA_DH)
                v_ref[g, b, 0:NSA_DH, :] = pt[rows, b * SUB:(b + 1) * SUB].astype(BF16)
                v_ref[g, b, NSA_DH:VAUG, :] = vones_ref[...]


def _proj(x, g, wn, wt, layer, kcs, kcw, vones, gm_ln_g, gm_ln_b, gm_ws, gm_bst, ret_gn_g, ret_gn_b, tm=512):
    s = x.shape[0]
    fixed = lambda i: (0, 0)
    resident = pl.Buffered(1)
    hv = RET_HEADS * RET_DV
    return pl.pallas_call(
        _proj_kernel,
        out_shape=[
            jax.ShapeDtypeStruct((s, GM_WIDTH), BF16),
            jax.ShapeDtypeStruct((s, hv), BF16),
            jax.ShapeDtypeStruct((s, 2 * KVW), F32),
            jax.ShapeDtypeStruct((s, 2 * KAUG), F32),
            jax.ShapeDtypeStruct((NSA_KV, s, KAUG), BF16),
            jax.ShapeDtypeStruct((NSA_KV, s, KAUG), BF16),
            jax.ShapeDtypeStruct((NSA_KV, s // Q_BLOCK, NSA_DH, QCOLS), BF16),
            jax.ShapeDtypeStruct((NSA_KV, s // SUB, VAUG, SUB), BF16),
            jax.ShapeDtypeStruct((NSA_KV, s // SUB, VAUG, SUB), BF16),
        ],
        grid=(s // tm,),
        in_specs=[
            pl.BlockSpec((tm, D_MODEL), lambda i: (i, 0)),
            pl.BlockSpec((1, D_MODEL), fixed),
            pl.BlockSpec((None, D_MODEL, WN_WIDTH), lambda i: (layer, 0, 0), pipeline_mode=resident),
            pl.BlockSpec((None, WT_ROWS, D_MODEL), lambda i: (layer, 0, 0), pipeline_mode=resident),
            pl.BlockSpec((tm, KAUG), fixed),
            pl.BlockSpec((tm, KAUG), fixed),
            pl.BlockSpec((VAUG - NSA_DH, SUB), fixed),
            pl.BlockSpec((1, GM_WIDTH), fixed),
            pl.BlockSpec((1, GM_WIDTH), fixed),
            pl.BlockSpec((GM_GROUPS, GM_CHUNK, GM_CHUNK), lambda i: (0, 0, 0)),
            pl.BlockSpec((GM_CHUNK, GM_GROUPS), fixed),
            pl.BlockSpec((1, hv), fixed),
            pl.BlockSpec((1, hv), fixed),
        ],
        out_specs=[
            pl.BlockSpec((tm, GM_WIDTH), lambda i: (i, 0)),
            pl.BlockSpec((tm, hv), lambda i: (i, 0)),
            pl.BlockSpec((tm, 2 * KVW), lambda i: (i, 0)),
            pl.BlockSpec((tm, 2 * KAUG), lambda i: (i, 0)),
            pl.BlockSpec((NSA_KV, tm, KAUG), lambda i: (0, i, 0)),
            pl.BlockSpec((NSA_KV, tm, KAUG), lambda i: (0, i, 0)),
            pl.BlockSpec((NSA_KV, tm // Q_BLOCK, NSA_DH, QCOLS), lambda i: (0, i, 0, 0)),
            pl.BlockSpec((NSA_KV, tm // SUB, VAUG, SUB), lambda i: (0, i, 0, 0)),
            pl.BlockSpec((NSA_KV, tm // SUB, VAUG, SUB), lambda i: (0, i, 0, 0)),
        ],
        scratch_shapes=[pltpu.VMEM((RET_HEADS, RET_DK, RET_DV), F32)],
        compiler_params=_cparams(("arbitrary",)),
        name="proj",
    )(x, g, wn, wt, kcs, kcw, vones, gm_ln_g, gm_ln_b, gm_ws, gm_bst, ret_gn_g, ret_gn_b)


def _proj_weights(w_in_t):
    w = w_in_t
    z = lambda n: jnp.zeros((w.shape[0], n, D_MODEL), BF16)
    kv = lambda i, g: w[:, OFF_NKV + i * KVW + g * NSA_DH:OFF_NKV + i * KVW + (g + 1) * NSA_DH]
    ngl = N_NSA_BRANCH * NSA_REP
    rows = [w[:, OFF_GM_U:OFF_RQ], w[:, OFF_RV:OFF_NQ], w[:, OFF_RQ:OFF_RK],
            w[:, OFF_NKV:OFF_NKV + 2 * KVW]]
    for g in range(NSA_KV):
        rows += [w[:, OFF_NG + g * ngl:OFF_NG + (g + 1) * ngl], z(KAUG - ngl)]
    for i in (2, 4):
        for g in range(NSA_KV):
            rows += [kv(i, g), z(KAUG - NSA_DH)]
    wn = jnp.swapaxes(jnp.concatenate(rows, axis=1), 1, 2)
    wt = jnp.concatenate([w[:, OFF_NQ:OFF_NQ + NSA_HEADS * NSA_DH], kv(3, 0), kv(3, 1), kv(5, 0), kv(5, 1),
                          w[:, OFF_RK:OFF_RV]], axis=1)
    return wn, wt


def _proj_constants(tm=512):
    pos = jnp.arange(tm)
    zero = jnp.zeros((tm, NSA_DH), F32)

    def pos_cols(p):
        return jnp.stack([p // 16 * 16, p % 16], axis=1).astype(F32)

    onehot = (pos[:, None] // SLC_BLOCK % SEL_TILE_BLOCKS == jnp.arange(SEL_TILE_BLOCKS)[None, :]).astype(F32)
    tail = jnp.zeros((tm, KAUG - COL_POS - 2), F32)
    kcs = jnp.concatenate([zero, onehot, pos_cols(pos % SUB), tail], axis=1)
    kcw = jnp.concatenate([zero, jnp.zeros_like(onehot), pos_cols(pos % SUB), tail], axis=1)
    r = jnp.arange(VAUG - NSA_DH)[:, None]
    vones = jnp.broadcast_to(jnp.where(r == 0, 1.0, 0.0), (VAUG - NSA_DH, SUB)).astype(BF16)
    return kcs, kcw, vones


def _retention_mixer(q_all, kt_all, v_all, g_all, gg_ref, gb_ref, st_ref):
    c = RET_CHUNK
    ri = lax.broadcasted_iota(jnp.int32, (c, c), 0)
    ci = lax.broadcasted_iota(jnp.int32, (c, c), 1)
    diff = (ri - ci).astype(F32)
    pos_k = lax.broadcasted_iota(jnp.int32, (RET_DK, c), 1).astype(F32)
    pos_v = lax.broadcasted_iota(jnp.int32, (c, RET_DV), 0).astype(F32)
    heads = []
    for h in range(RET_HEADS):
        log_gamma = math.log(1.0 - 2.0 ** (-5.0 - h))
        intra_decay = jnp.where(diff >= 0, jnp.exp(log_gamma * jnp.maximum(diff, 0.0)), 0.0)
        k_decay = jnp.exp(log_gamma * (c - 1.0 - pos_k))
        q_decay = jnp.exp(log_gamma * (pos_v + 1.0))
        chunk_decay = math.exp(log_gamma * c)
        gam = gg_ref[:, h * RET_DV:(h + 1) * RET_DV]
        bet = gb_ref[:, h * RET_DV:(h + 1) * RET_DV]
        outs = []
        for n in range(q_all.shape[0] // c):
            rs = slice(n * c, (n + 1) * c)
            q = q_all[rs, h * RET_DK:(h + 1) * RET_DK].astype(BF16)
            kt = kt_all[h * RET_DK:(h + 1) * RET_DK, rs] * (RET_DK ** -0.5)
            v = v_all[rs, h * RET_DV:(h + 1) * RET_DV].astype(BF16)
            scores = _dot(q, kt.astype(BF16)) * intra_decay
            intra = _dot(scores.astype(BF16), v)
            state = st_ref[h]
            cross = _dot(q, state.astype(BF16)) * q_decay
            kv = _dot((kt * k_decay).astype(BF16), v)
            st_ref[h] = state * chunk_decay + kv
            y = intra + cross
            mu = jnp.mean(y, axis=-1, keepdims=True)
            yc = y - mu
            var = jnp.mean(yc * yc, axis=-1, keepdims=True)
            yn = yc * lax.rsqrt(var + NORM_EPS) * gam + bet
            gate = g_all[rs, h * RET_DV:(h + 1) * RET_DV]
            outs.append(jax.nn.silu(gate) * yn)
        heads.append(jnp.concatenate(outs, axis=0))
    return jnp.concatenate(heads, axis=1)


def _compress_kernel(t_ref, pos_ref, w1_ref, w2_ref, o_ref, xlo_ref, xhi_ref, *, values):
    n = t_ref.shape[0] // CMP_STRIDE
    for i in range(CMP_STRIDE):
        x = t_ref[pl.ds(i, n, stride=CMP_STRIDE), :]
        xlo_ref[:, i * KVW:(i + 1) * KVW] = (x + pos_ref[0:1, i * KVW:(i + 1) * KVW]).astype(BF16)
        xhi_ref[:, i * KVW:(i + 1) * KVW] = (x + pos_ref[1:2, i * KVW:(i + 1) * KVW]).astype(BF16)
    lo = _dot(xlo_ref[...], w1_ref[0])
    hi = _dot(xhi_ref[...], w1_ref[1])
    hi_next = jnp.concatenate([hi[1:], hi[:1]], axis=0)
    rowi = lax.broadcasted_iota(jnp.int32, lo.shape, 0)
    hid = jnp.where(rowi < n - 1, jax.nn.gelu(lo + hi_next), 0.0).astype(BF16)
    for g in range(NSA_KV):
        if values:
            vt = lax.dot_general(w2_ref[g], hid, (((1,), (1,)), ((), ())), preferred_element_type=F32)
            r = lax.broadcasted_iota(jnp.int32, vt.shape, 0)
            vt = jnp.where(r == NSA_DH, 1.0, vt).astype(BF16)
            for c in range(n // CMP_TILE):
                o_ref[g, c] = vt[:, c * CMP_TILE:(c + 1) * CMP_TILE]
        else:
            k = _dot(hid, w2_ref[g])
            col = lax.broadcasted_iota(jnp.int32, k.shape, 1)
            blk = (lax.broadcasted_iota(jnp.int32, k.shape, 0) & (CMP_TILE - 1)) * CMP_STRIDE
            k = jnp.where(col == NSA_DH, blk.astype(F32), k).astype(BF16)
            for c in range(n // CMP_TILE):
                o_ref[g, c] = k[c * CMP_TILE:(c + 1) * CMP_TILE, :]


def _compress(pcv, posrows, w1, w2, layer, values):
    s = pcv.shape[0]
    n = s // CMP_STRIDE
    nct = n // CMP_TILE
    out_tile = (VAUG, CMP_TILE) if values else (CMP_TILE, KAUG)
    width = CMP_STRIDE * KVW
    return pl.pallas_call(
        functools.partial(_compress_kernel, values=values),
        out_shape=jax.ShapeDtypeStruct((NSA_KV, nct) + out_tile, BF16),
        grid=(1,),
        in_specs=[
            pl.BlockSpec((s, KVW), lambda i: (0, 1 if values else 0)),
            pl.BlockSpec((None, 2, width), lambda i: (layer, 0, 0)),
            pl.BlockSpec((None, 2, width, NSA_KV * CMP_HIDDEN), lambda i: (layer, 0, 0, 0)),
            pl.BlockSpec((None, NSA_KV) + w2.shape[2:], lambda i: (layer, 0, 0, 0)),
        ],
        out_specs=pl.BlockSpec((NSA_KV, nct) + out_tile, lambda i: (0, 0, 0, 0)),
        scratch_shapes=[pltpu.VMEM((n, width), BF16), pltpu.VMEM((n, width), BF16)],
        compiler_params=_cparams(("arbitrary",)),
        name="nsa_compress_v" if values else "nsa_compress_k",
    )(pcv, posrows, w1, w2)


def _compress_weights(pos, w1, w2, values):
    nl = pos.shape[0]
    posrows = jnp.tile(pos.reshape(nl, 2, CMP_STRIDE, 1, NSA_DH), (1, 1, 1, NSA_KV, 1)).reshape(nl, 2, CMP_STRIDE * KVW)
    w = w1.astype(BF16).reshape(nl, 2, CMP_STRIDE, NSA_DH, CMP_HIDDEN)
    eye = jnp.eye(NSA_KV, dtype=BF16)
    w1b = jnp.einsum('lhidc,gk->lhigdkc', w, eye).reshape(nl, 2, CMP_STRIDE * KVW, NSA_KV * CMP_HIDDEN)
    w2g = jnp.einsum('lcd,gk->lgkcd', w2.astype(BF16), eye).reshape(nl, NSA_KV, NSA_KV * CMP_HIDDEN, NSA_DH)
    if values:
        w2g = jnp.pad(w2g.transpose(0, 1, 3, 2), ((0, 0), (0, 0), (0, VAUG - NSA_DH), (0, 0)))
    else:
        w2g = jnp.pad(w2g, ((0, 0), (0, 0), (0, 0), (0, KAUG - NSA_DH)))
    return posrows, w1b, w2g


def _nsa_cmp_kernel(qt_ref, kca_ref, vo_ref, slope_ref, oc_ref, selb_ref, any_ref, dmask_ref, jf_ref):
    qb = pl.program_id(0)
    nct = kca_ref.shape[1]
    nslc = selb_ref.shape[2]
    nsub = nslc // SUB_BLOCKS

    @pl.when(qb == 0)
    def _():
        cl = lax.broadcasted_iota(jnp.int32, (CMP_TILE, QCOLS), 0)
        q = lax.broadcasted_iota(jnp.int32, (CMP_TILE, QCOLS), 1) & (Q_BLOCK - 1)
        dmask_ref[...] = (cl * CMP_STRIDE - q).astype(F32)
        jf_ref[...] = lax.broadcasted_iota(jnp.int32, jf_ref.shape, 0).astype(F32)

    t0 = qb * Q_BLOCK
    r16 = lax.broadcasted_iota(jnp.int32, (16, QCOLS), 0)
    td = t0 // CMP_TILE_TOKENS

    def colmax(s):
        return jnp.max(s, axis=0, keepdims=True)

    def tile_scores(nt, g):
        slope = slope_ref[g]
        qa = jnp.concatenate([qt_ref[g, 0] * (NSA_DH ** -0.5), jnp.where(r16 == 0, slope, 0.0).astype(BF16),
                              jnp.zeros((KAUG - NSA_DH - 16, QCOLS), BF16)], axis=0)
        offs = [slope * (t0 - ti * CMP_TILE_TOKENS).astype(F32) for ti in range(nt)]
        zs, m = [], None
        for ti in range(nt):
            s = _dot(kca_ref[g, ti], qa)
            if ti >= nt - CMP_VARIANT_STEP - 1:
                lim = (t0 - ti * CMP_TILE_TOKENS - (CMP_LEN - 1)).astype(F32)
                s = jnp.where(dmask_ref[...] <= lim, s, NEG)
            zs.append(s)
            cand = colmax(s) - offs[ti]
            m = cand if m is None else jnp.maximum(m, cand)
        return zs, m, offs

    def importance(nt, g, zs, m, offs):
        acc, pieces = None, []
        for ti in range(nt):
            p = jnp.exp(zs[ti] - (m + offs[ti])).astype(BF16)
            t = _dot(vo_ref[g, ti], p)
            acc = t[0:VAUG] if acc is None else acc + t[0:VAUG]
            pieces.append(t[VAUG:VAUG + 2 * CMP_TILE_SLC])
        inv = jnp.where(m > 0.5 * NEG, 1.0 / acc[NSA_DH:NSA_DH + 1], 0.0)
        oc_ref[g, 0] = acc[0:NSA_DH] * inv
        blocks = []
        for b in range(nt):
            part = pieces[b][0:CMP_TILE_SLC]
            if b >= 1:
                part = part + pieces[b - 1][CMP_TILE_SLC:2 * CMP_TILE_SLC]
            blocks.append(part)
        imp4 = jnp.concatenate(blocks, axis=0) * inv
        imp = imp4[:, 0:Q_BLOCK]
        for r in range(1, NSA_REP):
            imp = imp + imp4[:, r * Q_BLOCK:(r + 1) * Q_BLOCK]
        return imp

    def both_groups(nt):
        nr = nt * CMP_TILE_SLC
        width = NSA_KV * Q_BLOCK
        scored = [tile_scores(nt, g) for g in range(NSA_KV)]
        imp = jnp.concatenate([importance(nt, g, *scored[g]) for g in range(NSA_KV)], axis=1)
        j = lax.broadcasted_iota(jnp.int32, (nr, width), 0)
        qq = lax.broadcasted_iota(jnp.int32, (nr, width), 1) & (Q_BLOCK - 1)
        cur = 2 * qb + jnp.where(qq >= SLC_BLOCK, 1, 0)
        forced = (j == 0) | (j == cur) | (j == cur - 1)
        imp = jnp.where(j > cur, -BIG, imp)
        imp = jnp.where(forced, PICKED, imp)
        jf = jf_ref[0:nr, :]
        for _ in range(max(min(N_SELECT, nslc) - 3, 0)):
            best = jnp.max(imp, axis=0, keepdims=True)
            first = jnp.min(jnp.where(imp == best, jf, float(nslc)), axis=0, keepdims=True)
            imp = jnp.where(jf == first, PICKED, imp)
        sel = imp == PICKED
        selb = jnp.where(sel, 0.0, NEG)
        ti = lax.broadcasted_iota(jnp.int32, (nsub, nr), 0) * SUB_BLOCKS
        tj = lax.broadcasted_iota(jnp.int32, (nsub, nr), 1)
        member = jnp.where((tj >= ti) & (tj < ti + SUB_BLOCKS), 1.0, 0.0).astype(BF16)
        count = _dot(member, jnp.where(sel, 1.0, 0.0).astype(BF16)).astype(BF16)
        for g in range(NSA_KV):
            lanes = slice(g * Q_BLOCK, (g + 1) * Q_BLOCK)
            selb_ref[g, 0, 0:nr, :] = selb[:, lanes]
            if nr < nslc:
                selb_ref[g, 0, nr:nslc, :] = jnp.full((nslc - nr, Q_BLOCK), NEG, F32)
            any_ref[g, 0] = lax.dot_general(jnp.ones((8, Q_BLOCK), BF16), count[:, lanes],
                                            (((1,), (1,)), ((), ())), preferred_element_type=F32)

    for k in range(-(-nct // CMP_VARIANT_STEP)):
        pl.when(td // CMP_VARIANT_STEP == k)(functools.partial(both_groups, min((k + 1) * CMP_VARIANT_STEP, nct)))


def _nsa_cmp(qt, kca, vcta, ovl, slopes, nslc):
    kv, nqb = qt.shape[0], qt.shape[1]
    nct = kca.shape[1]
    nsub = nslc // SUB_BLOCKS
    vo = jnp.concatenate([vcta, jnp.broadcast_to(ovl, (kv, nct) + ovl.shape)], axis=2)
    vo_rows = vo.shape[2]
    return pl.pallas_call(
        _nsa_cmp_kernel,
        out_shape=[
            jax.ShapeDtypeStruct((kv, nqb, NSA_DH, QCOLS), F32),
            jax.ShapeDtypeStruct((kv, nqb, nslc, Q_BLOCK), F32),
            jax.ShapeDtypeStruct((kv, nqb, 8, nsub), F32),
        ],
        grid=(nqb,),
        in_specs=[
            pl.BlockSpec((kv, 1, NSA_DH, QCOLS), lambda b: (0, b, 0, 0)),
            pl.BlockSpec((kv, nct, CMP_TILE, KAUG), lambda b: (0, 0, 0, 0)),
            pl.BlockSpec((kv, nct, vo_rows, CMP_TILE), lambda b: (0, 0, 0, 0)),
            pl.BlockSpec((kv, 1, QCOLS), lambda b: (0, 0, 0)),
        ],
        out_specs=[
            pl.BlockSpec((kv, 1, NSA_DH, QCOLS), lambda b: (0, b, 0, 0)),
            pl.BlockSpec((kv, 1, nslc, Q_BLOCK), lambda b: (0, b, 0, 0)),
            pl.BlockSpec((kv, 1, 8, nsub), lambda b: (0, b, 0, 0)),
        ],
        scratch_shapes=[pltpu.VMEM((CMP_TILE, QCOLS), F32), pltpu.VMEM((nslc, kv * Q_BLOCK), F32)],
        compiler_params=_cparams(("arbitrary",)),
        name="nsa_cmp_topk",
    )(qt, kca, vo, slopes)


def _nsa_attn_kernel(cnt_ref, lst_ref, qt_ref, ksa_ref, vsta_ref, kwa_ref, vwta_ref, selb_ref, oc_ref, gl_ref,
                     slope_ref, o_ref, addlo_ref, addhi_ref, z_ref, m_ref, acc_ref, car_ref, part_ref, gate_ref):
    g = pl.program_id(0)
    qb = pl.program_id(1)
    nqb = pl.num_programs(1)
    nsub = ksa_ref.shape[1]
    slope = slope_ref[0]

    @pl.when(qb == 0)
    def _():
        cl = lax.broadcasted_iota(jnp.int32, (SUB, QCOLS), 0)
        q = lax.broadcasted_iota(jnp.int32, (SUB, QCOLS), 1) & (Q_BLOCK - 1)
        addlo_ref[...] = jnp.where(cl <= q, 0.0, NEG)
        addhi_ref[...] = jnp.where(cl > q, 0.0, NEG)

    t0 = qb * Q_BLOCK
    q = qt_ref[0, 0] * (NSA_DH ** -0.5)
    r8 =lax.broadcasted_iota(jnp.int32, (SEL_TILE_BLOCKS, QCOLS), 0)
    slope_rows = jnp.where(r8 < 2, slope, 0.0)
    zero_rows = jnp.zeros((KAUG - COL_SEL - 16, QCOLS), BF16)
    q_win = jnp.concatenate([q, jnp.concatenate([jnp.zeros_like(slope_rows), slope_rows], axis=0).astype(BF16),
                             zero_rows], axis=0)

    def q_sel(a):
        grp = a // (SEL_TILE_BLOCKS // SUB_BLOCKS)
        sb = selb_ref[0, 0, pl.ds(pl.multiple_of(grp * SEL_TILE_BLOCKS, SEL_TILE_BLOCKS), SEL_TILE_BLOCKS), :]
        sb = jnp.concatenate([sb] * NSA_REP, axis=1)
        return jnp.concatenate([q, jnp.concatenate([sb, slope_rows], axis=0).astype(BF16), zero_rows], axis=0)

    def colmax(s):
        return jnp.max(s, axis=0, keepdims=True)

    def weigh(v_ref, subs, slots, shifts):
        pv = None
        for a, slot, shift in zip(subs, slots, shifts):
            p = jnp.exp(z_ref[slot] - shift).astype(BF16)
            t = _dot(v_ref[0, a], p)
            pv = t if pv is None else pv + t
        return pv

    first = qb - WINDOW // Q_BLOCK
    near = [jnp.maximum(first + i, 0) for i in range(WIN_TILES)]
    gone = [jnp.where(first + i >= 0, 0.0, -NEG) for i in range(WIN_TILES)]
    offs_s, offs_w, mg, mw = [], [], None, None
    for i in range(WIN_TILES):
        w = _dot(kwa_ref[0, near[i]], q_win)
        if i == 0:
            w = w + addhi_ref[...]
        if i == WIN_TILES - 1:
            w = w + addlo_ref[...]
        z_ref[WIN_TILES + i] = w
        off_w = gone[i] - slope * float(i * SUB)
        cw = colmax(w) - off_w
        mw = cw if mw is None else jnp.maximum(mw, cw)
        offs_w.append(off_w)
    for i in range(WIN_TILES):
        s = _dot(ksa_ref[0, near[i]], q_sel(near[i]))
        if i == WIN_TILES - 1:
            s = s + addlo_ref[...]
        z_ref[i] = s
        off_s = slope * float((WIN_TILES - 1 - i) * SUB) + gone[i]
        cs = colmax(s) - off_s
        mg = cs if mg is None else jnp.maximum(mg, cs)
        offs_s.append(off_s)
    pvw = weigh(vwta_ref, near, range(WIN_TILES, 2 * WIN_TILES), [mw + o for o in offs_w])
    o_win = pvw[0:NSA_DH] / pvw[NSA_DH:NSA_DH + 1]
    gl_t = gl_ref[...].T
    gates = [jax.nn.sigmoid(jnp.concatenate(
        [gl_t[r * N_NSA_BRANCH + br:r * N_NSA_BRANCH + br + 1, :] for r in range(NSA_REP)], axis=1))
        for br in range(N_NSA_BRANCH)]
    part_ref[...] = gates[0] * oc_ref[0, 0] + gates[2] * o_win
    gate_ref[...] = gates[1]

    base = (g * nqb + qb) * nsub
    count = cnt_ref[g * nqb + qb]

    def far_score(gi, half, u):
        e = gi * FAR_GROUP + u
        a = lst_ref[base + e]
        s = _dot(ksa_ref[0, a], q_sel(a))
        z_ref[FAR_SLOT + half * FAR_GROUP + u] = s
        off = slope * (t0 - a * SUB).astype(F32) + jnp.where(e < count, 0.0, -NEG)
        return colmax(s) - off, off

    def far_scores(gi, half):
        offs, mg = [], None
        for u in range(FAR_GROUP):
            cand, off = far_score(gi, half, u)
            mg = cand if mg is None else jnp.maximum(mg, cand)
            offs.append(off)
        return mg, tuple(offs)

    def keep(half, mg, offs):
        car_ref[half, 0:1, :] = mg
        for u in range(FAR_GROUP):
            car_ref[half, u + 1:u + 2, :] = offs[u]

    def far_stage(gi, half):
        mg = car_ref[half, 0:1, :]
        offs = [car_ref[half, u + 1:u + 2, :] for u in range(FAR_GROUP)]
        m_old = m_ref[...]
        m_new = jnp.maximum(m_old, mg)
        pv, nmg, noffs = None, None, []
        for u in range(FAR_GROUP):
            cand, off = far_score(gi + 1, 1 - half, u)
            nmg = cand if nmg is None else jnp.maximum(nmg, cand)
            noffs.append(off)
            a = lst_ref[base + gi * FAR_GROUP + u]
            t = weigh(vsta_ref, [a], [FAR_SLOT + half * FAR_GROUP + u], [m_new + offs[u]])
            pv = t if pv is None else pv + t
        acc_ref[...] = jnp.exp(m_old - m_new) * acc_ref[...] + pv
        m_ref[...] = m_new
        keep(1 - half, nmg, noffs)

    ngroups = (count + FAR_GROUP - 1) // FAR_GROUP

    def far_pair(pi, carry):
        far_stage(2 * pi, 0)

        @pl.when(2 * pi + 1 < ngroups)
        def _():
            far_stage(2 * pi + 1, 1)

        return carry

    keep(0, *far_scores(0, 0))
    m_ref[...] = mg
    acc_ref[...] = weigh(vsta_ref, near, range(WIN_TILES), [mg + o for o in offs_s])
    lax.fori_loop(0, (ngroups + 1) // 2, far_pair, 0)
    acc = acc_ref[...]
    o_sel = acc[0:NSA_DH] / acc[NSA_DH:NSA_DH + 1]

    out = part_ref[...] + gate_ref[...] * o_sel
    out_t = out.T
    heads = [out_t[r * Q_BLOCK:(r + 1) * Q_BLOCK, :] for r in range(NSA_REP)]
    o_ref[...] = jnp.concatenate(heads, axis=1).astype(o_ref.dtype)


def _nsa_attn(far_count, far_list, qt, ksa, vsta, kwa, vwta, selb, oc, gl, slopes):
    kv, nqb = qt.shape[0], qt.shape[1]
    nsub = ksa.shape[1]
    nslc = selb.shape[2]
    blk = lambda g, b, c, l: (g, b, 0, 0)
    grp = lambda g, b, c, l: (g, 0, 0, 0)
    grid_spec = pltpu.PrefetchScalarGridSpec(
        num_scalar_prefetch=2,
        grid=(kv, nqb),
        in_specs=[
            pl.BlockSpec((1, 1, NSA_DH, QCOLS), blk),
            pl.BlockSpec((1, nsub, SUB, KAUG), grp),
            pl.BlockSpec((1, nsub, VAUG, SUB), grp),
            pl.BlockSpec((1, nsub, SUB, KAUG), grp),
            pl.BlockSpec((1, nsub, VAUG, SUB), grp),
            pl.BlockSpec((1, 1, nslc, Q_BLOCK), blk),
            pl.BlockSpec((1, 1, NSA_DH, QCOLS), blk),
            pl.BlockSpec((Q_BLOCK, KAUG), lambda g, b, c, l: (b, g)),
            pl.BlockSpec((1, 1, QCOLS), lambda g, b, c, l: (g, 0, 0)),
        ],
        out_specs=pl.BlockSpec((Q_BLOCK, NSA_REP * NSA_DH), lambda g, b, c, l: (b, g)),
        scratch_shapes=[
            pltpu.VMEM((SUB, QCOLS), F32),
            pltpu.VMEM((SUB, QCOLS), F32),
            pltpu.VMEM((FAR_SLOT + 2 * FAR_GROUP, SUB, QCOLS), F32),
            pltpu.VMEM((1, QCOLS), F32),
            pltpu.VMEM((VAUG, QCOLS), F32),
            pltpu.VMEM((2, 8, QCOLS), F32),
            pltpu.VMEM((NSA_DH, QCOLS), F32),
            pltpu.VMEM((1, QCOLS), F32),
        ],
    )
    return pl.pallas_call(
        _nsa_attn_kernel,
        out_shape=jax.ShapeDtypeStruct((nsub * SUB, NSA_HEADS * NSA_DH), BF16),
        grid_spec=grid_spec,
        compiler_params=_cparams(("arbitrary", "arbitrary")),
        name="nsa_attn",
    )(far_count, far_list, qt, ksa, vsta, kwa, vwta, selb, oc, gl, slopes)


def _far_lists(picks, nqb):
    nsub = picks.shape[-1]
    a = jnp.arange(nsub)[None, None, :]
    far = a < (jnp.arange(nqb)[None, :, None] - WINDOW // Q_BLOCK)
    active = (picks > 0) & far
    count = jnp.sum(active, axis=-1).astype(jnp.int32)
    order = jnp.argsort(jnp.where(active, 0, 1), axis=-1, stable=True).astype(jnp.int32)
    lst = jnp.where(a < count[..., None], order, 0).reshape(-1)
    return count.reshape(-1), jnp.concatenate([lst, jnp.zeros((2 * FAR_GROUP,), jnp.int32)])


def _merge_kernel(x_ref, g_ref, ya_ref, yb_ref, yc_ref, wg_ref, bg_ref, wb_ref, wo_ref, o_ref):
    x = x_ref[...]
    h = _rms(x, g_ref[...]).astype(BF16)
    mix = None
    for mi, y_ref in enumerate((ya_ref, yb_ref, yc_ref)):
        cs = slice(mi * D_MODEL, (mi + 1) * D_MODEL)
        gate = jax.nn.sigmoid(_dot(h, wg_ref[:, cs]) + bg_ref[:, cs])
        term = gate * _dot(y_ref[...], wb_ref[mi])
        mix = term if mix is None else mix + term
    o_ref[...] = x + _dot(mix.astype(BF16), wo_ref[...])


def _merge(x, g, ya, yb, yc, wg, bg, wb, wo, layer, tm=512):
    s = x.shape[0]
    row = lambda i: (i, 0)
    fixed2 = lambda i: (0, 0)
    return pl.pallas_call(
        _merge_kernel,
        out_shape=jax.ShapeDtypeStruct((s, D_MODEL), F32),
        grid=(s // tm,),
        in_specs=[
            pl.BlockSpec((tm, D_MODEL), row),
            pl.BlockSpec((1, D_MODEL), fixed2),
            pl.BlockSpec((tm, MIX_WIDTH), row),
            pl.BlockSpec((tm, MIX_WIDTH), row),
            pl.BlockSpec((tm, MIX_WIDTH), row),
            pl.BlockSpec((None, D_MODEL, N_MIXERS * D_MODEL), lambda i: (layer, 0, 0), pipeline_mode=pl.Buffered(1)),
            pl.BlockSpec((1, N_MIXERS * D_MODEL), fixed2),
            pl.BlockSpec((None, N_MIXERS, MIX_WIDTH, D_MODEL), lambda i: (layer, 0, 0, 0), pipeline_mode=pl.Buffered(1)),
            pl.BlockSpec((None, D_MODEL, D_MODEL), lambda i: (layer, 0, 0), pipeline_mode=pl.Buffered(1)),
        ],
        out_specs=pl.BlockSpec((tm, D_MODEL), row),
        compiler_params=_cparams(("arbitrary",)),
        name="merge",
    )(x, g, ya, yb, yc, wg, bg, wb, wo)


def _alibi_slope_cols():
    h = jnp.arange(1, NSA_HEADS + 1, dtype=F32)
    slopes = (2.0 ** (-8.0 * h / NSA_HEADS)).reshape(NSA_KV, NSA_REP)
    return jnp.repeat(slopes, Q_BLOCK, axis=1).reshape(NSA_KV, 1, QCOLS)


def _overlap_local():
    cl = jnp.arange(CMP_TILE)[None, :]
    jl = jnp.arange(2 * CMP_TILE_SLC)[:, None]
    ov = (cl * CMP_STRIDE < (jl + 1) * SLC_BLOCK) & (cl * CMP_STRIDE + CMP_LEN > jl * SLC_BLOCK)
    return ov.astype(BF16)


def _nsa(pcv, pgl, ksa, kwa, qt, vsta, vwta, cmp_k, cmp_v, layer):
    s = pcv.shape[0]
    nqb = s // Q_BLOCK
    slopes = _alibi_slope_cols()
    kca = _compress(pcv, *cmp_k, layer, False)
    vcta = _compress(pcv, *cmp_v, layer, True)
    oc, selb, picks = _nsa_cmp(qt, kca, vcta, _overlap_local(), slopes, s // SLC_BLOCK)
    far_count, far_list = _far_lists(picks[:, :, 0, :], nqb)
    ksa = ksa.reshape(NSA_KV, s // SUB, SUB, KAUG)
    kwa = kwa.reshape(NSA_KV, s // SUB, SUB, KAUG)
    return _nsa_attn(far_count, far_list, qt, ksa, vsta, kwa, vwta, selb, oc, pgl, slopes)


def kernel(x, ffn1_norm, ffn1_w1, ffn1_w2, mix_norm, w_in, gm_ln_g, gm_ln_b, gm_ws, gm_bs, ret_gn_g, ret_gn_b,
           cmp_pos, cmp_w1, cmp_w2, w_branch_out, w_merge_gate, b_merge_gate, w_o, ffn2_norm, ffn2_w1, ffn2_w2,
           final_norm):
    bsz, s, d = x.shape
    depth = ffn1_w1.shape[0]
    assert d == D_MODEL and s % CMP_TILE_TOKENS == 0 and (s // 512) % depth == 0, x.shape
    row = lambda v: v.reshape(1, -1)
    fin = row(final_norm)
    first_w1, first_w2 = _to_bf16(ffn1_w1, 1), _to_bf16(ffn1_w2, 1)
    later = (ffn1_w1, ffn1_w2, ffn2_w1, ffn2_w2, jnp.swapaxes(w_in, 1, 2), w_merge_gate,
             w_branch_out.reshape(depth, N_MIXERS * MIX_WIDTH, D_MODEL), w_o)
    converted = None
    kcs, kcw, vones = _proj_constants()
    cmp_k = _compress_weights(cmp_pos[:, 0], cmp_w1[:, 0], cmp_w2[:, 0], False)
    cmp_v = _compress_weights(cmp_pos[:, 1], cmp_w1[:, 1], cmp_w2[:, 1], True)
    outs = []
    for b in range(bsz):
        xb = x.reshape(s, D_MODEL) if bsz == 1 else x[b]
        for l in range(depth):
            if converted is None:
                xb, *converted = _ffn(xb, row(ffn1_norm[l]), first_w1, first_w2, 0, fin, False, cast=later)
                f1w1, f1w2, f2w1, f2w2, w_in_b, wg_b, wb_b, wo_b = converted
                wb_b = wb_b.reshape(w_branch_out.shape)
                wn, wt = _proj_weights(w_in_b)
            else:
                xb = _ffn(xb, row(ffn1_norm[l]), f1w1, f1w2, l, fin, False)
            y_a, y_b, pcv, pgl, ksa, kwa, qt, vsta, vwta = _proj(
                xb, row(mix_norm[l]), wn, wt, l, kcs, kcw, vones,
                row(gm_ln_g[l]), row(gm_ln_b[l]), gm_ws[l], gm_bs[l].T, row(ret_gn_g[l]), row(ret_gn_b[l]))
            y_c = _nsa(pcv, pgl, ksa, kwa, qt, vsta, vwta, cmp_k, cmp_v, l)
            xb = _merge(xb, row(mix_norm[l]), y_a, y_b, y_c, wg_b, row(b_merge_gate[l]), wb_b, wo_b, l)
            xb = _ffn(xb, row(ffn2_norm[l]), f2w1, f2w2, l, fin, l == depth - 1)
        outs.append(xb)
    return outs[0].reshape(1, s, D_MODEL) if bsz == 1 else jnp.stack(outs)
```
